```python
import jax, jax.numpy as jnp
from jax import lax
import numpy as np


D_MODEL = 2048
BATCH = 4
SEQ = 2048
DEPTH = 2
DEC_BATCH = 128
DEC_SEQ = 1
PAST_LEN = 16384
PAGE_SIZE = 128

D_MIX_A = D_MODEL // 2
A_HEADS = 8
A_HEAD_DIM = D_MIX_A // A_HEADS
CHUNK = 128
D_MIX_B = D_MODEL // 2
POOL_WINDOWS = (2, 4, 8, 16)
POOL_GROUPS = 4
POOL_GROUP_DIM = D_MIX_B // POOL_GROUPS
POOL_BUF = 15
D_MIX_C = D_MODEL // 2
CONV_WIDTH = 31
CONV_BUF = CONV_WIDTH - 1
N_BRANCH = 3
N_IN = 2 * D_MIX_A + D_MIX_B + 2 * D_MIX_C + N_BRANCH * D_MODEL
D_FF = ((8 * D_MODEL // 3 + 255) // 256) * 256
LN_EPS = 1e-5

kernel_name = "gated_hybrid_chunkmlp_pool_conformer_step"


def layer_norm(x, g, b):
    xf = x.astype(jnp.float32)
    mu = jnp.mean(xf, axis=-1, keepdims=True)
    xc = xf - mu
    var = jnp.mean(xc * xc, axis=-1, keepdims=True)
    y = xc * lax.rsqrt(var + LN_EPS) * g.astype(jnp.float32) + b.astype(jnp.float32)
    return y.astype(x.dtype)


def chunk_spatial_mix(v, ws, bs):
    b, t, c = v.shape
    n_chunks = -(-t // CHUNK)
    pad = n_chunks * CHUNK - t
    vp = jnp.pad(v, ((0, 0), (0, pad), (0, 0))).reshape(b, n_chunks, CHUNK, A_HEADS, A_HEAD_DIM)
    causal = jnp.tril(jnp.ones((CHUNK, CHUNK), dtype=bool))
    wm = jnp.where(causal[None], ws, 0).astype(v.dtype)
    s = jnp.einsum('hij,bcjhd->bcihd', wm, vp) + jnp.transpose(bs)[None, None, :, :, None]
    return s.reshape(b, n_chunks * CHUNK, c)[:, :t]


def multiscale_pool(xb, prefix, w_group, scale):
    b, t, _ = xb.shape
    p_len = prefix.shape[1]
    ext = jnp.concatenate([prefix, xb], axis=1)
    extf = ext.astype(jnp.float32)
    cs = jnp.pad(jnp.cumsum(extf, axis=1), ((0, 0), (1, 0), (0, 0)))
    pos = jnp.arange(p_len, p_len + t)
    parts = []
    for g, w in enumerate(POOL_WINDOWS):
        lo = jnp.maximum(pos - w + 1, 0)
        cnt = (pos - lo + 1).astype(jnp.float32)
        sl = slice(g * POOL_GROUP_DIM, (g + 1) * POOL_GROUP_DIM)
        win = cs[:, pos + 1, sl] - cs[:, lo, sl]
        parts.append(win / cnt[None, :, None])
    pooled = (jnp.concatenate(parts, axis=-1) - extf[:, p_len:]).astype(xb.dtype)
    pg = pooled.reshape(b, t, POOL_GROUPS, POOL_GROUP_DIM)
    mixed = jnp.einsum('btgc,gcd->btgd', pg, w_group).reshape(b, t, D_MIX_B)
    return mixed * scale, ext[:, -POOL_BUF:]


def causal_depthwise_conv(c, prefix, w_dw, b_dw):
    ext = jnp.concatenate([prefix, c], axis=1)
    out = lax.conv_general_dilated(
        ext, w_dw[:, None, :].astype(ext.dtype), window_strides=(1,), padding='VALID',
        dimension_numbers=('NWC', 'WIO', 'NWC'), feature_group_count=D_MIX_C)
    return out + b_dw, ext[:, -CONV_BUF:]


def decoder_layer(x, pool_prefix, conv_prefix, alpha,
                  w_in, b_in, a_ln_g, a_ln_b, a_ws, a_bs, w_a_out,
                  b_w_group, b_scale, w_b_out,
                  c_w_dw, c_b_dw, c_ln_g, c_ln_b, w_c_out,
                  w_out, ln1_g, ln1_b, w_ffn_up, w_ffn_down, ln2_g, ln2_b):
    h = jnp.einsum('btd,dn->btn', x, w_in) + b_in
    s1 = 2 * D_MIX_A
    s2 = s1 + D_MIX_B
    s3 = s2 + 2 * D_MIX_C
    h_a, h_b, h_c, h_g = h[..., :s1], h[..., s1:s2], h[..., s2:s3], h[..., s3:]
    u, v = jnp.split(jax.nn.gelu(h_a, approximate=False), 2, axis=-1)
    v = layer_norm(v, a_ln_g, a_ln_b)
    y_a = jnp.einsum('btc,cd->btd', u * chunk_spatial_mix(v, a_ws, a_bs), w_a_out)
    pooled, pool_rows = multiscale_pool(h_b, pool_prefix, b_w_group, b_scale)
    y_b = jnp.einsum('btc,cd->btd', pooled, w_b_out)
    c_val, c_gate = jnp.split(h_c, 2, axis=-1)
    conv, conv_rows = causal_depthwise_conv(c_val * jax.nn.sigmoid(c_gate), conv_prefix, c_w_dw, c_b_dw)
    y_c = jnp.einsum('btc,cd->btd', jax.nn.silu(layer_norm(conv, c_ln_g, c_ln_b)), w_c_out)
    gates = jax.nn.sigmoid(h_g).reshape(h_g.shape[:-1] + (N_BRANCH, D_MODEL))
    merged = gates[..., 0, :] * y_a + gates[..., 1, :] * y_b + gates[..., 2, :] * y_c
    mix = jnp.einsum('btc,cd->btd', merged, w_out)
    x = layer_norm(alpha * x + mix, ln1_g, ln1_b)
    f_gate, f_up = jnp.split(jnp.einsum('btd,df->btf', x, w_ffn_up), 2, axis=-1)
    ffn = jnp.einsum('btf,fd->btd', jax.nn.silu(f_gate) * f_up, w_ffn_down)
    x = layer_norm(alpha * x + ffn, ln2_g, ln2_b)
    return x, pool_rows, conv_rows, v


def setup_inputs(seed: int = 0) -> dict:
    key = jax.random.key(seed)
    ks = jax.random.split(key, 26)
    beta = (8.0 * DEPTH) ** -0.25

    def nrm(k, shape, scale):
        return jax.random.normal(k, shape, jnp.float32) * scale

    def gain(k, shape):
        return 1.0 + nrm(k, shape, 0.02)

    L = DEPTH
    return {
        "x_prompt": nrm(ks[0], (BATCH, SEQ, D_MODEL), 1.0),
        "x_sample": nrm(ks[1], (DEC_BATCH, DEC_SEQ, D_MODEL), 1.0),
        "state_pool": nrm(ks[2], (L, DEC_BATCH, POOL_BUF, D_MIX_B), 1.0),
        "state_conv": nrm(ks[3], (L, DEC_BATCH, CONV_BUF, D_MIX_C), 1.0),
        "w_in": nrm(ks[4], (L, D_MODEL, N_IN), D_MODEL ** -0.5),
        "b_in": nrm(ks[5], (L, N_IN), 0.02),
        "a_ln_g": gain(ks[6], (L, D_MIX_A)),
        "a_ln_b": nrm(ks[7], (L, D_MIX_A), 0.02),
        "a_ws": nrm(ks[8], (L, A_HEADS, CHUNK, CHUNK), 0.5 * CHUNK ** -0.5),
        "a_bs": 1.0 + nrm(ks[9], (L, A_HEADS, CHUNK), 0.1),
        "w_a_out": nrm(ks[10], (L, D_MIX_A, D_MODEL), beta * D_MIX_A ** -0.5),
        "b_w_group": nrm(ks[11], (L, POOL_GROUPS, POOL_GROUP_DIM, POOL_GROUP_DIM), POOL_GROUP_DIM ** -0.5),
        "b_scale": 1.0 + nrm(ks[12], (L, D_MIX_B), 0.1),
        "w_b_out": nrm(ks[13], (L, D_MIX_B, D_MODEL), beta * D_MIX_B ** -0.5),
        "c_w_dw": nrm(ks[14], (L, CONV_WIDTH, D_MIX_C), CONV_WIDTH ** -0.5),
        "c_b_dw": nrm(ks[15], (L, D_MIX_C), 0.02),
        "c_ln_g": gain(ks[16], (L, D_MIX_C)),
        "c_ln_b": nrm(ks[17], (L, D_MIX_C), 0.02),
        "w_c_out": nrm(ks[18], (L, D_MIX_C, D_MODEL), beta * D_MIX_C ** -0.5),
        "w_out": nrm(ks[19], (L, D_MODEL, D_MODEL), beta * D_MODEL ** -0.5),
        "ln1_g": gain(ks[20], (L, D_MODEL)),
        "ln1_b": nrm(ks[21], (L, D_MODEL), 0.02),
        "w_ffn_up": nrm(ks[22], (L, D_MODEL, 2 * D_FF), D_MODEL ** -0.5),
        "w_ffn_down": nrm(ks[23], (L, D_FF, D_MODEL), beta * D_FF ** -0.5),
        "ln2_g": gain(ks[24], (L, D_MODEL)),
        "ln2_b": nrm(ks[25], (L, D_MODEL), 0.02),
    }


def reference(x_prompt, x_sample, state_pool, state_conv,
              w_in, b_in, a_ln_g, a_ln_b, a_ws, a_bs, w_a_out,
              b_w_group, b_scale, w_b_out,
              c_w_dw, c_b_dw, c_ln_g, c_ln_b, w_c_out,
              w_out, ln1_g, ln1_b, w_ffn_up, w_ffn_down, ln2_g, ln2_b):
    alpha = (2.0 * DEPTH) ** 0.25
    n_prompt = x_prompt.shape[0]
    yp = x_prompt
    ys = x_sample
    pool_p, conv_p, pool_s, conv_s, v_s = [], [], [], [], []
    for l in range(DEPTH):
        lw = (w_in[l], b_in[l], a_ln_g[l], a_ln_b[l], a_ws[l], a_bs[l], w_a_out[l],
              b_w_group[l], b_scale[l], w_b_out[l],
              c_w_dw[l], c_b_dw[l], c_ln_g[l], c_ln_b[l], w_c_out[l],
              w_out[l], ln1_g[l], ln1_b[l], w_ffn_up[l], w_ffn_down[l], ln2_g[l], ln2_b[l])
        pool0 = jnp.zeros((n_prompt, 0, D_MIX_B), yp.dtype)
        conv0 = jnp.zeros((n_prompt, CONV_BUF, D_MIX_C), yp.dtype)
        yp, pr, cr, _ = decoder_layer(yp, pool0, conv0, alpha, *lw)
        ys, ps, cs, vs = decoder_layer(ys, state_pool[l].astype(ys.dtype), state_conv[l].astype(ys.dtype), alpha, *lw)
        pool_p.append(pr)
        conv_p.append(cr)
        pool_s.append(ps)
        conv_s.append(cs)
        v_s.append(vs)
    new_pool_prompt = jnp.stack(pool_p, axis=0)
    new_conv_prompt = jnp.stack(conv_p, axis=0)
    new_pool_sample = jnp.stack(pool_s, axis=0)
    new_conv_sample = jnp.stack(conv_s, axis=0)
    new_chunk_v_sample = jnp.stack(v_s, axis=0)
    return (yp, ys, new_pool_prompt, new_conv_prompt, new_pool_sample, new_conv_sample, new_chunk_v_sample)
```

```python
import functools

import jax
import jax.numpy as jnp
from jax import lax
from jax.experimental import pallas as pl
from jax.experimental.pallas import tpu as pltpu

F32 = jnp.float32
BF16 = jnp.bfloat16

LN_EPS = 1e-5
A_HEADS = 8
CHUNK = 128
POOL_WINDOWS = (2, 4, 8, 16)
POOL_BUF = 15
CONV_WIDTH = 31
CONV_BUF = CONV_WIDTH - 1
N_BRANCH = 3

POOL_HALO = 16
CONV_HALO = 32
ROW_BLOCK = 32
SUBLANES = 8
LANES = 128
V7X_VMEM_BYTES = 64 * 1024 * 1024


def _cparams(semantics, vmem_mb):
    assert vmem_mb * 1024 * 1024 < V7X_VMEM_BYTES
    return pltpu.CompilerParams(dimension_semantics=semantics,
                                vmem_limit_bytes=vmem_mb * 1024 * 1024)


def _layer_norm(x, g, b):
    mu = jnp.mean(x, axis=-1, keepdims=True)
    xc = x - mu
    var = jnp.mean(xc * xc, axis=-1, keepdims=True)
    return xc * lax.rsqrt(var + LN_EPS) * g + b


def _gelu_exact(x):
    return 0.5 * x * (1.0 + lax.erf(x * (0.5 ** 0.5)))


def _sigmoid(x):
    return 1.0 / (1.0 + jnp.exp(-x))


def _bdot(a, w):
    return jnp.dot(a, w.astype(BF16), preferred_element_type=F32)


def _proj_kernel(x_ref, w_ref, b_ref, o_ref, *, act):
    h = _bdot(x_ref[...], w_ref[...]) + b_ref[...]
    if act == "gelu":
        h = _gelu_exact(h)
    elif act == "sigmoid":
        h = _sigmoid(h)
    o_ref[...] = h.astype(o_ref.dtype)


def _proj_nobias_kernel(x_ref, w_ref, o_ref):
    o_ref[...] = _bdot(x_ref[...], w_ref[...]).astype(o_ref.dtype)


def _glu_kernel(x_ref, wv_ref, wg_ref, bv_ref, bg_ref, o_ref):
    x = x_ref[...]
    val = _bdot(x, wv_ref[...]) + bv_ref[...]
    gate = _bdot(x, wg_ref[...]) + bg_ref[...]
    o_ref[...] = val * _sigmoid(gate)


def _proj(x, w, b, layer, col0, ncols, act, out_dtype, tm, tn, name):
    m_rows, k = x.shape
    off = col0 // tn
    return pl.pallas_call(
        functools.partial(_proj_kernel, act=act),
        grid=(m_rows // tm, ncols // tn),
        in_specs=[
            pl.BlockSpec((tm, k), lambda m, n: (m, 0)),
            pl.BlockSpec((None, k, tn), lambda m, n: (layer, 0, off + n)),
            pl.BlockSpec((None, 1, tn), lambda m, n: (layer, 0, off + n)),
        ],
        out_specs=pl.BlockSpec((tm, tn), lambda m, n: (m, n)),
        out_shape=jax.ShapeDtypeStruct((m_rows, ncols), out_dtype),
        compiler_params=_cparams(("parallel", "arbitrary"), 40),
        name=name,
    )(x, w, b)


def _proj_nobias(x, w, layer, out_dtype, tm, tn, name):
    m_rows, k = x.shape
    ncols = w.shape[-1]
    return pl.pallas_call(
        _proj_nobias_kernel,
        grid=(m_rows // tm, ncols // tn),
        in_specs=[
            pl.BlockSpec((tm, k), lambda m, n: (m, 0)),
            pl.BlockSpec((None, k, tn), lambda m, n: (layer, 0, n)),
        ],
        out_specs=pl.BlockSpec((tm, tn), lambda m, n: (m, n)),
        out_shape=jax.ShapeDtypeStruct((m_rows, ncols), out_dtype),
        compiler_params=_cparams(("parallel", "arbitrary"), 40),
        name=name,
    )(x, w)


def _glu_proj(x, w, b, layer, col0, ncols, tm, tn, name):
    m_rows, k = x.shape
    off_v = col0 // tn
    off_g = (col0 + ncols) // tn
    return pl.pallas_call(
        _glu_kernel,
        grid=(m_rows // tm, ncols // tn),
        in_specs=[
            pl.BlockSpec((tm, k), lambda m, n: (m, 0)),
            pl.BlockSpec((None, k, tn), lambda m, n: (layer, 0, off_v + n)),
            pl.BlockSpec((None, k, tn), lambda m, n: (layer, 0, off_g + n)),
            pl.BlockSpec((None, 1, tn), lambda m, n: (layer, 0, off_v + n)),
            pl.BlockSpec((None, 1, tn), lambda m, n: (layer, 0, off_g + n)),
        ],
        out_specs=pl.BlockSpec((tm, tn), lambda m, n: (m, n)),
        out_shape=jax.ShapeDtypeStruct((m_rows, ncols), F32),
        compiler_params=_cparams(("parallel", "arbitrary"), 48),
        name=name,
    )(x, w, w, b, b)


def _mix_prompt_kernel(u_ref, v_ref, hb_ref, hbh_ref, c_ref, ch_ref,
                       alg_ref, alb_ref, ws_ref, bsb_ref, wgrp_ref, bsc_ref,
                       wdw_ref, bdw_ref, clg_ref, clb_ref,
                       pa_ref, pb_ref, pc_ref,
                       hb_ext, c_ext, pool_scr, conv_scr, *, tile):
    t = pl.program_id(1)
    n_chunks = tile // CHUNK
    d_mix = v_ref.shape[-1]
    head_dim = d_mix // A_HEADS
    group_dim = d_mix // len(POOL_WINDOWS)

    vb = _layer_norm(v_ref[...], alg_ref[...], alb_ref[...]).astype(BF16)
    row = lax.broadcasted_iota(jnp.int32, (CHUNK, CHUNK), 0)
    col = lax.broadcasted_iota(jnp.int32, (CHUNK, CHUNK), 1)
    for h in range(A_HEADS):
        hs = slice(h * head_dim, (h + 1) * head_dim)
        wm = jnp.where(row >= col, ws_ref[h], 0.0).astype(BF16)
        rhs = jnp.concatenate(
            [vb[ci * CHUNK:(ci + 1) * CHUNK, hs] for ci in range(n_chunks)], axis=1)
        s = jnp.dot(wm, rhs, preferred_element_type=F32)
        bias = bsb_ref[h]
        for ci in range(n_chunks):
            rs = slice(ci * CHUNK, (ci + 1) * CHUNK)
            s_c = s[:, ci * head_dim:(ci + 1) * head_dim] + bias
            pa_ref[rs, hs] = (u_ref[rs, hs] * s_c).astype(BF16)

    first = t == 0
    hb_ext[0:POOL_HALO, :] = jnp.where(first, 0.0, hbh_ref[...])
    hb_ext[POOL_HALO:POOL_HALO + tile, :] = hb_ref[...]

    def pool_body(i, carry):
        r0 = pl.multiple_of(i * ROW_BLOCK, ROW_BLOCK)
        pos = t * tile + r0 + lax.broadcasted_iota(jnp.int32, (ROW_BLOCK, group_dim), 0)
        for g, w in enumerate(POOL_WINDOWS):
            gs = slice(g * group_dim, (g + 1) * group_dim)
            x = hb_ext[pl.ds(r0, POOL_HALO + ROW_BLOCK), gs]
            tok = x[POOL_HALO:POOL_HALO + ROW_BLOCK]
            win = tok
            for k in range(1, w):
                win = win + x[POOL_HALO - k:POOL_HALO - k + ROW_BLOCK]
            cnt = jnp.minimum(pos + 1, w).astype(F32)
            pool_scr[pl.ds(r0, ROW_BLOCK), gs] = win / cnt - tok
        return carry

    lax.fori_loop(0, tile // ROW_BLOCK, pool_body, 0)
    for g in range(len(POOL_WINDOWS)):
        gs = slice(g * group_dim, (g + 1) * group_dim)
        mixed = _bdot(pool_scr[:, gs].astype(BF16), wgrp_ref[g])
        pb_ref[:, gs] = (mixed * bsc_ref[:, gs]).astype(BF16)

    c_ext[0:CONV_HALO, :] = jnp.where(first, 0.0, ch_ref[...])
    c_ext[CONV_HALO:CONV_HALO + tile, :] = c_ref[...]
    lead = CONV_HALO - CONV_BUF

    def conv_body(i, carry):
        r0 = pl.multiple_of(i * ROW_BLOCK, ROW_BLOCK)
        for lt in range(d_mix // LANES):
            ls = slice(lt * LANES, (lt + 1) * LANES)
            x = c_ext[pl.ds(r0, CONV_HALO + ROW_BLOCK), ls]
            acc = jnp.broadcast_to(bdw_ref[:, ls], (ROW_BLOCK, LANES))
            for r in range(SUBLANES):
                taps = [k for k in range(CONV_WIDTH) if (lead + k) % SUBLANES == r]
                if not taps:
                    continue
                q_max = (lead + taps[-1]) // SUBLANES
                xr = x[r:r + SUBLANES * q_max + ROW_BLOCK]
                for k in taps:
                    q = (lead + k) // SUBLANES
                    acc = acc + wdw_ref[k:k + 1, ls] * xr[SUBLANES * q:SUBLANES * q + ROW_BLOCK]
            conv_scr[pl.ds(r0, ROW_BLOCK), ls] = acc
        return carry

    lax.fori_loop(0, tile // ROW_BLOCK, conv_body, 0)
    y = _layer_norm(conv_scr[...], clg_ref[...], clb_ref[...])
    pc_ref[...] = (y * _sigmoid(y)).astype(BF16)


def _mix_prompt(ga, hb, c, lw, layer, batch, seq, tile):
    d_mix = hb.shape[-1]
    n_t = seq // tile
    vec = lambda: pl.BlockSpec((None, 1, d_mix), lambda b, t: (layer, 0, 0))
    tile_spec = lambda cb: pl.BlockSpec((tile, d_mix), lambda b, t: (b * n_t + t, cb))

    def halo_spec(rows):
        per = tile // rows
        return pl.BlockSpec((rows, d_mix),
                            lambda b, t: (jnp.maximum((b * n_t + t) * per - 1, 0), 0))

    out = jax.ShapeDtypeStruct((batch * seq + lw["m_sample"], d_mix), BF16)
    return pl.pallas_call(
        functools.partial(_mix_prompt_kernel, tile=tile),
        grid=(batch, n_t),
        in_specs=[
            tile_spec(0), tile_spec(1),
            tile_spec(0), halo_spec(POOL_HALO),
            tile_spec(0), halo_spec(CONV_HALO),
            vec(), vec(),
            pl.BlockSpec((None, A_HEADS, CHUNK, CHUNK), lambda b, t: (layer, 0, 0, 0)),
            pl.BlockSpec((None, A_HEADS, CHUNK, CHUNK), lambda b, t: (layer, 0, 0, 0)),
            pl.BlockSpec((None,) + lw["b_w_group"].shape[1:], lambda b, t: (layer, 0, 0, 0)),
            vec(),
            pl.BlockSpec((None, CONV_WIDTH, d_mix), lambda b, t: (layer, 0, 0)),
            vec(), vec(), vec(),
        ],
        out_specs=[tile_spec(0), tile_spec(0), tile_spec(0)],
        out_shape=[out, out, out],
        scratch_shapes=[
            pltpu.VMEM((POOL_HALO + tile, d_mix), F32),
            pltpu.VMEM((CONV_HALO + tile, d_mix), F32),
            pltpu.VMEM((tile, d_mix), F32),
            pltpu.VMEM((tile, d_mix), F32),
        ],
        compiler_params=_cparams(("parallel", "arbitrary"), 48),
        name="mix_prompt",
    )(ga, ga, hb, hb, c, c,
      lw["a_ln_g"], lw["a_ln_b"], lw["a_ws"], lw["a_bs_b"], lw["b_w_group"], lw["b_scale"],
      lw["c_w_dw"], lw["c_b_dw"], lw["c_ln_g"], lw["c_ln_b"])


def _mix_sample_kernel(pa_in, pb_in, pc_in,
                       u_ref, v_ref, hb_ref, c_ref, sp_ref, sc_ref,
                       alg_ref, alb_ref, ws0_ref, bs0_ref, wgrp_ref, bsc_ref,
                       wdw_ref, bdw_ref, clg_ref, clb_ref,
                       pa_ref, pb_ref, pc_ref, vout_ref):
    del pa_in, pb_in, pc_in
    d_mix = v_ref.shape[-1]
    group_dim = d_mix // len(POOL_WINDOWS)

    v = _layer_norm(v_ref[...], alg_ref[...], alb_ref[...])
    vout_ref[...] = v
    s = v.astype(BF16).astype(F32) * ws0_ref[...].astype(BF16).astype(F32) + bs0_ref[...]
    pa_ref[...] = (u_ref[...] * s).astype(BF16)

    hb = hb_ref[...]
    sp = sp_ref[...]
    krow = lax.broadcasted_iota(jnp.int32, (POOL_BUF, group_dim), 0)
    for g, w in enumerate(POOL_WINDOWS):
        gs = slice(g * group_dim, (g + 1) * group_dim)
        keep = (krow >= POOL_BUF - (w - 1)).astype(F32)
        tok = hb[:, gs]
        win = tok + jnp.sum(sp[:, :, gs] * keep[None], axis=1)
        pooled = win / float(w) - tok
        mixed = _bdot(pooled.astype(BF16), wgrp_ref[g])
        pb_ref[:, gs] = (mixed * bsc_ref[:, gs]).astype(BF16)

    c_new = c_ref[...]
    conv = (jnp.sum(sc_ref[...] * wdw_ref[0:CONV_BUF, :][None], axis=1)
            + c_new * wdw_ref[CONV_BUF:CONV_WIDTH, :] + bdw_ref[...])
    y = _layer_norm(conv, clg_ref[...], clb_ref[...])
    pc_ref[...] = (y * _sigmoid(y)).astype(BF16)


def _mix_sample(pa, pb, pc, ga, hb, c, state_pool, state_conv, lw, layer, m_prompt, rows):
    d_mix = hb.shape[-1]
    m_sample = state_pool.shape[1]
    base = m_prompt // rows
    vec = lambda: pl.BlockSpec((None, 1, d_mix), lambda i: (layer, 0, 0))
    row_spec = lambda cb: pl.BlockSpec((rows, d_mix), lambda i: (base + i, cb))
    big = jax.ShapeDtypeStruct(pa.shape, BF16)
    any_spec = pl.BlockSpec(memory_space=pl.ANY)
    return pl.pallas_call(
        _mix_sample_kernel,
        grid=(m_sample // rows,),
        in_specs=[
            any_spec, any_spec, any_spec,
            row_spec(0), row_spec(1), row_spec(0), row_spec(0),
            pl.BlockSpec((None, rows, POOL_BUF, d_mix), lambda i: (layer, i, 0, 0)),
            pl.BlockSpec((None, rows, CONV_BUF, d_mix), lambda i: (layer, i, 0, 0)),
            vec(), vec(), vec(), vec(),
            pl.BlockSpec((None,) + lw["b_w_group"].shape[1:], lambda i: (layer, 0, 0, 0)),
            vec(),
            pl.BlockSpec((None, CONV_WIDTH, d_mix), lambda i: (layer, 0, 0)),
            vec(), vec(), vec(),
        ],
        out_specs=[row_spec(0), row_spec(0), row_spec(0),
                   pl.BlockSpec((rows, d_mix), lambda i: (i, 0))],
        out_shape=[big, big, big, jax.ShapeDtypeStruct((m_sample, d_mix), F32)],
        input_output_aliases={0: 0, 1: 1, 2: 2},
        compiler_params=_cparams(("parallel",), 48),
        name="mix_sample",
    )(pa, pb, pc, ga, ga, hb, c, state_pool, state_conv,
      lw["a_ln_g"], lw["a_ln_b"], lw["a_ws0"], lw["a_bs0"], lw["b_w_group"], lw["b_scale"],
      lw["c_w_dw"], lw["c_b_dw"], lw["c_ln_g"], lw["c_ln_b"])


def _merge_kernel(pa_ref, pb_ref, pc_ref, wa_ref, wb_ref, wc_ref,
                  g0_ref, g1_ref, g2_ref, o_ref):
    merged = (g0_ref[...] * _bdot(pa_ref[...], wa_ref[...])
              + g1_ref[...] * _bdot(pb_ref[...], wb_ref[...])
              + g2_ref[...] * _bdot(pc_ref[...], wc_ref[...]))
    o_ref[...] = merged.astype(o_ref.dtype)


def _merge(pa, pb, pc, gates, w_a, w_b, w_c, layer, tm, tn):
    m_rows, d_mix = pa.shape
    d_model = w_a.shape[-1]
    nb = d_model // tn
    act = lambda: pl.BlockSpec((tm, d_mix), lambda m, n: (m, 0))
    wsp = lambda: pl.BlockSpec((None, d_mix, tn), lambda m, n: (layer, 0, n))
    gsp = lambda j: pl.BlockSpec((tm, tn), lambda m, n: (m, j * nb + n))
    return pl.pallas_call(
        _merge_kernel,
        grid=(m_rows // tm, nb),
        in_specs=[act(), act(), act(), wsp(), wsp(), wsp(), gsp(0), gsp(1), gsp(2)],
        out_specs=pl.BlockSpec((tm, tn), lambda m, n: (m, n)),
        out_shape=jax.ShapeDtypeStruct((m_rows, d_model), BF16),
        compiler_params=_cparams(("parallel", "arbitrary"), 48),
        name="merge",
    )(pa, pb, pc, w_a, w_b, w_c, gates, gates, gates)


def _ln_res_kernel(x_ref, r_ref, g_ref, b_ref, of_ref, ob_ref, *, alpha):
    y = _layer_norm(alpha * x_ref[...] + r_ref[...], g_ref[...], b_ref[...])
    of_ref[...] = y
    ob_ref[...] = y.astype(BF16)


def _ln_res(x, r, g, b, layer, alpha, tm):
    m_rows, d = x.shape
    row = lambda: pl.BlockSpec((tm, d), lambda m: (m, 0))
    vec = lambda: pl.BlockSpec((None, 1, d), lambda m: (layer, 0, 0))
    return pl.pallas_call(
        functools.partial(_ln_res_kernel, alpha=alpha),
        grid=(m_rows // tm,),
        in_specs=[row(), row(), vec(), vec()],
        out_specs=[row(), row()],
        out_shape=[jax.ShapeDtypeStruct((m_rows, d), F32),
                   jax.ShapeDtypeStruct((m_rows, d), BF16)],
        compiler_params=_cparams(("parallel",), 40),
        name="ln_res",
    )(x, r, g, b)


def _ffn_kernel(x_ref, wg_ref, wu_ref, wd_ref, o_ref):
    f = pl.program_id(1)
    x = x_ref[...]
    gate = _bdot(x, wg_ref[...])
    up = _bdot(x, wu_ref[...])
    hid = (gate * _sigmoid(gate) * up).astype(BF16)
    part = _bdot(hid, wd_ref[...])

    @pl.when(f == 0)
    def _():
        o_ref[...] = part

    @pl.when(f > 0)
    def _():
        o_ref[...] += part


def _ffn(x, w_up, w_down, layer, tm, tf):
    m_rows, d = x.shape
    d_ff = w_down.shape[1]
    nf = d_ff // tf
    return pl.pallas_call(
        _ffn_kernel,
        grid=(m_rows // tm, nf),
        in_specs=[
            pl.BlockSpec((tm, d), lambda m, f: (m, 0)),
            pl.BlockSpec((None, d, tf), lambda m, f: (layer, 0, f)),
            pl.BlockSpec((None, d, tf), lambda m, f: (layer, 0, nf + f)),
            pl.BlockSpec((None, tf, d), lambda m, f: (layer, f, 0)),
        ],
        out_specs=pl.BlockSpec((tm, d), lambda m, f: (m, 0)),
        out_shape=jax.ShapeDtypeStruct((m_rows, d), F32),
        compiler_params=_cparams(("parallel", "arbitrary"), 58),
        name="ffn",
    )(x, w_up, w_up, w_down)


def kernel(x_prompt, x_sample, state_pool, state_conv, w_in, b_in, a_ln_g, a_ln_b, a_ws, a_bs, w_a_out, b_w_group, b_scale, w_b_out, c_w_dw, c_b_dw, c_ln_g, c_ln_b, w_c_out, w_out, ln1_g, ln1_b, w_ffn_up, w_ffn_down, ln2_g, ln2_b):
    batch, seq, d_model = x_prompt.shape
    m_sample = x_sample.shape[0] * x_sample.shape[1]
    depth = w_in.shape[0]
    d_mix = a_ln_g.shape[-1]
    head_dim = d_mix // A_HEADS
    m_prompt = batch * seq
    m_rows = m_prompt + m_sample
    alpha = (2.0 * depth) ** 0.25

    tm = m_rows // 8
    tm_ln = m_rows // 20
    assert m_rows % 8 == 0 and tm % 16 == 0 and tm_ln % 16 == 0
    assert x_sample.shape[1] == 1 and seq % 512 == 0

    vec3 = lambda a: a.reshape(depth, 1, a.shape[-1])
    lw = {
        "m_sample": m_sample,
        "a_ln_g": vec3(a_ln_g), "a_ln_b": vec3(a_ln_b),
        "a_ws": a_ws,
        "a_bs_b": jnp.broadcast_to(a_bs[..., None], a_bs.shape + (head_dim,)),
        "a_ws0": jnp.repeat(a_ws[:, :, 0, 0], head_dim, axis=-1).reshape(depth, 1, d_mix),
        "a_bs0": jnp.repeat(a_bs[:, :, 0], head_dim, axis=-1).reshape(depth, 1, d_mix),
        "b_w_group": b_w_group, "b_scale": vec3(b_scale),
        "c_w_dw": c_w_dw, "c_b_dw": vec3(c_b_dw),
        "c_ln_g": vec3(c_ln_g), "c_ln_b": vec3(c_ln_b),
    }
    b_in3 = vec3(b_in)
    ln1_g3, ln1_b3, ln2_g3, ln2_b3 = vec3(ln1_g), vec3(ln1_b), vec3(ln2_g), vec3(ln2_b)

    s1 = 2 * d_mix
    s2 = s1 + d_mix
    s3 = s2 + 2 * d_mix

    x = jnp.concatenate([x_prompt.reshape(m_prompt, d_model),
                         x_sample.reshape(m_sample, d_model)], axis=0)
    xb = x.astype(BF16)

    pool_p, conv_p, pool_s, conv_s, v_s = [], [], [], [], []
    for l in range(depth):
        ga = _proj(xb, w_in, b_in3, l, 0, s1, "gelu", F32, tm, 512, "proj_a")
        hb = _proj(xb, w_in, b_in3, l, s1, d_mix, "none", F32, tm, 512, "proj_b")
        c = _glu_proj(xb, w_in, b_in3, l, s2, d_mix, tm, 512, "proj_c")
        gates = _proj(xb, w_in, b_in3, l, s3, N_BRANCH * d_model, "sigmoid", F32, tm, 512,
                      "proj_gates")

        pa, pb, pc = _mix_prompt(ga, hb, c, lw, l, batch, seq, 512)
        pa, pb, pc, v_new = _mix_sample(pa, pb, pc, ga, hb, c, state_pool, state_conv,
                                        lw, l, m_prompt, 32)

        merged = _merge(pa, pb, pc, gates, w_a_out, w_b_out, w_c_out, l, tm, 256)
        mix = _proj_nobias(merged, w_out, l, F32, tm, 512, "proj_out")
        x, xb = _ln_res(x, mix, ln1_g3, ln1_b3, l, alpha, tm_ln)
        ffn = _ffn(xb, w_ffn_up, w_ffn_down, l, tm, 256)
        x, xb = _ln_res(x, ffn, ln2_g3, ln2_b3, l, alpha, tm_ln)

        hb_p = hb[:m_prompt].reshape(batch, seq, d_mix)
        c_p = c[:m_prompt].reshape(batch, seq, d_mix)
        pool_p.append(hb_p[:, seq - POOL_BUF:])
        conv_p.append(c_p[:, seq - CONV_BUF:])
        pool_s.append(jnp.concatenate([state_pool[l][:, 1:], hb[m_prompt:, None]], axis=1))
        conv_s.append(jnp.concatenate([state_conv[l][:, 1:], c[m_prompt:, None]], axis=1))
        v_s.append(v_new[:, None])

    y_prompt = x[:m_prompt].reshape(batch, seq, d_model)
    y_sample = x[m_prompt:].reshape(m_sample, 1, d_model)
    return (y_prompt, y_sample, jnp.stack(pool_p), jnp.stack(conv_p),
            jnp.stack(pool_s), jnp.stack(conv_s), jnp.stack(v_s))
```

```python
import functools

import jax
import jax.numpy as jnp
from jax import lax
from jax.experimental import pallas as pl
from jax.experimental.pallas import tpu as pltpu

F32 = jnp.float32
BF16 = jnp.bfloat16

LN_EPS = 1e-5
A_HEADS = 8
CHUNK = 128
POOL_WINDOWS = (2, 4, 8, 16)
POOL_BUF = 15
CONV_WIDTH = 31
CONV_BUF = CONV_WIDTH - 1
N_BRANCH = 3

POOL_HALO = 16
CONV_HALO = 32
ROW_BLOCK = 32
SUBLANES = 8
LANES = 128
V7X_VMEM_BYTES = 64 * 1024 * 1024


def _cparams(semantics, vmem_mb):
    assert vmem_mb * 1024 * 1024 < V7X_VMEM_BYTES
    return pltpu.CompilerParams(dimension_semantics=semantics,
                                vmem_limit_bytes=vmem_mb * 1024 * 1024)


def _layer_norm(x, g, b):
    mu = jnp.mean(x, axis=-1, keepdims=True)
    xc = x - mu
    var = jnp.mean(xc * xc, axis=-1, keepdims=True)
    return xc * lax.rsqrt(var + LN_EPS) * g + b


def _gelu_exact(x):
    return 0.5 * x * (1.0 + lax.erf(x * (0.5 ** 0.5)))


def _sigmoid(x):
    return 1.0 / (1.0 + jnp.exp(-x))


def _bdot(a, w):
    return jnp.dot(a, w.astype(BF16), preferred_element_type=F32)


def _proj_kernel(x_ref, w_ref, b_ref, o_ref, *, act):
    h = _bdot(x_ref[...], w_ref[...]) + b_ref[...]
    if act == "gelu":
        h = _gelu_exact(h)
    elif act == "sigmoid":
        h = _sigmoid(h)
    o_ref[...] = h.astype(o_ref.dtype)


def _proj_nobias_kernel(x_ref, w_ref, o_ref):
    o_ref[...] = _bdot(x_ref[...], w_ref[...]).astype(o_ref.dtype)


def _glu_kernel(x_ref, wv_ref, wg_ref, bv_ref, bg_ref, o_ref):
    x = x_ref[...]
    val = _bdot(x, wv_ref[...]) + bv_ref[...]
    gate = _bdot(x, wg_ref[...]) + bg_ref[...]
    o_ref[...] = val * _sigmoid(gate)


def _proj(x, w, b, layer, col0, ncols, act, out_dtype, tm, tn, name):
    m_rows, k = x.shape
    off = col0 // tn
    return pl.pallas_call(
        functools.partial(_proj_kernel, act=act),
        grid=(m_rows // tm, ncols // tn),
        in_specs=[
            pl.BlockSpec((tm, k), lambda m, n: (m, 0)),
            pl.BlockSpec((None, k, tn), lambda m, n: (layer, 0, off + n)),
            pl.BlockSpec((None, 1, tn), lambda m, n: (layer, 0, off + n)),
        ],
        out_specs=pl.BlockSpec((tm, tn), lambda m, n: (m, n)),
        out_shape=jax.ShapeDtypeStruct((m_rows, ncols), out_dtype),
        compiler_params=_cparams(("parallel", "arbitrary"), 40),
        name=name,
    )(x, w, b)


def _proj_nobias(x, w, layer, out_dtype, tm, tn, name):
    m_rows, k = x.shape
    ncols = w.shape[-1]
    return pl.pallas_call(
        _proj_nobias_kernel,
        grid=(m_rows // tm, ncols // tn),
        in_specs=[
            pl.BlockSpec((tm, k), lambda m, n: (m, 0)),
            pl.BlockSpec((None, k, tn), lambda m, n: (layer, 0, n)),
        ],
        out_specs=pl.BlockSpec((tm, tn), lambda m, n: (m, n)),
        out_shape=jax.ShapeDtypeStruct((m_rows, ncols), out_dtype),
        compiler_params=_cparams(("parallel", "arbitrary"), 40),
        name=name,
    )(x, w)


def _glu_proj(x, w, b, layer, col0, ncols, tm, tn, name):
    m_rows, k = x.shape
    off_v = col0 // tn
    off_g = (col0 + ncols) // tn
    return pl.pallas_call(
        _glu_kernel,
        grid=(m_rows // tm, ncols // tn),
        in_specs=[
            pl.BlockSpec((tm, k), lambda m, n: (m, 0)),
            pl.BlockSpec((None, k, tn), lambda m, n: (layer, 0, off_v + n)),
            pl.BlockSpec((None, k, tn), lambda m, n: (layer, 0, off_g + n)),
            pl.BlockSpec((None, 1, tn), lambda m, n: (layer, 0, off_v + n)),
            pl.BlockSpec((None, 1, tn), lambda m, n: (layer, 0, off_g + n)),
        ],
        out_specs=pl.BlockSpec((tm, tn), lambda m, n: (m, n)),
        out_shape=jax.ShapeDtypeStruct((m_rows, ncols), F32),
        compiler_params=_cparams(("parallel", "arbitrary"), 48),
        name=name,
    )(x, w, w, b, b)


def _mix_prompt_kernel(u_ref, v_ref, hb_ref, hbh_ref, c_ref, ch_ref,
                       alg_ref, alb_ref, ws_ref, bsb_ref, wgrp_ref, bsc_ref,
                       wdw_ref, bdw_ref, clg_ref, clb_ref,
                       pa_ref, pb_ref, pc_ref,
                       hb_ext, c_ext, pool_scr, conv_scr, *, tile):
    t = pl.program_id(1)
    n_chunks = tile // CHUNK
    d_mix = v_ref.shape[-1]
    head_dim = d_mix // A_HEADS
    group_dim = d_mix // len(POOL_WINDOWS)

    vb = _layer_norm(v_ref[...], alg_ref[...], alb_ref[...]).astype(BF16)
    row = lax.broadcasted_iota(jnp.int32, (CHUNK, CHUNK), 0)
    col = lax.broadcasted_iota(jnp.int32, (CHUNK, CHUNK), 1)
    for h in range(A_HEADS):
        hs = slice(h * head_dim, (h + 1) * head_dim)
        wm = jnp.where(row >= col, ws_ref[h], 0.0).astype(BF16)
        rhs = jnp.concatenate(
            [vb[ci * CHUNK:(ci + 1) * CHUNK, hs] for ci in range(n_chunks)], axis=1)
        s = jnp.dot(wm, rhs, preferred_element_type=F32)
        bias = bsb_ref[h]
        for ci in range(n_chunks):
            rs = slice(ci * CHUNK, (ci + 1) * CHUNK)
            s_c = s[:, ci * head_dim:(ci + 1) * head_dim] + bias
            pa_ref[rs, hs] = (u_ref[rs, hs] * s_c).astype(BF16)

    first = t == 0
    hb_ext[0:POOL_HALO, :] = jnp.where(first, 0.0, hbh_ref[...])
    hb_ext[POOL_HALO:POOL_HALO + tile, :] = hb_ref[...]

    def pool_body(i, carry):
        r0 = pl.multiple_of(i * ROW_BLOCK, ROW_BLOCK)
        pos = t * tile + r0 + lax.broadcasted_iota(jnp.int32, (ROW_BLOCK, group_dim), 0)
        for g, w in enumerate(POOL_WINDOWS):
            gs = slice(g * group_dim, (g + 1) * group_dim)
            x = hb_ext[pl.ds(r0, POOL_HALO + ROW_BLOCK), gs]
            tok = x[POOL_HALO:POOL_HALO + ROW_BLOCK]
            win = tok
            for k in range(1, w):
                win = win + x[POOL_HALO - k:POOL_HALO - k + ROW_BLOCK]
            cnt = jnp.minimum(pos + 1, w).astype(F32)
            pool_scr[pl.ds(r0, ROW_BLOCK), gs] = win / cnt - tok
        return carry

    lax.fori_loop(0, tile // ROW_BLOCK, pool_body, 0)
    for g in range(len(POOL_WINDOWS)):
        gs = slice(g * group_dim, (g + 1) * group_dim)
        mixed = _bdot(pool_scr[:, gs].astype(BF16), wgrp_ref[g])
        pb_ref[:, gs] = (mixed * bsc_ref[:, gs]).astype(BF16)

    c_ext[0:CONV_HALO, :] = jnp.where(first, 0.0, ch_ref[...])
    c_ext[CONV_HALO:CONV_HALO + tile, :] = c_ref[...]
    lead = CONV_HALO - CONV_BUF

    def conv_body(i, carry):
        r0 = pl.multiple_of(i * ROW_BLOCK, ROW_BLOCK)
        for lt in range(d_mix // LANES):
            ls = slice(lt * LANES, (lt + 1) * LANES)
            x = c_ext[pl.ds(r0, CONV_HALO + ROW_BLOCK), ls]
            acc = jnp.broadcast_to(bdw_ref[:, ls], (ROW_BLOCK, LANES))
            for r in range(SUBLANES):
                taps = [k for k in range(CONV_WIDTH) if (lead + k) % SUBLANES == r]
                if not taps:
                    continue
                q_max = (lead + taps[-1]) // SUBLANES
                xr = x[r:r + SUBLANES * q_max + ROW_BLOCK]
                for k in taps:
                    q = (lead + k) // SUBLANES
                    acc = acc + wdw_ref[k:k + 1, ls] * xr[SUBLANES * q:SUBLANES * q + ROW_BLOCK]
            conv_scr[pl.ds(r0, ROW_BLOCK), ls] = acc
        return carry

    lax.fori_loop(0, tile // ROW_BLOCK, conv_body, 0)
    y = _layer_norm(conv_scr[...], clg_ref[...], clb_ref[...])
    pc_ref[...] = (y * _sigmoid(y)).astype(BF16)


def _mix_prompt(ga, hb, c, lw, layer, batch, seq, tile):
    d_mix = hb.shape[-1]
    n_t = seq // tile
    vec = lambda: pl.BlockSpec((None, 1, d_mix), lambda b, t: (layer, 0, 0))
    tile_spec = lambda cb: pl.BlockSpec((tile, d_mix), lambda b, t: (b * n_t + t, cb))

    def halo_spec(rows):
        per = tile // rows
        return pl.BlockSpec((rows, d_mix),
                            lambda b, t: (jnp.maximum((b * n_t + t) * per - 1, 0), 0))

    out = jax.ShapeDtypeStruct((batch * seq + lw["m_sample"], d_mix), BF16)
    return pl.pallas_call(
        functools.partial(_mix_prompt_kernel, tile=tile),
        grid=(batch, n_t),
        in_specs=[
            tile_spec(0), tile_spec(1),
            tile_spec(0), halo_spec(POOL_HALO),
            tile_spec(0), halo_spec(CONV_HALO),
            vec(), vec(),
            pl.BlockSpec((None, A_HEADS, CHUNK, CHUNK), lambda b, t: (layer, 0, 0, 0)),
            pl.BlockSpec((None, A_HEADS, CHUNK, CHUNK), lambda b, t: (layer, 0, 0, 0)),
            pl.BlockSpec((None,) + lw["b_w_group"].shape[1:], lambda b, t: (layer, 0, 0, 0)),
            vec(),
            pl.BlockSpec((None, CONV_WIDTH, d_mix), lambda b, t: (layer, 0, 0)),
            vec(), vec(), vec(),
        ],
        out_specs=[tile_spec(0), tile_spec(0), tile_spec(0)],
        out_shape=[out, out, out],
        scratch_shapes=[
            pltpu.VMEM((POOL_HALO + tile, d_mix), F32),
            pltpu.VMEM((CONV_HALO + tile, d_mix), F32),
            pltpu.VMEM((tile, d_mix), F32),
            pltpu.VMEM((tile, d_mix), F32),
        ],
        compiler_params=_cparams(("parallel", "arbitrary"), 48),
        name="mix_prompt",
    )(ga, ga, hb, hb, c, c,
      lw["a_ln_g"], lw["a_ln_b"], lw["a_ws"], lw["a_bs_b"], lw["b_w_group"], lw["b_scale"],
      lw["c_w_dw"], lw["c_b_dw"], lw["c_ln_g"], lw["c_ln_b"])


def _mix_sample_kernel(pa_in, pb_in, pc_in,
                       u_ref, v_ref, hb_ref, c_ref, sp_ref, sc_ref,
                       alg_ref, alb_ref, ws0_ref, bs0_ref, wgrp_ref, bsc_ref,
                       wdw_ref, bdw_ref, clg_ref, clb_ref,
                       pa_ref, pb_ref, pc_ref, vout_ref):
    del pa_in, pb_in, pc_in
    d_mix = v_ref.shape[-1]
    group_dim = d_mix // len(POOL_WINDOWS)

    v = _layer_norm(v_ref[...], alg_ref[...], alb_ref[...])
    vout_ref[...] = v
    s = v.astype(BF16).astype(F32) * ws0_ref[...].astype(BF16).astype(F32) + bs0_ref[...]
    pa_ref[...] = (u_ref[...] * s).astype(BF16)

    hb = hb_ref[...]
    sp = sp_ref[...]
    krow = lax.broadcasted_iota(jnp.int32, (POOL_BUF, group_dim), 0)
    for g, w in enumerate(POOL_WINDOWS):
        gs = slice(g * group_dim, (g + 1) * group_dim)
        keep = (krow >= POOL_BUF - (w - 1)).astype(F32)
        tok = hb[:, gs]
        win = tok + jnp.sum(sp[:, :, gs] * keep[None], axis=1)
        pooled = win / float(w) - tok
        mixed = _bdot(pooled.astype(BF16), wgrp_ref[g])
        pb_ref[:, gs] = (mixed * bsc_ref[:, gs]).astype(BF16)

    c_new = c_ref[...]
    conv = (jnp.sum(sc_ref[...] * wdw_ref[0:CONV_BUF, :][None], axis=1)
            + c_new * wdw_ref[CONV_BUF:CONV_WIDTH, :] + bdw_ref[...])
    y = _layer_norm(conv, clg_ref[...], clb_ref[...])
    pc_ref[...] = (y * _sigmoid(y)).astype(BF16)


def _mix_sample(pa, pb, pc, ga, hb, c, state_pool, state_conv, lw, layer, m_prompt, rows):
    d_mix = hb.shape[-1]
    m_sample = state_pool.shape[1]
    base = m_prompt // rows
    vec = lambda: pl.BlockSpec((None, 1, d_mix), lambda i: (layer, 0, 0))
    row_spec = lambda cb: pl.BlockSpec((rows, d_mix), lambda i: (base + i, cb))
    big = jax.ShapeDtypeStruct(pa.shape, BF16)
    any_spec = pl.BlockSpec(memory_space=pl.ANY)
    return pl.pallas_call(
        _mix_sample_kernel,
        grid=(m_sample // rows,),
        in_specs=[
            any_spec, any_spec, any_spec,
            row_spec(0), row_spec(1), row_spec(0), row_spec(0),
            pl.BlockSpec((None, rows, POOL_BUF, d_mix), lambda i: (layer, i, 0, 0)),
            pl.BlockSpec((None, rows, CONV_BUF, d_mix), lambda i: (layer, i, 0, 0)),
            vec(), vec(), vec(), vec(),
            pl.BlockSpec((None,) + lw["b_w_group"].shape[1:], lambda i: (layer, 0, 0, 0)),
            vec(),
            pl.BlockSpec((None, CONV_WIDTH, d_mix), lambda i: (layer, 0, 0)),
            vec(), vec(), vec(),
        ],
        out_specs=[row_spec(0), row_spec(0), row_spec(0),
                   pl.BlockSpec((rows, d_mix), lambda i: (i, 0))],
        out_shape=[big, big, big, jax.ShapeDtypeStruct((m_sample, d_mix), F32)],
        input_output_aliases={0: 0, 1: 1, 2: 2},
        compiler_params=_cparams(("parallel",), 48),
        name="mix_sample",
    )(pa, pb, pc, ga, ga, hb, c, state_pool, state_conv,
      lw["a_ln_g"], lw["a_ln_b"], lw["a_ws0"], lw["a_bs0"], lw["b_w_group"], lw["b_scale"],
      lw["c_w_dw"], lw["c_b_dw"], lw["c_ln_g"], lw["c_ln_b"])


def _merge_kernel(pa_ref, pb_ref, pc_ref, wa_ref, wb_ref, wc_ref,
                  g0_ref, g1_ref, g2_ref, o_ref):
    merged = (g0_ref[...] * _bdot(pa_ref[...], wa_ref[...])
              + g1_ref[...] * _bdot(pb_ref[...], wb_ref[...])
              + g2_ref[...] * _bdot(pc_ref[...], wc_ref[...]))
    o_ref[...] = merged.astype(o_ref.dtype)


def _merge(pa, pb, pc, gates, w_a, w_b, w_c, layer, tm, tn):
    m_rows, d_mix = pa.shape
    d_model = w_a.shape[-1]
    nb = d_model // tn
    act = lambda: pl.BlockSpec((tm, d_mix), lambda m, n: (m, 0))
    wsp = lambda: pl.BlockSpec((None, d_mix, tn), lambda m, n: (layer, 0, n))
    gsp = lambda j: pl.BlockSpec((tm, tn), lambda m, n: (m, j * nb + n))
    return pl.pallas_call(
        _merge_kernel,
        grid=(m_rows // tm, nb),
        in_specs=[act(), act(), act(), wsp(), wsp(), wsp(), gsp(0), gsp(1), gsp(2)],
        out_specs=pl.BlockSpec((tm, tn), lambda m, n: (m, n)),
        out_shape=jax.ShapeDtypeStruct((m_rows, d_model), BF16),
        compiler_params=_cparams(("parallel", "arbitrary"), 48),
        name="merge",
    )(pa, pb, pc, w_a, w_b, w_c, gates, gates, gates)


def _ln_res_kernel(x_ref, r_ref, g_ref, b_ref, of_ref, ob_ref, *, alpha):
    y = _layer_norm(alpha * x_ref[...] + r_ref[...], g_ref[...], b_ref[...])
    of_ref[...] = y
    ob_ref[...] = y.astype(BF16)


def _ln_res(x, r, g, b, layer, alpha, tm):
    m_rows, d = x.shape
    row = lambda: pl.BlockSpec((tm, d), lambda m: (m, 0))
    vec = lambda: pl.BlockSpec((None, 1, d), lambda m: (layer, 0, 0))
    return pl.pallas_call(
        functools.partial(_ln_res_kernel, alpha=alpha),
        grid=(m_rows // tm,),
        in_specs=[row(), row(), vec(), vec()],
        out_specs=[row(), row()],
        out_shape=[jax.ShapeDtypeStruct((m_rows, d), F32),
                   jax.ShapeDtypeStruct((m_rows, d), BF16)],
        compiler_params=_cparams(("parallel",), 40),
        name="ln_res",
    )(x, r, g, b)


def _ffn_kernel(x_ref, wg_ref, wu_ref, wd_ref, o_ref):
    @pl.when(pl.program_id(1) == 0)
    def _():
        o_ref[...] = jnp.zeros_like(o_ref)

    x = x_ref[...]
    gate = _bdot(x, wg_ref[...])
    up = _bdot(x, wu_ref[...])
    hid = (gate * _sigmoid(gate) * up).astype(BF16)
    o_ref[...] += _bdot(hid, wd_ref[...])


def _ffn(x, w_up, w_down, layer, tm, tf):
    m_rows, d = x.shape
    d_ff = w_down.shape[1]
    nf = d_ff // tf
    return pl.pallas_call(
        _ffn_kernel,
        grid=(m_rows // tm, nf),
        in_specs=[
            pl.BlockSpec((tm, d), lambda m, f: (m, 0)),
            pl.BlockSpec((None, d, tf), lambda m, f: (layer, 0, f)),
            pl.BlockSpec((None, d, tf), lambda m, f: (layer, 0, nf + f)),
            pl.BlockSpec((None, tf, d), lambda m, f: (layer, f, 0)),
        ],
        out_specs=pl.BlockSpec((tm, d), lambda m, f: (m, 0)),
        out_shape=jax.ShapeDtypeStruct((m_rows, d), F32),
        compiler_params=_cparams(("parallel", "arbitrary"), 58),
        name="ffn",
    )(x, w_up, w_up, w_down)


def kernel(x_prompt, x_sample, state_pool, state_conv, w_in, b_in, a_ln_g, a_ln_b, a_ws, a_bs, w_a_out, b_w_group, b_scale, w_b_out, c_w_dw, c_b_dw, c_ln_g, c_ln_b, w_c_out, w_out, ln1_g, ln1_b, w_ffn_up, w_ffn_down, ln2_g, ln2_b):
    batch, seq, d_model = x_prompt.shape
    m_sample = x_sample.shape[0] * x_sample.shape[1]
    depth = w_in.shape[0]
    d_mix = a_ln_g.shape[-1]
    head_dim = d_mix // A_HEADS
    m_prompt = batch * seq
    m_rows = m_prompt + m_sample
    alpha = (2.0 * depth) ** 0.25

    tm = m_rows // 8
    tm_ln = m_rows // 20
    assert m_rows % 8 == 0 and tm % 16 == 0 and tm_ln % 16 == 0
    assert x_sample.shape[1] == 1 and seq % 512 == 0

    vec3 = lambda a: a.reshape(depth, 1, a.shape[-1])
    lw = {
        "m_sample": m_sample,
        "a_ln_g": vec3(a_ln_g), "a_ln_b": vec3(a_ln_b),
        "a_ws": a_ws,
        "a_bs_b": jnp.broadcast_to(a_bs[..., None], a_bs.shape + (head_dim,)),
        "a_ws0": jnp.repeat(a_ws[:, :, 0, 0], head_dim, axis=-1).reshape(depth, 1, d_mix),
        "a_bs0": jnp.repeat(a_bs[:, :, 0], head_dim, axis=-1).reshape(depth, 1, d_mix),
        "b_w_group": b_w_group, "b_scale": vec3(b_scale),
        "c_w_dw": c_w_dw, "c_b_dw": vec3(c_b_dw),
        "c_ln_g": vec3(c_ln_g), "c_ln_b": vec3(c_ln_b),
    }
    b_in3 = vec3(b_in)
    ln1_g3, ln1_b3, ln2_g3, ln2_b3 = vec3(ln1_g), vec3(ln1_b), vec3(ln2_g), vec3(ln2_b)

    s1 = 2 * d_mix
    s2 = s1 + d_mix
    s3 = s2 + 2 * d_mix

    x = jnp.concatenate([x_prompt.reshape(m_prompt, d_model),
                         x_sample.reshape(m_sample, d_model)], axis=0)
    xb = x.astype(BF16)

    pool_p, conv_p, pool_s, conv_s, v_s = [], [], [], [], []
    for l in range(depth):
        ga = _proj(xb, w_in, b_in3, l, 0, s1, "gelu", F32, tm, 512, "proj_a")
        hb = _proj(xb, w_in, b_in3, l, s1, d_mix, "none", F32, tm, 512, "proj_b")
        c = _glu_proj(xb, w_in, b_in3, l, s2, d_mix, tm, 512, "proj_c")
        gates = _proj(xb, w_in, b_in3, l, s3, N_BRANCH * d_model, "sigmoid", BF16, tm, 512,
                      "proj_gates")

        pa, pb, pc = _mix_prompt(ga, hb, c, lw, l, batch, seq, 512)
        pa, pb, pc, v_new = _mix_sample(pa, pb, pc, ga, hb, c, state_pool, state_conv,
                                        lw, l, m_prompt, 32)

        merged = _merge(pa, pb, pc, gates, w_a_out, w_b_out, w_c_out, l, tm, 256)
        mix = _proj_nobias(merged, w_out, l, F32, tm, 512, "proj_out")
        x, xb = _ln_res(x, mix, ln1_g3, ln1_b3, l, alpha, tm_ln)
        ffn = _ffn(xb, w_ffn_up, w_ffn_down, l, tm, 256)
        x, xb = _ln_res(x, ffn, ln2_g3, ln2_b3, l, alpha, tm_ln)

        hb_p = hb[:m_prompt].reshape(batch, seq, d_mix)
        c_p = c[:m_prompt].reshape(batch, seq, d_mix)
        pool_p.append(hb_p[:, seq - POOL_BUF:])
        conv_p.append(c_p[:, seq - CONV_BUF:])
        pool_s.append(jnp.concatenate([state_pool[l][:, 1:], hb[m_prompt:, None]], axis=1))
        conv_s.append(jnp.concatenate([state_conv[l][:, 1:], c[m_prompt:, None]], axis=1))
        v_s.append(v_new[:, None])

    y_prompt = x[:m_prompt].reshape(batch, seq, d_model)
    y_sample = x[m_prompt:].reshape(m_sample, 1, d_model)
    return (y_prompt, y_sample, jnp.stack(pool_p), jnp.stack(conv_p),
            jnp.stack(pool_s), jnp.stack(conv_s), jnp.stack(v_s))
```

```python
import functools

import jax
import jax.numpy as jnp
from jax import lax
from jax.experimental import pallas as pl
from jax.experimental.pallas import tpu as pltpu

F32 = jnp.float32
BF16 = jnp.bfloat16

LN_EPS = 1e-5
A_HEADS = 8
CHUNK = 128
POOL_WINDOWS = (2, 4, 8, 16)
POOL_BUF = 15
CONV_WIDTH = 31
CONV_BUF = CONV_WIDTH - 1
N_BRANCH = 3

SUBLANES = 8
LANES = 128
BF16_ROWS = 16
POOL_HALO = 16
CONV_HALO = 32
ROW_BLOCK = 64
MIX_SUB = 256
LN_ROWS = 208
V7X_VMEM_BYTES = 64 * 1024 * 1024

_ONCE = pl.Buffered(1)


def _cparams(semantics, vmem_mb):
    assert vmem_mb * 1024 * 1024 < V7X_VMEM_BYTES
    return pltpu.CompilerParams(dimension_semantics=semantics,
                                vmem_limit_bytes=vmem_mb * 1024 * 1024)


def _layer_norm(x, g, b):
    mu = jnp.mean(x, axis=-1, keepdims=True)
    xc = x - mu
    var = jnp.mean(xc * xc, axis=-1, keepdims=True)
    return xc * lax.rsqrt(var + LN_EPS) * g + b


def _gelu_exact(x):
    return 0.5 * x * (1.0 + lax.erf(x * (0.5 ** 0.5)))


def _sigmoid(x):
    return 1.0 / (1.0 + jnp.exp(-x))


def _bdot(a, w):
    return jnp.dot(a, w.astype(BF16), preferred_element_type=F32)


def _residual_ln_rows(x_ref, acc_ref, g_ref, b_ref, bf_ref, alpha):
    def body(i, carry):
        r = pl.multiple_of(i * LN_ROWS, LN_ROWS)
        y = _layer_norm(alpha * x_ref[pl.ds(r, LN_ROWS), :] + acc_ref[pl.ds(r, LN_ROWS), :],
                        g_ref[...], b_ref[...])
        acc_ref[pl.ds(r, LN_ROWS), :] = y
        if bf_ref is not None:
            bf_ref[pl.ds(r, LN_ROWS), :] = y.astype(BF16)
        return carry

    lax.fori_loop(0, acc_ref.shape[0] // LN_ROWS, body, 0)


def _proj_kernel(x_ref, w_ref, b_ref, o_ref, *, act):
    h = _bdot(x_ref[...], w_ref[...]) + b_ref[...]
    if act == "gelu":
        h = _gelu_exact(h)
    elif act == "sigmoid":
        h = _sigmoid(h)
    o_ref[...] = h.astype(o_ref.dtype)


def _glu_kernel(x_ref, wv_ref, wg_ref, bv_ref, bg_ref, o_ref):
    x = x_ref[...]
    val = _bdot(x, wv_ref[...]) + bv_ref[...]
    gate = _bdot(x, wg_ref[...]) + bg_ref[...]
    o_ref[...] = val * _sigmoid(gate)


def _proj(x, w, b, layer, col0, ncols, act, out_dtype, tm, tn, name):
    m_rows, k = x.shape
    off = col0 // tn
    return pl.pallas_call(
        functools.partial(_proj_kernel, act=act),
        grid=(m_rows // tm, ncols // tn),
        in_specs=[
            pl.BlockSpec((tm, k), lambda m, n: (m, 0)),
            pl.BlockSpec((None, k, tn), lambda m, n: (layer, 0, off + n)),
            pl.BlockSpec((None, 1, tn), lambda m, n: (layer, 0, off + n)),
        ],
        out_specs=pl.BlockSpec((tm, tn), lambda m, n: (m, n)),
        out_shape=jax.ShapeDtypeStruct((m_rows, ncols), out_dtype),
        compiler_params=_cparams(("parallel", "arbitrary"), 48),
        name=name,
    )(x, w, b)


def _glu_proj(x, w, b, layer, col0, ncols, tm, tn, name):
    m_rows, k = x.shape
    off_v = col0 // tn
    off_g = (col0 + ncols) // tn
    return pl.pallas_call(
        _glu_kernel,
        grid=(m_rows // tm, ncols // tn),
        in_specs=[
            pl.BlockSpec((tm, k), lambda m, n: (m, 0)),
            pl.BlockSpec((None, k, tn), lambda m, n: (layer, 0, off_v + n)),
            pl.BlockSpec((None, k, tn), lambda m, n: (layer, 0, off_g + n)),
            pl.BlockSpec((None, 1, tn), lambda m, n: (layer, 0, off_v + n)),
            pl.BlockSpec((None, 1, tn), lambda m, n: (layer, 0, off_g + n)),
        ],
        out_specs=pl.BlockSpec((tm, tn), lambda m, n: (m, n)),
        out_shape=jax.ShapeDtypeStruct((m_rows, ncols), F32),
        compiler_params=_cparams(("parallel", "arbitrary"), 52),
        name=name,
    )(x, w, w, b, b)


def _mix_prompt_step(t, u_ref, v_ref, hb_ref, hbh_ref, c_ref, ch_ref,
                     alg_ref, alb_ref, ws_ref, bsb_ref, wgrp_ref, bsc_ref,
                     wdw_ref, bdw_ref, clg_ref, clb_ref,
                     pa_ref, pb_ref, pc_ref, hb_ext, c_ext, pool_scr, conv_scr):
    sub, d_mix = v_ref.shape
    n_chunks = sub // CHUNK
    head_dim = d_mix // A_HEADS
    group_dim = d_mix // len(POOL_WINDOWS)
    base = pl.multiple_of(t * sub, sub)

    vb = _layer_norm(v_ref[...], alg_ref[...], alb_ref[...]).astype(BF16)
    row = lax.broadcasted_iota(jnp.int32, (CHUNK, CHUNK), 0)
    col = lax.broadcasted_iota(jnp.int32, (CHUNK, CHUNK), 1)
    for h in range(A_HEADS):
        hs = slice(h * head_dim, (h + 1) * head_dim)
        wm = jnp.where(row >= col, ws_ref[h], 0.0).astype(BF16)
        rhs = jnp.concatenate(
            [vb[ci * CHUNK:(ci + 1) * CHUNK, hs] for ci in range(n_chunks)], axis=1)
        s = jnp.dot(wm, rhs, preferred_element_type=F32)
        bias = bsb_ref[h]
        for ci in range(n_chunks):
            rs = slice(ci * CHUNK, (ci + 1) * CHUNK)
            s_c = s[:, ci * head_dim:(ci + 1) * head_dim] + bias
            pa_ref[pl.ds(base + ci * CHUNK, CHUNK), hs] = (u_ref[rs, hs] * s_c).astype(BF16)

    first = t == 0
    hb_ext[0:POOL_HALO, :] = jnp.where(first, 0.0, hbh_ref[...])
    hb_ext[POOL_HALO:POOL_HALO + sub, :] = hb_ref[...]

    def pool_body(i, carry):
        r0 = pl.multiple_of(i * ROW_BLOCK, ROW_BLOCK)
        pos = base + r0 + lax.broadcasted_iota(jnp.int32, (ROW_BLOCK, group_dim), 0)
        for g, w in enumerate(POOL_WINDOWS):
            gs = slice(g * group_dim, (g + 1) * group_dim)
            x = hb_ext[pl.ds(r0, POOL_HALO + ROW_BLOCK), gs]
            tok = x[POOL_HALO:POOL_HALO + ROW_BLOCK]
            win = tok
            for k in range(1, w):
                win = win + x[POOL_HALO - k:POOL_HALO - k + ROW_BLOCK]
            cnt = jnp.minimum(pos + 1, w).astype(F32)
            pool_scr[pl.ds(r0, ROW_BLOCK), gs] = win / cnt - tok
        return carry

    lax.fori_loop(0, sub // ROW_BLOCK, pool_body, 0)
    for g in range(len(POOL_WINDOWS)):
        gs = slice(g * group_dim, (g + 1) * group_dim)
        mixed = _bdot(pool_scr[:, gs].astype(BF16), wgrp_ref[g])
        pb_ref[pl.ds(base, sub), gs] = (mixed * bsc_ref[:, gs]).astype(BF16)

    c_ext[0:CONV_HALO, :] = jnp.where(first, 0.0, ch_ref[...])
    c_ext[CONV_HALO:CONV_HALO + sub, :] = c_ref[...]
    lead = CONV_HALO - CONV_BUF

    def conv_body(i, carry):
        r0 = pl.multiple_of(i * ROW_BLOCK, ROW_BLOCK)
        for lt in range(d_mix // LANES):
            ls = slice(lt * LANES, (lt + 1) * LANES)
            x = c_ext[pl.ds(r0, CONV_HALO + ROW_BLOCK), ls]
            acc = jnp.broadcast_to(bdw_ref[:, ls], (ROW_BLOCK, LANES))
            for r in range(SUBLANES):
                taps = [k for k in range(CONV_WIDTH) if (lead + k) % SUBLANES == r]
                if not taps:
                    continue
                q_max = (lead + taps[-1]) // SUBLANES
                xr = x[r:r + SUBLANES * q_max + ROW_BLOCK]
                for k in taps:
                    q = (lead + k) // SUBLANES
                    acc = acc + wdw_ref[k:k + 1, ls] * xr[SUBLANES * q:SUBLANES * q + ROW_BLOCK]
            conv_scr[pl.ds(r0, ROW_BLOCK), ls] = acc
        return carry

    lax.fori_loop(0, sub // ROW_BLOCK, conv_body, 0)
    y = _layer_norm(conv_scr[...], clg_ref[...], clb_ref[...])
    pc_ref[pl.ds(base, sub), :] = (y * _sigmoid(y)).astype(BF16)


def _mix_sample_step(row0, u_ref, v_ref, hb_ref, c_ref, sp_ref, sc_ref,
                     alg_ref, alb_ref, ws0_ref, bs0_ref, wgrp_ref, bsc_ref,
                     wdw_ref, bdw_ref, clg_ref, clb_ref,
                     pa_ref, pb_ref, pc_ref, vout_ref):
    rows, d_mix = v_ref.shape
    group_dim = d_mix // len(POOL_WINDOWS)
    out_rows = slice(row0, row0 + rows)

    v = _layer_norm(v_ref[...], alg_ref[...], alb_ref[...])
    vout_ref[...] = v
    s = v.astype(BF16).astype(F32) * ws0_ref[...].astype(BF16).astype(F32) + bs0_ref[...]
    pa_ref[out_rows, :] = (u_ref[...] * s).astype(BF16)

    hb = hb_ref[...]
    sp = sp_ref[...]
    krow = lax.broadcasted_iota(jnp.int32, (POOL_BUF, group_dim), 0)
    for g, w in enumerate(POOL_WINDOWS):
        gs = slice(g * group_dim, (g + 1) * group_dim)
        keep = (krow >= POOL_BUF - (w - 1)).astype(F32)
        tok = hb[:, gs]
        win = tok + jnp.sum(sp[:, :, gs] * keep[None], axis=1)
        pooled = win / float(w) - tok
        mixed = _bdot(pooled.astype(BF16), wgrp_ref[g])
        pb_ref[out_rows, gs] = (mixed * bsc_ref[:, gs]).astype(BF16)

    conv = (jnp.sum(sc_ref[...] * wdw_ref[0:CONV_BUF, :][None], axis=1)
            + c_ref[...] * wdw_ref[CONV_BUF:CONV_WIDTH, :] + bdw_ref[...])
    y = _layer_norm(conv, clg_ref[...], clb_ref[...])
    pc_ref[out_rows, :] = (y * _sigmoid(y)).astype(BF16)


def _mix_kernel(u_ref, v_ref, hb_ref, hbh_ref, c_ref, ch_ref,
                us_ref, vs_ref, hbs_ref, cs_ref, sp_ref, sc_ref,
                alg_ref, alb_ref, ws_ref, bsb_ref, ws0_ref, bs0_ref, wgrp_ref, bsc_ref,
                wdw_ref, bdw_ref, clg_ref, clb_ref,
                pa_ref, pb_ref, pc_ref, vout_ref,
                hb_ext, c_ext, pool_scr, conv_scr, *, n_sub):
    t = pl.program_id(1)

    @pl.when(t < n_sub)
    def _():
        _mix_prompt_step(t, u_ref, v_ref, hb_ref, hbh_ref, c_ref, ch_ref,
                         alg_ref, alb_ref, ws_ref, bsb_ref, wgrp_ref, bsc_ref,
                         wdw_ref, bdw_ref, clg_ref, clb_ref,
                         pa_ref, pb_ref, pc_ref, hb_ext, c_ext, pool_scr, conv_scr)

    @pl.when(t == n_sub)
    def _():
        _mix_sample_step(n_sub * v_ref.shape[0], us_ref, vs_ref, hbs_ref, cs_ref, sp_ref, sc_ref,
                         alg_ref, alb_ref, ws0_ref, bs0_ref, wgrp_ref, bsc_ref,
                         wdw_ref, bdw_ref, clg_ref, clb_ref,
                         pa_ref, pb_ref, pc_ref, vout_ref)


def _mix(ga3, hb3, c3, state_pool, state_conv, lw, layer, seq):
    batch, tile_rows, d_mix = hb3.shape
    rows_s = tile_rows - seq
    n_sub = seq // MIX_SUB
    sub_idx = lambda t: jnp.minimum(t, n_sub - 1)
    vec = lambda: pl.BlockSpec((None, 1, d_mix), lambda b, t: (layer, 0, 0))
    sub_spec = lambda cb: pl.BlockSpec((None, MIX_SUB, d_mix), lambda b, t: (b, sub_idx(t), cb))
    smp_spec = lambda cb: pl.BlockSpec((None, rows_s, d_mix), lambda b, t: (b, seq // rows_s, cb))

    def halo_spec(rows):
        per = MIX_SUB // rows
        return pl.BlockSpec((None, rows, d_mix),
                            lambda b, t: (b, jnp.maximum(sub_idx(t) * per - 1, 0), 0))

    def state_spec(buf):
        return pl.BlockSpec((None, rows_s, buf, d_mix), lambda b, t: (layer, b, 0, 0),
                            pipeline_mode=_ONCE)

    head_spec = lambda: pl.BlockSpec((None, A_HEADS, CHUNK, CHUNK), lambda b, t: (layer, 0, 0, 0))
    tile_out = pl.BlockSpec((None, tile_rows, d_mix), lambda b, t: (b, 0, 0))
    out3 = jax.ShapeDtypeStruct((batch, tile_rows, d_mix), BF16)
    return pl.pallas_call(
        functools.partial(_mix_kernel, n_sub=n_sub),
        grid=(batch, n_sub + 1),
        in_specs=[
            sub_spec(0), sub_spec(1),
            sub_spec(0), halo_spec(POOL_HALO),
            sub_spec(0), halo_spec(CONV_HALO),
            smp_spec(0), smp_spec(1), smp_spec(0), smp_spec(0),
            state_spec(POOL_BUF), state_spec(CONV_BUF),
            vec(), vec(),
            head_spec(), head_spec(),
            vec(), vec(),
            pl.BlockSpec((None,) + lw["b_w_group"].shape[1:], lambda b, t: (layer, 0, 0, 0)),
            vec(),
            pl.BlockSpec((None, CONV_WIDTH, d_mix), lambda b, t: (layer, 0, 0)),
            vec(), vec(), vec(),
        ],
        out_specs=[tile_out, tile_out, tile_out,
                   pl.BlockSpec((rows_s, d_mix), lambda b, t: (b, 0))],
        out_shape=[out3, out3, out3, jax.ShapeDtypeStruct((batch * rows_s, d_mix), F32)],
        scratch_shapes=[
            pltpu.VMEM((POOL_HALO + MIX_SUB, d_mix), F32),
            pltpu.VMEM((CONV_HALO + MIX_SUB, d_mix), F32),
            pltpu.VMEM((MIX_SUB, d_mix), F32),
            pltpu.VMEM((MIX_SUB, d_mix), F32),
        ],
        compiler_params=_cparams(("parallel", "arbitrary"), 56),
        name="mix",
    )(ga3, ga3, hb3, hb3, c3, c3, ga3, ga3, hb3, c3, state_pool, state_conv,
      lw["a_ln_g"], lw["a_ln_b"], lw["a_ws"], lw["a_bs_b"], lw["a_ws0"], lw["a_bs0"],
      lw["b_w_group"], lw["b_scale"], lw["c_w_dw"], lw["c_b_dw"], lw["c_ln_g"], lw["c_ln_b"])


def _merge_kernel(pa_ref, pb_ref, pc_ref, wa_ref, wb_ref, wc_ref,
                  g0_ref, g1_ref, g2_ref, o_ref):
    merged = (g0_ref[...] * _bdot(pa_ref[...], wa_ref[...])
              + g1_ref[...] * _bdot(pb_ref[...], wb_ref[...])
              + g2_ref[...] * _bdot(pc_ref[...], wc_ref[...]))
    o_ref[...] = merged.astype(o_ref.dtype)


def _merge(pa, pb, pc, gates, w_a, w_b, w_c, layer, tm, tn):
    m_rows, d_mix = pa.shape
    d_model = w_a.shape[-1]
    nb = d_model // tn
    act = lambda: pl.BlockSpec((tm, d_mix), lambda m, n: (m, 0))
    wsp = lambda: pl.BlockSpec((None, d_mix, tn), lambda m, n: (layer, 0, n))
    gsp = lambda j: pl.BlockSpec((tm, tn), lambda m, n: (m, j * nb + n))
    return pl.pallas_call(
        _merge_kernel,
        grid=(m_rows // tm, nb),
        in_specs=[act(), act(), act(), wsp(), wsp(), wsp(), gsp(0), gsp(1), gsp(2)],
        out_specs=pl.BlockSpec((tm, tn), lambda m, n: (m, n)),
        out_shape=jax.ShapeDtypeStruct((m_rows, d_model), BF16),
        compiler_params=_cparams(("parallel", "arbitrary"), 52),
        name="merge",
    )(pa, pb, pc, w_a, w_b, w_c, gates, gates, gates)


def _projout_ln_kernel(m_ref, w_ref, x_ref, g_ref, b_ref, o_ref, *, alpha):
    k = pl.program_id(1)

    @pl.when(k == 0)
    def _():
        o_ref[...] = jnp.zeros_like(o_ref)

    o_ref[...] += _bdot(m_ref[...], w_ref[...])

    @pl.when(k == pl.num_programs(1) - 1)
    def _():
        _residual_ln_rows(x_ref, o_ref, g_ref, b_ref, None, alpha)


def _projout_ln(merged, w_out, x, g, b, layer, alpha, tm, tk):
    m_rows, d = x.shape
    vec = lambda: pl.BlockSpec((None, 1, d), lambda m, k: (layer, 0, 0))
    return pl.pallas_call(
        functools.partial(_projout_ln_kernel, alpha=alpha),
        grid=(m_rows // tm, merged.shape[1] // tk),
        in_specs=[
            pl.BlockSpec((tm, tk), lambda m, k: (m, k)),
            pl.BlockSpec((None, tk, d), lambda m, k: (layer, k, 0)),
            pl.BlockSpec((tm, d), lambda m, k: (m, 0), pipeline_mode=_ONCE),
            vec(), vec(),
        ],
        out_specs=pl.BlockSpec((tm, d), lambda m, k: (m, 0)),
        out_shape=jax.ShapeDtypeStruct((m_rows, d), F32),
        compiler_params=_cparams(("parallel", "arbitrary"), 48),
        name="projout_ln",
    )(merged, w_out, x, g, b)


def _ffn_ln_kernel(x_ref, wg_ref, wu_ref, wd_ref, g_ref, b_ref, of_ref, *maybe_bf, alpha):
    f = pl.program_id(1)

    @pl.when(f == 0)
    def _():
        of_ref[...] = jnp.zeros_like(of_ref)

    x = x_ref[...].astype(BF16)
    gate = _bdot(x, wg_ref[...])
    up = _bdot(x, wu_ref[...])
    hid = (gate * _sigmoid(gate) * up).astype(BF16)
    of_ref[...] += _bdot(hid, wd_ref[...])

    @pl.when(f == pl.num_programs(1) - 1)
    def _():
        _residual_ln_rows(x_ref, of_ref, g_ref, b_ref, maybe_bf[0] if maybe_bf else None, alpha)


def _ffn_ln(x, w_up, w_down, g, b, layer, alpha, tm, tf, emit_bf16):
    m_rows, d = x.shape
    d_ff = w_down.shape[1]
    nf = d_ff // tf
    row = lambda: pl.BlockSpec((tm, d), lambda m, f: (m, 0), pipeline_mode=_ONCE)
    vec = lambda: pl.BlockSpec((None, 1, d), lambda m, f: (layer, 0, 0))
    out_specs = [row()]
    out_shape = [jax.ShapeDtypeStruct((m_rows, d), F32)]
    if emit_bf16:
        out_specs.append(row())
        out_shape.append(jax.ShapeDtypeStruct((m_rows, d), BF16))
    return pl.pallas_call(
        functools.partial(_ffn_ln_kernel, alpha=alpha),
        grid=(m_rows // tm, nf),
        in_specs=[
            row(),
            pl.BlockSpec((None, d, tf), lambda m, f: (layer, 0, f)),
            pl.BlockSpec((None, d, tf), lambda m, f: (layer, 0, nf + f)),
            pl.BlockSpec((None, tf, d), lambda m, f: (layer, f, 0)),
            vec(), vec(),
        ],
        out_specs=out_specs,
        out_shape=out_shape,
        compiler_params=_cparams(("parallel", "arbitrary"), 60),
        name="ffn_ln",
    )(x, w_up, w_up, w_down, g, b)


def _state_shift_kernel(s_ref, *refs):
    new_refs, o_ref = refs[:-1], refs[-1]
    layer = pl.program_id(0)
    keep = s_ref.shape[1] - 1
    new = new_refs[0][...]
    for j in range(1, len(new_refs)):
        new = jnp.where(layer == j, new_refs[j][...], new)
    o_ref[:, 0:keep, :] = s_ref[:, 1:keep + 1, :]
    o_ref[:, keep:keep + 1, :] = new[:, None, :]


def _state_shift(state, new_rows3, seq):
    depth, _, buf, d_mix = state.shape
    batch, tile_rows, _ = new_rows3[0].shape
    rows_s = tile_rows - seq
    blk = pl.BlockSpec((None, rows_s, buf, d_mix), lambda l, b: (l, b, 0, 0))
    new_spec = pl.BlockSpec((None, rows_s, d_mix), lambda l, b: (b, seq // rows_s, 0))
    return pl.pallas_call(
        _state_shift_kernel,
        grid=(depth, batch),
        in_specs=[blk] + [new_spec] * depth,
        out_specs=blk,
        out_shape=jax.ShapeDtypeStruct(state.shape, state.dtype),
        compiler_params=_cparams(("parallel", "parallel"), 32),
        name="state_shift",
    )(state, *new_rows3)


def kernel(x_prompt, x_sample, state_pool, state_conv, w_in, b_in, a_ln_g, a_ln_b, a_ws, a_bs, w_a_out, b_w_group, b_scale, w_b_out, c_w_dw, c_b_dw, c_ln_g, c_ln_b, w_c_out, w_out, ln1_g, ln1_b, w_ffn_up, w_ffn_down, ln2_g, ln2_b):
    batch, seq, d_model = x_prompt.shape
    m_sample = x_sample.shape[0] * x_sample.shape[1]
    depth = w_in.shape[0]
    d_mix = a_ln_g.shape[-1]
    head_dim = d_mix // A_HEADS
    alpha = (2.0 * depth) ** 0.25

    assert x_sample.shape[1] == 1 and m_sample % batch == 0
    rows_s = m_sample // batch
    tile_rows = seq + rows_s
    m_rows = batch * tile_rows
    half_rows = tile_rows // 2
    assert rows_s % BF16_ROWS == 0 and seq % rows_s == 0 and seq % MIX_SUB == 0
    assert tile_rows % 2 == 0 and half_rows % LN_ROWS == 0 and LN_ROWS % BF16_ROWS == 0

    vec3 = lambda a: a.reshape(depth, 1, a.shape[-1])
    lw = {
        "a_ln_g": vec3(a_ln_g), "a_ln_b": vec3(a_ln_b),
        "a_ws": a_ws,
        "a_bs_b": jnp.broadcast_to(a_bs[..., None], a_bs.shape + (head_dim,)),
        "a_ws0": jnp.repeat(a_ws[:, :, 0, 0], head_dim, axis=-1).reshape(depth, 1, d_mix),
        "a_bs0": jnp.repeat(a_bs[:, :, 0], head_dim, axis=-1).reshape(depth, 1, d_mix),
        "b_w_group": b_w_group, "b_scale": vec3(b_scale),
        "c_w_dw": c_w_dw, "c_b_dw": vec3(c_b_dw),
        "c_ln_g": vec3(c_ln_g), "c_ln_b": vec3(c_ln_b),
    }
    b_in3 = vec3(b_in)
    ln1_g3, ln1_b3, ln2_g3, ln2_b3 = vec3(ln1_g), vec3(ln1_b), vec3(ln2_g), vec3(ln2_b)

    s1 = 2 * d_mix
    s2 = s1 + d_mix
    s3 = s2 + 2 * d_mix

    x = jnp.concatenate([x_prompt, x_sample.reshape(batch, rows_s, d_model)],
                        axis=1).reshape(m_rows, d_model)
    xb = x.astype(BF16)
    tile3 = lambda a: a.reshape(batch, tile_rows, a.shape[-1])

    hb_l, c_l, v_l = [], [], []
    for l in range(depth):
        ga = _proj(xb, w_in, b_in3, l, 0, s1, "gelu", F32, tile_rows, 512, "proj_a")
        hb = _proj(xb, w_in, b_in3, l, s1, d_mix, "none", F32, tile_rows, 512, "proj_b")
        c = _glu_proj(xb, w_in, b_in3, l, s2, d_mix, tile_rows, 256, "proj_c")
        gates = _proj(xb, w_in, b_in3, l, s3, N_BRANCH * d_model, "sigmoid", BF16,
                      tile_rows, 512, "proj_gates")

        pa, pb, pc, v_new = _mix(tile3(ga), tile3(hb), tile3(c), state_pool, state_conv,
                                 lw, l, seq)
        flat = lambda a: a.reshape(m_rows, d_mix)
        merged = _merge(flat(pa), flat(pb), flat(pc), gates, w_a_out, w_b_out, w_c_out,
                        l, tile_rows, 256)
        x1 = _projout_ln(merged, w_out, x, ln1_g3, ln1_b3, l, alpha, half_rows, 512)
        outs = _ffn_ln(x1, w_ffn_up, w_ffn_down, ln2_g3, ln2_b3, l, alpha, tile_rows, 256,
                       emit_bf16=l + 1 < depth)
        x = outs[0]
        xb = outs[1] if l + 1 < depth else None

        hb_l.append(tile3(hb))
        c_l.append(tile3(c))
        v_l.append(v_new)

    x3 = tile3(x)
    y_prompt = x3[:, :seq]
    y_sample = x3[:, seq:].reshape(m_sample, 1, d_model)
    new_pool_prompt = jnp.stack([a[:, seq - POOL_BUF:seq] for a in hb_l])
    new_conv_prompt = jnp.stack([a[:, seq - CONV_BUF:seq] for a in c_l])
    new_pool_sample = _state_shift(state_pool, hb_l, seq)
    new_conv_sample = _state_shift(state_conv, c_l, seq)
    new_chunk_v = jnp.stack(v_l)[:, :, None, :]
    return (y_prompt, y_sample, new_pool_prompt, new_conv_prompt,
            new_pool_sample, new_conv_sample, new_chunk_v)
```

```python
import functools

import jax
import jax.numpy as jnp
from jax import lax
from jax.experimental import pallas as pl
from jax.experimental.pallas import tpu as pltpu

F32 = jnp.float32
BF16 = jnp.bfloat16

LN_EPS = 1e-5
A_HEADS = 8
CHUNK = 128
POOL_WINDOWS = (2, 4, 8, 16)
POOL_BUF = 15
CONV_WIDTH = 31
CONV_BUF = CONV_WIDTH - 1
N_BRANCH = 3

SUBLANES = 8
LANES = 128
BF16_ROWS = 16
POOL_HALO = 16
CONV_HALO = 32
ROW_BLOCK = 64
MIX_SUB = 256
LN_ROWS = 208
ROW_SPLITS = 5
V7X_VMEM_BYTES = 64 * 1024 * 1024

_ONCE = pl.Buffered(1)


def _cparams(semantics, vmem_mb):
    assert vmem_mb * 1024 * 1024 < V7X_VMEM_BYTES
    return pltpu.CompilerParams(dimension_semantics=semantics,
                                vmem_limit_bytes=vmem_mb * 1024 * 1024)


def _layer_norm(x, g, b):
    mu = jnp.mean(x, axis=-1, keepdims=True)
    xc = x - mu
    var = jnp.mean(xc * xc, axis=-1, keepdims=True)
    return xc * lax.rsqrt(var + LN_EPS) * g + b


def _gelu_exact(x):
    return 0.5 * x * (1.0 + lax.erf(x * (0.5 ** 0.5)))


def _sigmoid(x):
    return 1.0 / (1.0 + jnp.exp(-x))


def _bdot(a, w):
    return jnp.dot(a, w.astype(BF16), preferred_element_type=F32)


def _residual_ln_rows(x_ref, acc_ref, g_ref, b_ref, bf_ref, alpha):
    def body(i, carry):
        r = pl.multiple_of(i * LN_ROWS, LN_ROWS)
        y = _layer_norm(alpha * x_ref[pl.ds(r, LN_ROWS), :] + acc_ref[pl.ds(r, LN_ROWS), :],
                        g_ref[...], b_ref[...])
        acc_ref[pl.ds(r, LN_ROWS), :] = y
        if bf_ref is not None:
            bf_ref[pl.ds(r, LN_ROWS), :] = y.astype(BF16)
        return carry

    lax.fori_loop(0, acc_ref.shape[0] // LN_ROWS, body, 0)


def _proj_kernel(x_ref, w_ref, b_ref, o_ref, *, act):
    w = w_ref[...].astype(BF16)
    rows = x_ref.shape[0] // ROW_SPLITS
    for i in range(ROW_SPLITS):
        rs = slice(i * rows, (i + 1) * rows)
        h = jnp.dot(x_ref[rs, :], w, preferred_element_type=F32) + b_ref[...]
        if act == "gelu":
            h = _gelu_exact(h)
        elif act == "sigmoid":
            h = _sigmoid(h)
        o_ref[rs, :] = h.astype(o_ref.dtype)


def _glu_kernel(x_ref, wv_ref, wg_ref, bv_ref, bg_ref, o_ref):
    x = x_ref[...]
    val = _bdot(x, wv_ref[...]) + bv_ref[...]
    gate = _bdot(x, wg_ref[...]) + bg_ref[...]
    o_ref[...] = val * _sigmoid(gate)


def _proj(x, w, b, layer, col0, ncols, act, out_dtype, tm, tn, name):
    m_rows, k = x.shape
    off = col0 // tn
    return pl.pallas_call(
        functools.partial(_proj_kernel, act=act),
        grid=(m_rows // tm, ncols // tn),
        in_specs=[
            pl.BlockSpec((tm, k), lambda m, n: (m, 0)),
            pl.BlockSpec((None, k, tn), lambda m, n: (layer, 0, off + n)),
            pl.BlockSpec((None, 1, tn), lambda m, n: (layer, 0, off + n)),
        ],
        out_specs=pl.BlockSpec((tm, tn), lambda m, n: (m, n)),
        out_shape=jax.ShapeDtypeStruct((m_rows, ncols), out_dtype),
        compiler_params=_cparams(("parallel", "arbitrary"), 48),
        name=name,
    )(x, w, b)


def _glu_proj(x, w, b, layer, col0, ncols, tm, tn, name):
    m_rows, k = x.shape
    off_v = col0 // tn
    off_g = (col0 + ncols) // tn
    return pl.pallas_call(
        _glu_kernel,
        grid=(m_rows // tm, ncols // tn),
        in_specs=[
            pl.BlockSpec((tm, k), lambda m, n: (m, 0)),
            pl.BlockSpec((None, k, tn), lambda m, n: (layer, 0, off_v + n)),
            pl.BlockSpec((None, k, tn), lambda m, n: (layer, 0, off_g + n)),
            pl.BlockSpec((None, 1, tn), lambda m, n: (layer, 0, off_v + n)),
            pl.BlockSpec((None, 1, tn), lambda m, n: (layer, 0, off_g + n)),
        ],
        out_specs=pl.BlockSpec((tm, tn), lambda m, n: (m, n)),
        out_shape=jax.ShapeDtypeStruct((m_rows, ncols), F32),
        compiler_params=_cparams(("parallel", "arbitrary"), 52),
        name=name,
    )(x, w, w, b, b)


def _mix_prompt_step(t, u_ref, v_ref, hb_ref, hbh_ref, c_ref, ch_ref,
                     alg_ref, alb_ref, ws_ref, bsb_ref, wgrp_ref, bsc_ref,
                     wdw_ref, bdw_ref, clg_ref, clb_ref,
                     pa_ref, pb_ref, pc_ref, hb_ext, c_ext, pool_scr, conv_scr):
    sub, d_mix = v_ref.shape
    n_chunks = sub // CHUNK
    head_dim = d_mix // A_HEADS
    group_dim = d_mix // len(POOL_WINDOWS)
    base = pl.multiple_of(t * sub, sub)

    vb = _layer_norm(v_ref[...], alg_ref[...], alb_ref[...]).astype(BF16)
    row = lax.broadcasted_iota(jnp.int32, (CHUNK, CHUNK), 0)
    col = lax.broadcasted_iota(jnp.int32, (CHUNK, CHUNK), 1)
    for h in range(A_HEADS):
        hs = slice(h * head_dim, (h + 1) * head_dim)
        wm = jnp.where(row >= col, ws_ref[h], 0.0).astype(BF16)
        rhs = jnp.concatenate(
            [vb[ci * CHUNK:(ci + 1) * CHUNK, hs] for ci in range(n_chunks)], axis=1)
        s = jnp.dot(wm, rhs, preferred_element_type=F32)
        bias = bsb_ref[h]
        for ci in range(n_chunks):
            rs = slice(ci * CHUNK, (ci + 1) * CHUNK)
            s_c = s[:, ci * head_dim:(ci + 1) * head_dim] + bias
            pa_ref[pl.ds(base + ci * CHUNK, CHUNK), hs] = (u_ref[rs, hs] * s_c).astype(BF16)

    first = t == 0
    hb_ext[0:POOL_HALO, :] = jnp.where(first, 0.0, hbh_ref[...])
    hb_ext[POOL_HALO:POOL_HALO + sub, :] = hb_ref[...]

    def pool_body(i, carry):
        r0 = pl.multiple_of(i * ROW_BLOCK, ROW_BLOCK)
        pos = base + r0 + lax.broadcasted_iota(jnp.int32, (ROW_BLOCK, group_dim), 0)
        for g, w in enumerate(POOL_WINDOWS):
            gs = slice(g * group_dim, (g + 1) * group_dim)
            x = hb_ext[pl.ds(r0, POOL_HALO + ROW_BLOCK), gs]
            tok = x[POOL_HALO:POOL_HALO + ROW_BLOCK]
            win = tok
            for k in range(1, w):
                win = win + pltpu.roll(x, k, axis=0)[POOL_HALO:POOL_HALO + ROW_BLOCK]
            cnt = jnp.minimum(pos + 1, w).astype(F32)
            pool_scr[pl.ds(r0, ROW_BLOCK), gs] = win / cnt - tok
        return carry

    lax.fori_loop(0, sub // ROW_BLOCK, pool_body, 0)
    for g in range(len(POOL_WINDOWS)):
        gs = slice(g * group_dim, (g + 1) * group_dim)
        mixed = _bdot(pool_scr[:, gs].astype(BF16), wgrp_ref[g])
        pb_ref[pl.ds(base, sub), gs] = (mixed * bsc_ref[:, gs]).astype(BF16)

    c_ext[0:CONV_HALO, :] = jnp.where(first, 0.0, ch_ref[...])
    c_ext[CONV_HALO:CONV_HALO + sub, :] = c_ref[...]
    lead = CONV_HALO - CONV_BUF

    def conv_body(i, carry):
        r0 = pl.multiple_of(i * ROW_BLOCK, ROW_BLOCK)
        for lt in range(d_mix // LANES):
            ls = slice(lt * LANES, (lt + 1) * LANES)
            x = c_ext[pl.ds(r0, CONV_HALO + ROW_BLOCK), ls]
            acc = jnp.broadcast_to(bdw_ref[:, ls], (ROW_BLOCK, LANES))
            for r in range(SUBLANES):
                taps = [k for k in range(CONV_WIDTH) if (lead + k) % SUBLANES == r]
                if not taps:
                    continue
                xr = x if r == 0 else pltpu.roll(x, x.shape[0] - r, axis=0)
                for k in taps:
                    q = (lead + k) // SUBLANES
                    acc = acc + wdw_ref[k:k + 1, ls] * xr[SUBLANES * q:SUBLANES * q + ROW_BLOCK]
            conv_scr[pl.ds(r0, ROW_BLOCK), ls] = acc
        return carry

    lax.fori_loop(0, sub // ROW_BLOCK, conv_body, 0)
    y = _layer_norm(conv_scr[...], clg_ref[...], clb_ref[...])
    pc_ref[pl.ds(base, sub), :] = (y * _sigmoid(y)).astype(BF16)


def _mix_sample_step(row0, u_ref, v_ref, hb_ref, c_ref, sp_ref, sc_ref,
                     alg_ref, alb_ref, ws0_ref, bs0_ref, wgrp_ref, bsc_ref,
                     wdw_ref, bdw_ref, clg_ref, clb_ref,
                     pa_ref, pb_ref, pc_ref, vout_ref):
    rows, d_mix = v_ref.shape
    group_dim = d_mix // len(POOL_WINDOWS)
    out_rows = slice(row0, row0 + rows)

    v = _layer_norm(v_ref[...], alg_ref[...], alb_ref[...])
    vout_ref[...] = v
    s = v.astype(BF16).astype(F32) * ws0_ref[...].astype(BF16).astype(F32) + bs0_ref[...]
    pa_ref[out_rows, :] = (u_ref[...] * s).astype(BF16)

    for g, w in enumerate(POOL_WINDOWS):
        gs = slice(g * group_dim, (g + 1) * group_dim)
        tok = hb_ref[:, gs]
        win = tok
        for k in range(POOL_BUF - (w - 1), POOL_BUF):
            win = win + sp_ref[k, :, gs]
        pooled = win / float(w) - tok
        mixed = _bdot(pooled.astype(BF16), wgrp_ref[g])
        pb_ref[out_rows, gs] = (mixed * bsc_ref[:, gs]).astype(BF16)

    conv = c_ref[...] * wdw_ref[CONV_BUF:CONV_WIDTH, :] + bdw_ref[...]
    for k in range(CONV_BUF):
        conv = conv + sc_ref[k] * wdw_ref[k:k + 1, :]
    y = _layer_norm(conv, clg_ref[...], clb_ref[...])
    pc_ref[out_rows, :] = (y * _sigmoid(y)).astype(BF16)


def _mix_kernel(u_ref, v_ref, hb_ref, hbh_ref, c_ref, ch_ref,
                us_ref, vs_ref, hbs_ref, cs_ref, sp_ref, sc_ref,
                alg_ref, alb_ref, ws_ref, bsb_ref, ws0_ref, bs0_ref, wgrp_ref, bsc_ref,
                wdw_ref, bdw_ref, clg_ref, clb_ref,
                pa_ref, pb_ref, pc_ref, vout_ref,
                hb_ext, c_ext, pool_scr, conv_scr, *, n_sub):
    t = pl.program_id(1)

    @pl.when(t < n_sub)
    def _():
        _mix_prompt_step(t, u_ref, v_ref, hb_ref, hbh_ref, c_ref, ch_ref,
                         alg_ref, alb_ref, ws_ref, bsb_ref, wgrp_ref, bsc_ref,
                         wdw_ref, bdw_ref, clg_ref, clb_ref,
                         pa_ref, pb_ref, pc_ref, hb_ext, c_ext, pool_scr, conv_scr)

    @pl.when(t == n_sub)
    def _():
        _mix_sample_step(n_sub * v_ref.shape[0], us_ref, vs_ref, hbs_ref, cs_ref, sp_ref, sc_ref,
                         alg_ref, alb_ref, ws0_ref, bs0_ref, wgrp_ref, bsc_ref,
                         wdw_ref, bdw_ref, clg_ref, clb_ref,
                         pa_ref, pb_ref, pc_ref, vout_ref)


def _mix(ga3, hb3, c3, state_pool, state_conv, lw, layer, seq):
    batch, tile_rows, d_mix = hb3.shape
    rows_s = tile_rows - seq
    n_sub = seq // MIX_SUB
    sub_idx = lambda t: jnp.minimum(t, n_sub - 1)
    vec = lambda: pl.BlockSpec((None, 1, d_mix), lambda b, t: (layer, 0, 0))
    sub_spec = lambda cb: pl.BlockSpec((None, MIX_SUB, d_mix), lambda b, t: (b, sub_idx(t), cb))
    smp_spec = lambda cb: pl.BlockSpec((None, rows_s, d_mix), lambda b, t: (b, seq // rows_s, cb))

    def halo_spec(rows):
        per = MIX_SUB // rows
        return pl.BlockSpec((None, rows, d_mix),
                            lambda b, t: (b, jnp.maximum(sub_idx(t) * per - 1, 0), 0))

    def state_spec(buf):
        return pl.BlockSpec((None, buf, rows_s, d_mix), lambda b, t: (layer, 0, b, 0),
                            pipeline_mode=_ONCE)

    head_spec = lambda: pl.BlockSpec((None, A_HEADS, CHUNK, CHUNK), lambda b, t: (layer, 0, 0, 0))
    tile_out = pl.BlockSpec((None, tile_rows, d_mix), lambda b, t: (b, 0, 0))
    out3 = jax.ShapeDtypeStruct((batch, tile_rows, d_mix), BF16)
    return pl.pallas_call(
        functools.partial(_mix_kernel, n_sub=n_sub),
        grid=(batch, n_sub + 1),
        in_specs=[
            sub_spec(0), sub_spec(1),
            sub_spec(0), halo_spec(POOL_HALO),
            sub_spec(0), halo_spec(CONV_HALO),
            smp_spec(0), smp_spec(1), smp_spec(0), smp_spec(0),
            state_spec(POOL_BUF), state_spec(CONV_BUF),
            vec(), vec(),
            head_spec(), head_spec(),
            vec(), vec(),
            pl.BlockSpec((None,) + lw["b_w_group"].shape[1:], lambda b, t: (layer, 0, 0, 0)),
            vec(),
            pl.BlockSpec((None, CONV_WIDTH, d_mix), lambda b, t: (layer, 0, 0)),
            vec(), vec(), vec(),
        ],
        out_specs=[tile_out, tile_out, tile_out,
                   pl.BlockSpec((rows_s, d_mix), lambda b, t: (b, 0))],
        out_shape=[out3, out3, out3, jax.ShapeDtypeStruct((batch * rows_s, d_mix), F32)],
        scratch_shapes=[
            pltpu.VMEM((POOL_HALO + MIX_SUB, d_mix), F32),
            pltpu.VMEM((CONV_HALO + MIX_SUB, d_mix), F32),
            pltpu.VMEM((MIX_SUB, d_mix), F32),
            pltpu.VMEM((MIX_SUB, d_mix), F32),
        ],
        compiler_params=_cparams(("parallel", "arbitrary"), 56),
        name="mix",
    )(ga3, ga3, hb3, hb3, c3, c3, ga3, ga3, hb3, c3, state_pool, state_conv,
      lw["a_ln_g"], lw["a_ln_b"], lw["a_ws"], lw["a_bs_b"], lw["a_ws0"], lw["a_bs0"],
      lw["b_w_group"], lw["b_scale"], lw["c_w_dw"], lw["c_b_dw"], lw["c_ln_g"], lw["c_ln_b"])


def _merge_kernel(pa_ref, pb_ref, pc_ref, wa_ref, wb_ref, wc_ref,
                  g0_ref, g1_ref, g2_ref, o_ref):
    merged = (g0_ref[...] * _bdot(pa_ref[...], wa_ref[...])
              + g1_ref[...] * _bdot(pb_ref[...], wb_ref[...])
              + g2_ref[...] * _bdot(pc_ref[...], wc_ref[...]))
    o_ref[...] = merged.astype(o_ref.dtype)


def _merge(pa, pb, pc, gates, w_a, w_b, w_c, layer, tm, tn):
    m_rows, d_mix = pa.shape
    d_model = w_a.shape[-1]
    nb = d_model // tn
    act = lambda: pl.BlockSpec((tm, d_mix), lambda m, n: (m, 0))
    wsp = lambda: pl.BlockSpec((None, d_mix, tn), lambda m, n: (layer, 0, n))
    gsp = lambda j: pl.BlockSpec((tm, tn), lambda m, n: (m, j * nb + n))
    return pl.pallas_call(
        _merge_kernel,
        grid=(m_rows // tm, nb),
        in_specs=[act(), act(), act(), wsp(), wsp(), wsp(), gsp(0), gsp(1), gsp(2)],
        out_specs=pl.BlockSpec((tm, tn), lambda m, n: (m, n)),
        out_shape=jax.ShapeDtypeStruct((m_rows, d_model), BF16),
        compiler_params=_cparams(("parallel", "arbitrary"), 52),
        name="merge",
    )(pa, pb, pc, w_a, w_b, w_c, gates, gates, gates)


def _projout_ln_kernel(m_ref, w_ref, x_ref, g_ref, b_ref, o_ref, *, alpha):
    k = pl.program_id(1)

    @pl.when(k == 0)
    def _():
        o_ref[...] = jnp.zeros_like(o_ref)

    o_ref[...] += _bdot(m_ref[...], w_ref[...])

    @pl.when(k == pl.num_programs(1) - 1)
    def _():
        _residual_ln_rows(x_ref, o_ref, g_ref, b_ref, None, alpha)


def _projout_ln(merged, w_out, x, g, b, layer, alpha, tm, tk):
    m_rows, d = x.shape
    vec = lambda: pl.BlockSpec((None, 1, d), lambda m, k: (layer, 0, 0))
    return pl.pallas_call(
        functools.partial(_projout_ln_kernel, alpha=alpha),
        grid=(m_rows // tm, merged.shape[1] // tk),
        in_specs=[
            pl.BlockSpec((tm, tk), lambda m, k: (m, k)),
            pl.BlockSpec((None, tk, d), lambda m, k: (layer, k, 0)),
            pl.BlockSpec((tm, d), lambda m, k: (m, 0), pipeline_mode=_ONCE),
            vec(), vec(),
        ],
        out_specs=pl.BlockSpec((tm, d), lambda m, k: (m, 0)),
        out_shape=jax.ShapeDtypeStruct((m_rows, d), F32),
        compiler_params=_cparams(("parallel", "arbitrary"), 48),
        name="projout_ln",
    )(merged, w_out, x, g, b)


def _ffn_ln_kernel(x_ref, wg_ref, wu_ref, wd_ref, g_ref, b_ref, of_ref, *maybe_bf, alpha):
    f = pl.program_id(1)

    @pl.when(f == 0)
    def _():
        of_ref[...] = jnp.zeros_like(of_ref)

    x = x_ref[...].astype(BF16)
    gate = _bdot(x, wg_ref[...])
    up = _bdot(x, wu_ref[...])
    hid = (gate * _sigmoid(gate) * up).astype(BF16)
    of_ref[...] += _bdot(hid, wd_ref[...])

    @pl.when(f == pl.num_programs(1) - 1)
    def _():
        _residual_ln_rows(x_ref, of_ref, g_ref, b_ref, maybe_bf[0] if maybe_bf else None, alpha)


def _ffn_ln(x, w_up, w_down, g, b, layer, alpha, tm, tf, emit_bf16):
    m_rows, d = x.shape
    d_ff = w_down.shape[1]
    nf = d_ff // tf
    row = lambda: pl.BlockSpec((tm, d), lambda m, f: (m, 0), pipeline_mode=_ONCE)
    vec = lambda: pl.BlockSpec((None, 1, d), lambda m, f: (layer, 0, 0))
    out_specs = [row()]
    out_shape = [jax.ShapeDtypeStruct((m_rows, d), F32)]
    if emit_bf16:
        out_specs.append(row())
        out_shape.append(jax.ShapeDtypeStruct((m_rows, d), BF16))
    return pl.pallas_call(
        functools.partial(_ffn_ln_kernel, alpha=alpha),
        grid=(m_rows // tm, nf),
        in_specs=[
            row(),
            pl.BlockSpec((None, d, tf), lambda m, f: (layer, 0, f)),
            pl.BlockSpec((None, d, tf), lambda m, f: (layer, 0, nf + f)),
            pl.BlockSpec((None, tf, d), lambda m, f: (layer, f, 0)),
            vec(), vec(),
        ],
        out_specs=out_specs,
        out_shape=out_shape,
        compiler_params=_cparams(("parallel", "arbitrary"), 60),
        name="ffn_ln",
    )(x, w_up, w_up, w_down, g, b)


def _state_shift_kernel(s_ref, *refs):
    new_refs, o_ref = refs[:-1], refs[-1]
    layer = pl.program_id(0)
    keep = s_ref.shape[0] - 1
    new = new_refs[0][...]
    for j in range(1, len(new_refs)):
        new = jnp.where(layer == j, new_refs[j][...], new)
    o_ref[0:keep] = s_ref[1:keep + 1]
    o_ref[keep] = new


def _state_shift(state_t, new_rows3, seq):
    depth, buf, _, d_mix = state_t.shape
    batch, tile_rows, _ = new_rows3[0].shape
    rows_s = tile_rows - seq
    blk = pl.BlockSpec((None, buf, rows_s, d_mix), lambda l, b: (l, 0, b, 0))
    new_spec = pl.BlockSpec((None, rows_s, d_mix), lambda l, b: (b, seq // rows_s, 0))
    return pl.pallas_call(
        _state_shift_kernel,
        grid=(depth, batch),
        in_specs=[blk] + [new_spec] * depth,
        out_specs=blk,
        out_shape=jax.ShapeDtypeStruct(state_t.shape, state_t.dtype),
        compiler_params=_cparams(("parallel", "parallel"), 32),
        name="state_shift",
    )(state_t, *new_rows3)


def kernel(x_prompt, x_sample, state_pool, state_conv, w_in, b_in, a_ln_g, a_ln_b, a_ws, a_bs, w_a_out, b_w_group, b_scale, w_b_out, c_w_dw, c_b_dw, c_ln_g, c_ln_b, w_c_out, w_out, ln1_g, ln1_b, w_ffn_up, w_ffn_down, ln2_g, ln2_b):
    batch, seq, d_model = x_prompt.shape
    m_sample = x_sample.shape[0] * x_sample.shape[1]
    depth = w_in.shape[0]
    d_mix = a_ln_g.shape[-1]
    head_dim = d_mix // A_HEADS
    alpha = (2.0 * depth) ** 0.25

    assert x_sample.shape[1] == 1 and m_sample % batch == 0
    rows_s = m_sample // batch
    tile_rows = seq + rows_s
    m_rows = batch * tile_rows
    half_rows = tile_rows // 2
    assert rows_s % BF16_ROWS == 0 and seq % rows_s == 0 and seq % MIX_SUB == 0
    assert tile_rows % 2 == 0 and half_rows % LN_ROWS == 0 and LN_ROWS % BF16_ROWS == 0

    vec3 = lambda a: a.reshape(depth, 1, a.shape[-1])
    lw = {
        "a_ln_g": vec3(a_ln_g), "a_ln_b": vec3(a_ln_b),
        "a_ws": a_ws,
        "a_bs_b": jnp.broadcast_to(a_bs[..., None], a_bs.shape + (head_dim,)),
        "a_ws0": jnp.repeat(a_ws[:, :, 0, 0], head_dim, axis=-1).reshape(depth, 1, d_mix),
        "a_bs0": jnp.repeat(a_bs[:, :, 0], head_dim, axis=-1).reshape(depth, 1, d_mix),
        "b_w_group": b_w_group, "b_scale": vec3(b_scale),
        "c_w_dw": c_w_dw, "c_b_dw": vec3(c_b_dw),
        "c_ln_g": vec3(c_ln_g), "c_ln_b": vec3(c_ln_b),
    }
    b_in3 = vec3(b_in)
    ln1_g3, ln1_b3, ln2_g3, ln2_b3 = vec3(ln1_g), vec3(ln1_b), vec3(ln2_g), vec3(ln2_b)

    s1 = 2 * d_mix
    s2 = s1 + d_mix
    s3 = s2 + 2 * d_mix

    x = jnp.concatenate([x_prompt, x_sample.reshape(batch, rows_s, d_model)],
                        axis=1).reshape(m_rows, d_model)
    xb = x.astype(BF16)
    tile3 = lambda a: a.reshape(batch, tile_rows, a.shape[-1])
    pool_t = jnp.transpose(state_pool, (0, 2, 1, 3))
    conv_t = jnp.transpose(state_conv, (0, 2, 1, 3))

    hb_l, c_l, v_l = [], [], []
    for l in range(depth):
        ga = _proj(xb, w_in, b_in3, l, 0, s1, "gelu", F32, tile_rows, 512, "proj_a")
        hb = _proj(xb, w_in, b_in3, l, s1, d_mix, "none", F32, tile_rows, 512, "proj_b")
        c = _glu_proj(xb, w_in, b_in3, l, s2, d_mix, tile_rows, 256, "proj_c")
        gates = _proj(xb, w_in, b_in3, l, s3, N_BRANCH * d_model, "sigmoid", BF16,
                      tile_rows, 512, "proj_gates")

        pa, pb, pc, v_new = _mix(tile3(ga), tile3(hb), tile3(c), pool_t, conv_t, lw, l, seq)
        flat = lambda a: a.reshape(m_rows, d_mix)
        merged = _merge(flat(pa), flat(pb), flat(pc), gates, w_a_out, w_b_out, w_c_out,
                        l, tile_rows, 256)
        x1 = _projout_ln(merged, w_out, x, ln1_g3, ln1_b3, l, alpha, half_rows, 512)
        outs = _ffn_ln(x1, w_ffn_up, w_ffn_down, ln2_g3, ln2_b3, l, alpha, tile_rows, 256,
                       emit_bf16=l + 1 < depth)
        x = outs[0]
        xb = outs[1] if l + 1 < depth else None

        hb_l.append(tile3(hb))
        c_l.append(tile3(c))
        v_l.append(v_new)

    x3 = tile3(x)
    y_prompt = x3[:, :seq]
    y_sample = x3[:, seq:].reshape(m_sample, 1, d_model)
    new_pool_prompt = jnp.stack([a[:, seq - POOL_BUF:seq] for a in hb_l])
    new_conv_prompt = jnp.stack([a[:, seq - CONV_BUF:seq] for a in c_l])
    new_pool_sample = jnp.transpose(_state_shift(pool_t, hb_l, seq), (0, 2, 1, 3))
    new_conv_sample = jnp.transpose(_state_shift(conv_t, c_l, seq), (0, 2, 1, 3))
    new_chunk_v = jnp.stack(v_l)[:, :, None, :]
    return (y_prompt, y_sample, new_pool_prompt, new_conv_prompt,
            new_pool_sample, new_conv_sample, new_chunk_v)
```

```python
import functools

import jax
import jax.numpy as jnp
from jax import lax
from jax.experimental import pallas as pl
from jax.experimental.pallas import tpu as pltpu

F32 = jnp.float32
BF16 = jnp.bfloat16

LN_EPS = 1e-5
A_HEADS = 8
CHUNK = 128
POOL_WINDOWS = (2, 4, 8, 16)
POOL_BUF = 15
CONV_WIDTH = 31
CONV_BUF = CONV_WIDTH - 1
N_BRANCH = 3

SUBLANES = 8
LANES = 128
BF16_ROWS = 16
POOL_HALO = 16
CONV_HALO = 32
ROW_BLOCK = 64
MIX_SUB = 256
LN_ROWS = 208
ROW_SPLITS = 5
V7X_VMEM_BYTES = 64 * 1024 * 1024

_ONCE = pl.Buffered(1)


def _cparams(semantics, vmem_mb):
    assert vmem_mb * 1024 * 1024 < V7X_VMEM_BYTES
    return pltpu.CompilerParams(dimension_semantics=semantics,
                                vmem_limit_bytes=vmem_mb * 1024 * 1024)


def _layer_norm(x, g, b):
    mu = jnp.mean(x, axis=-1, keepdims=True)
    xc = x - mu
    var = jnp.mean(xc * xc, axis=-1, keepdims=True)
    return xc * lax.rsqrt(var + LN_EPS) * g + b


def _gelu_exact(x):
    return 0.5 * x * (1.0 + lax.erf(x * (0.5 ** 0.5)))


def _sigmoid(x):
    return 1.0 / (1.0 + jnp.exp(-x))


def _bdot(a, w):
    return jnp.dot(a, w.astype(BF16), preferred_element_type=F32)


def _residual_ln_rows(x_ref, acc_ref, g_ref, b_ref, bf_ref, alpha):
    def body(i, carry):
        r = pl.multiple_of(i * LN_ROWS, LN_ROWS)
        y = _layer_norm(alpha * x_ref[pl.ds(r, LN_ROWS), :] + acc_ref[pl.ds(r, LN_ROWS), :],
                        g_ref[...], b_ref[...])
        acc_ref[pl.ds(r, LN_ROWS), :] = y
        if bf_ref is not None:
            bf_ref[pl.ds(r, LN_ROWS), :] = y.astype(BF16)
        return carry

    lax.fori_loop(0, acc_ref.shape[0] // LN_ROWS, body, 0)


def _proj_kernel(x_ref, w_ref, b_ref, o_ref, *, act):
    w = w_ref[...].astype(BF16)
    rows = x_ref.shape[0] // ROW_SPLITS
    for i in range(ROW_SPLITS):
        rs = slice(i * rows, (i + 1) * rows)
        h = jnp.dot(x_ref[rs, :], w, preferred_element_type=F32) + b_ref[...]
        if act == "gelu":
            h = _gelu_exact(h)
        elif act == "sigmoid":
            h = _sigmoid(h)
        o_ref[rs, :] = h.astype(o_ref.dtype)


def _proj_gelu_ln_kernel(x_ref, w_ref, b_ref, g_ref, beta_ref, o_ref, os_ref):
    w = w_ref[...].astype(BF16)
    rows = x_ref.shape[0] // ROW_SPLITS
    n_s = os_ref.shape[0]
    for i in range(ROW_SPLITS):
        rs = slice(i * rows, (i + 1) * rows)
        h = jnp.dot(x_ref[rs, :], w, preferred_element_type=F32) + b_ref[...]
        v = _layer_norm(_gelu_exact(h), g_ref[...], beta_ref[...])
        o_ref[rs, :] = v.astype(BF16)
        if i == ROW_SPLITS - 1:
            os_ref[...] = v[rows - n_s:, :]


def _glu_kernel(x_ref, wv_ref, wg_ref, bv_ref, bg_ref, o_ref):
    wv = wv_ref[...].astype(BF16)
    wg = wg_ref[...].astype(BF16)
    rows = x_ref.shape[0] // ROW_SPLITS
    for i in range(ROW_SPLITS):
        rs = slice(i * rows, (i + 1) * rows)
        x = x_ref[rs, :]
        val = jnp.dot(x, wv, preferred_element_type=F32) + bv_ref[...]
        gate = jnp.dot(x, wg, preferred_element_type=F32) + bg_ref[...]
        o_ref[rs, :] = val * _sigmoid(gate)


def _proj(x, w, b, layer, col0, ncols, act, out_dtype, tm, tn, name):
    m_rows, k = x.shape
    off = col0 // tn
    return pl.pallas_call(
        functools.partial(_proj_kernel, act=act),
        grid=(m_rows // tm, ncols // tn),
        in_specs=[
            pl.BlockSpec((tm, k), lambda m, n: (m, 0)),
            pl.BlockSpec((None, k, tn), lambda m, n: (layer, 0, off + n)),
            pl.BlockSpec((None, 1, tn), lambda m, n: (layer, 0, off + n)),
        ],
        out_specs=pl.BlockSpec((tm, tn), lambda m, n: (m, n)),
        out_shape=jax.ShapeDtypeStruct((m_rows, ncols), out_dtype),
        compiler_params=_cparams(("parallel", "arbitrary"), 56),
        name=name,
    )(x, w, b)


def _proj_gelu_ln(x, w, b, g, beta, layer, col0, ncols, tm, rows_s, name):
    m_rows, k = x.shape
    off = col0 // ncols
    vec = lambda: pl.BlockSpec((None, 1, ncols), lambda m: (layer, 0, 0))
    return pl.pallas_call(
        _proj_gelu_ln_kernel,
        grid=(m_rows // tm,),
        in_specs=[
            pl.BlockSpec((tm, k), lambda m: (m, 0)),
            pl.BlockSpec((None, k, ncols), lambda m: (layer, 0, off)),
            pl.BlockSpec((None, 1, ncols), lambda m: (layer, 0, off)),
            vec(), vec(),
        ],
        out_specs=[pl.BlockSpec((tm, ncols), lambda m: (m, 0)),
                   pl.BlockSpec((rows_s, ncols), lambda m: (m, 0))],
        out_shape=[jax.ShapeDtypeStruct((m_rows, ncols), BF16),
                   jax.ShapeDtypeStruct((m_rows // tm * rows_s, ncols), F32)],
        compiler_params=_cparams(("parallel",), 52),
        name=name,
    )(x, w, b, g, beta)


def _glu_proj(x, w, b, layer, col0, ncols, tm, tn, name):
    m_rows, k = x.shape
    off_v = col0 // tn
    off_g = (col0 + ncols) // tn
    return pl.pallas_call(
        _glu_kernel,
        grid=(m_rows // tm, ncols // tn),
        in_specs=[
            pl.BlockSpec((tm, k), lambda m, n: (m, 0)),
            pl.BlockSpec((None, k, tn), lambda m, n: (layer, 0, off_v + n)),
            pl.BlockSpec((None, k, tn), lambda m, n: (layer, 0, off_g + n)),
            pl.BlockSpec((None, 1, tn), lambda m, n: (layer, 0, off_v + n)),
            pl.BlockSpec((None, 1, tn), lambda m, n: (layer, 0, off_g + n)),
        ],
        out_specs=pl.BlockSpec((tm, tn), lambda m, n: (m, n)),
        out_shape=jax.ShapeDtypeStruct((m_rows, ncols), F32),
        compiler_params=_cparams(("parallel", "arbitrary"), 56),
        name=name,
    )(x, w, w, b, b)


def _mix_prompt_step(t, u_ref, v_ref, hb_ref, hbh_ref, c_ref, ch_ref,
                     ws_ref, bsb_ref, wgrp_ref, bsc_ref,
                     wdw_ref, bdw_ref, clg_ref, clb_ref,
                     pa_ref, pb_ref, pc_ref, hb_ext, c_ext, pool_scr, conv_scr):
    sub, d_mix = v_ref.shape
    n_chunks = sub // CHUNK
    head_dim = d_mix // A_HEADS
    group_dim = d_mix // len(POOL_WINDOWS)
    base = pl.multiple_of(t * sub, sub)

    vb = v_ref[...]
    row = lax.broadcasted_iota(jnp.int32, (CHUNK, CHUNK), 0)
    col = lax.broadcasted_iota(jnp.int32, (CHUNK, CHUNK), 1)
    for h in range(A_HEADS):
        hs = slice(h * head_dim, (h + 1) * head_dim)
        wm = jnp.where(row >= col, ws_ref[h], 0.0).astype(BF16)
        rhs = jnp.concatenate(
            [vb[ci * CHUNK:(ci + 1) * CHUNK, hs] for ci in range(n_chunks)], axis=1)
        s = jnp.dot(wm, rhs, preferred_element_type=F32)
        bias = bsb_ref[h]
        for ci in range(n_chunks):
            rs = slice(ci * CHUNK, (ci + 1) * CHUNK)
            s_c = s[:, ci * head_dim:(ci + 1) * head_dim] + bias
            pa_ref[pl.ds(base + ci * CHUNK, CHUNK), hs] = (u_ref[rs, hs] * s_c).astype(BF16)

    first = t == 0
    hb_ext[0:POOL_HALO, :] = jnp.where(first, 0.0, hbh_ref[...])
    hb_ext[POOL_HALO:POOL_HALO + sub, :] = hb_ref[...]

    def pool_body(i, carry):
        r0 = pl.multiple_of(i * ROW_BLOCK, ROW_BLOCK)
        pos = base + r0 + lax.broadcasted_iota(jnp.int32, (ROW_BLOCK, group_dim), 0)
        for g, w in enumerate(POOL_WINDOWS):
            gs = slice(g * group_dim, (g + 1) * group_dim)
            x = hb_ext[pl.ds(r0, POOL_HALO + ROW_BLOCK), gs]
            tok = x[POOL_HALO:POOL_HALO + ROW_BLOCK]
            win = tok
            for k in range(1, w):
                win = win + pltpu.roll(x, k, axis=0)[POOL_HALO:POOL_HALO + ROW_BLOCK]
            cnt = jnp.minimum(pos + 1, w).astype(F32)
            pool_scr[pl.ds(r0, ROW_BLOCK), gs] = win / cnt - tok
        return carry

    lax.fori_loop(0, sub // ROW_BLOCK, pool_body, 0)
    for g in range(len(POOL_WINDOWS)):
        gs = slice(g * group_dim, (g + 1) * group_dim)
        mixed = _bdot(pool_scr[:, gs].astype(BF16), wgrp_ref[g])
        pb_ref[pl.ds(base, sub), gs] = (mixed * bsc_ref[:, gs]).astype(BF16)

    c_ext[0:CONV_HALO, :] = jnp.where(first, 0.0, ch_ref[...])
    c_ext[CONV_HALO:CONV_HALO + sub, :] = c_ref[...]
    lead = CONV_HALO - CONV_BUF

    def conv_body(i, carry):
        r0 = pl.multiple_of(i * ROW_BLOCK, ROW_BLOCK)
        for lt in range(d_mix // LANES):
            ls = slice(lt * LANES, (lt + 1) * LANES)
            x = c_ext[pl.ds(r0, CONV_HALO + ROW_BLOCK), ls]
            acc = jnp.broadcast_to(bdw_ref[:, ls], (ROW_BLOCK, LANES))
            for r in range(SUBLANES):
                taps = [k for k in range(CONV_WIDTH) if (lead + k) % SUBLANES == r]
                if not taps:
                    continue
                xr = x if r == 0 else pltpu.roll(x, x.shape[0] - r, axis=0)
                for k in taps:
                    q = (lead + k) // SUBLANES
                    acc = acc + wdw_ref[k:k + 1, ls] * xr[SUBLANES * q:SUBLANES * q + ROW_BLOCK]
            conv_scr[pl.ds(r0, ROW_BLOCK), ls] = acc
        return carry

    lax.fori_loop(0, sub // ROW_BLOCK, conv_body, 0)
    y = _layer_norm(conv_scr[...], clg_ref[...], clb_ref[...])
    pc_ref[pl.ds(base, sub), :] = (y * _sigmoid(y)).astype(BF16)


def _mix_sample_step(row0, u_ref, v_ref, hb_ref, c_ref, sp_ref, sc_ref,
                     ws0_ref, bs0_ref, wgrp_ref, bsc_ref,
                     wdw_ref, bdw_ref, clg_ref, clb_ref,
                     pa_ref, pb_ref, pc_ref):
    rows, d_mix = v_ref.shape
    group_dim = d_mix // len(POOL_WINDOWS)
    out_rows = slice(row0, row0 + rows)

    s = v_ref[...].astype(F32) * ws0_ref[...].astype(BF16).astype(F32) + bs0_ref[...]
    pa_ref[out_rows, :] = (u_ref[...] * s).astype(BF16)

    for g, w in enumerate(POOL_WINDOWS):
        gs = slice(g * group_dim, (g + 1) * group_dim)
        tok = hb_ref[:, gs]
        win = tok
        for k in range(POOL_BUF - (w - 1), POOL_BUF):
            win = win + sp_ref[k, :, gs]
        pooled = win / float(w) - tok
        mixed = _bdot(pooled.astype(BF16), wgrp_ref[g])
        pb_ref[out_rows, gs] = (mixed * bsc_ref[:, gs]).astype(BF16)

    conv = c_ref[...] * wdw_ref[CONV_BUF:CONV_WIDTH, :] + bdw_ref[...]
    for k in range(CONV_BUF):
        conv = conv + sc_ref[k] * wdw_ref[k:k + 1, :]
    y = _layer_norm(conv, clg_ref[...], clb_ref[...])
    pc_ref[out_rows, :] = (y * _sigmoid(y)).astype(BF16)


def _mix_kernel(u_ref, v_ref, hb_ref, hbh_ref, c_ref, ch_ref,
                us_ref, vs_ref, hbs_ref, cs_ref, sp_ref, sc_ref,
                ws_ref, bsb_ref, ws0_ref, bs0_ref, wgrp_ref, bsc_ref,
                wdw_ref, bdw_ref, clg_ref, clb_ref,
                pa_ref, pb_ref, pc_ref,
                hb_ext, c_ext, pool_scr, conv_scr, *, n_sub):
    t = pl.program_id(1)

    @pl.when(t < n_sub)
    def _():
        _mix_prompt_step(t, u_ref, v_ref, hb_ref, hbh_ref, c_ref, ch_ref,
                         ws_ref, bsb_ref, wgrp_ref, bsc_ref,
                         wdw_ref, bdw_ref, clg_ref, clb_ref,
                         pa_ref, pb_ref, pc_ref, hb_ext, c_ext, pool_scr, conv_scr)

    @pl.when(t == n_sub)
    def _():
        _mix_sample_step(n_sub * v_ref.shape[0], us_ref, vs_ref, hbs_ref, cs_ref, sp_ref, sc_ref,
                         ws0_ref, bs0_ref, wgrp_ref, bsc_ref,
                         wdw_ref, bdw_ref, clg_ref, clb_ref,
                         pa_ref, pb_ref, pc_ref)


def _mix(u3, v3, hb3, c3, state_pool, state_conv, lw, layer, seq):
    batch, tile_rows, d_mix = hb3.shape
    rows_s = tile_rows - seq
    n_sub = seq // MIX_SUB
    sub_idx = lambda t: jnp.minimum(t, n_sub - 1)
    vec = lambda: pl.BlockSpec((None, 1, d_mix), lambda b, t: (layer, 0, 0))
    sub_spec = lambda: pl.BlockSpec((None, MIX_SUB, d_mix), lambda b, t: (b, sub_idx(t), 0))
    smp_spec = lambda: pl.BlockSpec((None, rows_s, d_mix), lambda b, t: (b, seq // rows_s, 0))

    def halo_spec(rows):
        per = MIX_SUB // rows
        return pl.BlockSpec((None, rows, d_mix),
                            lambda b, t: (b, jnp.maximum(sub_idx(t) * per - 1, 0), 0))

    def state_spec(buf):
        return pl.BlockSpec((None, buf, rows_s, d_mix), lambda b, t: (layer, 0, b, 0),
                            pipeline_mode=_ONCE)

    head_spec = lambda: pl.BlockSpec((None, A_HEADS, CHUNK, CHUNK), lambda b, t: (layer, 0, 0, 0))
    tile_out = pl.BlockSpec((None, tile_rows, d_mix), lambda b, t: (b, 0, 0))
    out3 = jax.ShapeDtypeStruct((batch, tile_rows, d_mix), BF16)
    return pl.pallas_call(
        functools.partial(_mix_kernel, n_sub=n_sub),
        grid=(batch, n_sub + 1),
        in_specs=[
            sub_spec(), sub_spec(),
            sub_spec(), halo_spec(POOL_HALO),
            sub_spec(), halo_spec(CONV_HALO),
            smp_spec(), smp_spec(), smp_spec(), smp_spec(),
            state_spec(POOL_BUF), state_spec(CONV_BUF),
            head_spec(), head_spec(),
            vec(), vec(),
            pl.BlockSpec((None,) + lw["b_w_group"].shape[1:], lambda b, t: (layer, 0, 0, 0)),
            vec(),
            pl.BlockSpec((None, CONV_WIDTH, d_mix), lambda b, t: (layer, 0, 0)),
            vec(), vec(), vec(),
        ],
        out_specs=[tile_out, tile_out, tile_out],
        out_shape=[out3, out3, out3],
        scratch_shapes=[
            pltpu.VMEM((POOL_HALO + MIX_SUB, d_mix), F32),
            pltpu.VMEM((CONV_HALO + MIX_SUB, d_mix), F32),
            pltpu.VMEM((MIX_SUB, d_mix), F32),
            pltpu.VMEM((MIX_SUB, d_mix), F32),
        ],
        compiler_params=_cparams(("parallel", "arbitrary"), 56),
        name="mix",
    )(u3, v3, hb3, hb3, c3, c3, u3, v3, hb3, c3, state_pool, state_conv,
      lw["a_ws"], lw["a_bs_b"], lw["a_ws0"], lw["a_bs0"],
      lw["b_w_group"], lw["b_scale"], lw["c_w_dw"], lw["c_b_dw"], lw["c_ln_g"], lw["c_ln_b"])


def _merge_kernel(pa_ref, pb_ref, pc_ref, wa_ref, wb_ref, wc_ref,
                  g0_ref, g1_ref, g2_ref, o_ref):
    merged = (g0_ref[...] * _bdot(pa_ref[...], wa_ref[...])
              + g1_ref[...] * _bdot(pb_ref[...], wb_ref[...])
              + g2_ref[...] * _bdot(pc_ref[...], wc_ref[...]))
    o_ref[...] = merged.astype(o_ref.dtype)


def _merge(pa, pb, pc, gates, w_a, w_b, w_c, layer, tm, tn):
    m_rows, d_mix = pa.shape
    d_model = w_a.shape[-1]
    nb = d_model // tn
    act = lambda: pl.BlockSpec((tm, d_mix), lambda m, n: (m, 0))
    wsp = lambda: pl.BlockSpec((None, d_mix, tn), lambda m, n: (layer, 0, n))
    gsp = lambda j: pl.BlockSpec((tm, tn), lambda m, n: (m, j * nb + n))
    return pl.pallas_call(
        _merge_kernel,
        grid=(m_rows // tm, nb),
        in_specs=[act(), act(), act(), wsp(), wsp(), wsp(), gsp(0), gsp(1), gsp(2)],
        out_specs=pl.BlockSpec((tm, tn), lambda m, n: (m, n)),
        out_shape=jax.ShapeDtypeStruct((m_rows, d_model), BF16),
        compiler_params=_cparams(("parallel", "arbitrary"), 52),
        name="merge",
    )(pa, pb, pc, w_a, w_b, w_c, gates, gates, gates)


def _projout_ln_kernel(m_ref, w_ref, x_ref, g_ref, b_ref, o_ref, *, alpha):
    k = pl.program_id(1)

    @pl.when(k == 0)
    def _():
        o_ref[...] = jnp.zeros_like(o_ref)

    o_ref[...] += _bdot(m_ref[...], w_ref[...])

    @pl.when(k == pl.num_programs(1) - 1)
    def _():
        _residual_ln_rows(x_ref, o_ref, g_ref, b_ref, None, alpha)


def _projout_ln(merged, w_out, x, g, b, layer, alpha, tm, tk):
    m_rows, d = x.shape
    vec = lambda: pl.BlockSpec((None, 1, d), lambda m, k: (layer, 0, 0))
    return pl.pallas_call(
        functools.partial(_projout_ln_kernel, alpha=alpha),
        grid=(m_rows // tm, merged.shape[1] // tk),
        in_specs=[
            pl.BlockSpec((tm, tk), lambda m, k: (m, k)),
            pl.BlockSpec((None, tk, d), lambda m, k: (layer, k, 0)),
            pl.BlockSpec((tm, d), lambda m, k: (m, 0), pipeline_mode=_ONCE),
            vec(), vec(),
        ],
        out_specs=pl.BlockSpec((tm, d), lambda m, k: (m, 0)),
        out_shape=jax.ShapeDtypeStruct((m_rows, d), F32),
        compiler_params=_cparams(("parallel", "arbitrary"), 48),
        name="projout_ln",
    )(merged, w_out, x, g, b)


def _ffn_ln_kernel(x_ref, wg_ref, wu_ref, wd_ref, g_ref, b_ref, of_ref, *maybe_bf, alpha):
    f = pl.program_id(1)

    @pl.when(f == 0)
    def _():
        of_ref[...] = jnp.zeros_like(of_ref)

    x = x_ref[...].astype(BF16)
    gate = _bdot(x, wg_ref[...])
    up = _bdot(x, wu_ref[...])
    hid = (gate * _sigmoid(gate) * up).astype(BF16)
    of_ref[...] += _bdot(hid, wd_ref[...])

    @pl.when(f == pl.num_programs(1) - 1)
    def _():
        _residual_ln_rows(x_ref, of_ref, g_ref, b_ref, maybe_bf[0] if maybe_bf else None, alpha)


def _ffn_ln(x, w_up, w_down, g, b, layer, alpha, tm, tf, emit_bf16):
    m_rows, d = x.shape
    d_ff = w_down.shape[1]
    nf = d_ff // tf
    row = lambda: pl.BlockSpec((tm, d), lambda m, f: (m, 0), pipeline_mode=_ONCE)
    vec = lambda: pl.BlockSpec((None, 1, d), lambda m, f: (layer, 0, 0))
    out_specs = [row()]
    out_shape = [jax.ShapeDtypeStruct((m_rows, d), F32)]
    if emit_bf16:
        out_specs.append(row())
        out_shape.append(jax.ShapeDtypeStruct((m_rows, d), BF16))
    return pl.pallas_call(
        functools.partial(_ffn_ln_kernel, alpha=alpha),
        grid=(m_rows // tm, nf),
        in_specs=[
            row(),
            pl.BlockSpec((None, d, tf), lambda m, f: (layer, 0, f)),
            pl.BlockSpec((None, d, tf), lambda m, f: (layer, 0, nf + f)),
            pl.BlockSpec((None, tf, d), lambda m, f: (layer, f, 0)),
            vec(), vec(),
        ],
        out_specs=out_specs,
        out_shape=out_shape,
        compiler_params=_cparams(("parallel", "arbitrary"), 60),
        name="ffn_ln",
    )(x, w_up, w_up, w_down, g, b)


def _state_shift_kernel(s_ref, *refs):
    new_refs, o_ref = refs[:-1], refs[-1]
    layer = pl.program_id(0)
    keep = s_ref.shape[0] - 1
    new = new_refs[0][...]
    for j in range(1, len(new_refs)):
        new = jnp.where(layer == j, new_refs[j][...], new)
    o_ref[0:keep] = s_ref[1:keep + 1]
    o_ref[keep] = new


def _state_shift(state_t, new_rows3, seq):
    depth, buf, _, d_mix = state_t.shape
    batch, tile_rows, _ = new_rows3[0].shape
    rows_s = tile_rows - seq
    blk = pl.BlockSpec((None, buf, rows_s, d_mix), lambda l, b: (l, 0, b, 0))
    new_spec = pl.BlockSpec((None, rows_s, d_mix), lambda l, b: (b, seq // rows_s, 0))
    return pl.pallas_call(
        _state_shift_kernel,
        grid=(depth, batch),
        in_specs=[blk] + [new_spec] * depth,
        out_specs=blk,
        out_shape=jax.ShapeDtypeStruct(state_t.shape, state_t.dtype),
        compiler_params=_cparams(("parallel", "parallel"), 32),
        name="state_shift",
    )(state_t, *new_rows3)


def kernel(x_prompt, x_sample, state_pool, state_conv, w_in, b_in, a_ln_g, a_ln_b, a_ws, a_bs, w_a_out, b_w_group, b_scale, w_b_out, c_w_dw, c_b_dw, c_ln_g, c_ln_b, w_c_out, w_out, ln1_g, ln1_b, w_ffn_up, w_ffn_down, ln2_g, ln2_b):
    batch, seq, d_model = x_prompt.shape
    m_sample = x_sample.shape[0] * x_sample.shape[1]
    depth = w_in.shape[0]
    d_mix = a_ln_g.shape[-1]
    head_dim = d_mix // A_HEADS
    alpha = (2.0 * depth) ** 0.25

    assert x_sample.shape[1] == 1 and m_sample % batch == 0
    rows_s = m_sample // batch
    tile_rows = seq + rows_s
    m_rows = batch * tile_rows
    half_rows = tile_rows // 2
    assert rows_s % BF16_ROWS == 0 and seq % rows_s == 0 and seq % MIX_SUB == 0
    assert tile_rows % 2 == 0 and half_rows % LN_ROWS == 0 and LN_ROWS % BF16_ROWS == 0
    assert tile_rows % (ROW_SPLITS * BF16_ROWS) == 0 and rows_s <= tile_rows // ROW_SPLITS

    vec3 = lambda a: a.reshape(depth, 1, a.shape[-1])
    lw = {
        "a_ln_g": vec3(a_ln_g), "a_ln_b": vec3(a_ln_b),
        "a_ws": a_ws,
        "a_bs_b": jnp.broadcast_to(a_bs[..., None], a_bs.shape + (head_dim,)),
        "a_ws0": jnp.repeat(a_ws[:, :, 0, 0], head_dim, axis=-1).reshape(depth, 1, d_mix),
        "a_bs0": jnp.repeat(a_bs[:, :, 0], head_dim, axis=-1).reshape(depth, 1, d_mix),
        "b_w_group": b_w_group, "b_scale": vec3(b_scale),
        "c_w_dw": c_w_dw, "c_b_dw": vec3(c_b_dw),
        "c_ln_g": vec3(c_ln_g), "c_ln_b": vec3(c_ln_b),
    }
    b_in3 = vec3(b_in)
    ln1_g3, ln1_b3, ln2_g3, ln2_b3 = vec3(ln1_g), vec3(ln1_b), vec3(ln2_g), vec3(ln2_b)

    s1 = 2 * d_mix
    s2 = s1 + d_mix
    s3 = s2 + 2 * d_mix

    x = jnp.concatenate([x_prompt, x_sample.reshape(batch, rows_s, d_model)],
                        axis=1).reshape(m_rows, d_model)
    xb = x.astype(BF16)
    tile3 = lambda a: a.reshape(batch, tile_rows, a.shape[-1])
    pool_t = jnp.transpose(state_pool, (0, 2, 1, 3))
    conv_t = jnp.transpose(state_conv, (0, 2, 1, 3))

    hb_l, c_l, v_l = [], [], []
    for l in range(depth):
        u = _proj(xb, w_in, b_in3, l, 0, d_mix, "gelu", BF16, tile_rows, d_mix, "proj_u")
        v, v_new = _proj_gelu_ln(xb, w_in, b_in3, lw["a_ln_g"], lw["a_ln_b"], l, d_mix, d_mix,
                                 tile_rows, rows_s, "proj_v")
        hb = _proj(xb, w_in, b_in3, l, s1, d_mix, "none", F32, tile_rows, d_mix, "proj_b")
        c = _glu_proj(xb, w_in, b_in3, l, s2, d_mix, tile_rows, 512, "proj_c")
        gates = _proj(xb, w_in, b_in3, l, s3, N_BRANCH * d_model, "sigmoid", BF16,
                      tile_rows, 1024, "proj_gates")

        pa, pb, pc = _mix(tile3(u), tile3(v), tile3(hb), tile3(c), pool_t, conv_t, lw, l, seq)
        flat = lambda a: a.reshape(m_rows, d_mix)
        merged = _merge(flat(pa), flat(pb), flat(pc), gates, w_a_out, w_b_out, w_c_out,
                        l, tile_rows, 256)
        x1 = _projout_ln(merged, w_out, x, ln1_g3, ln1_b3, l, alpha, half_rows, 512)
        outs = _ffn_ln(x1, w_ffn_up, w_ffn_down, ln2_g3, ln2_b3, l, alpha, tile_rows, 256,
                       emit_bf16=l + 1 < depth)
        x = outs[0]
        xb = outs[1] if l + 1 < depth else None

        hb_l.append(tile3(hb))
        c_l.append(tile3(c))
        v_l.append(v_new)

    x3 = tile3(x)
    y_prompt = x3[:, :seq]
    y_sample = x3[:, seq:].reshape(m_sample, 1, d_model)
    new_pool_prompt = jnp.stack([a[:, seq - POOL_BUF:seq] for a in hb_l])
    new_conv_prompt = jnp.stack([a[:, seq - CONV_BUF:seq] for a in c_l])
    new_pool_sample = jnp.transpose(_state_shift(pool_t, hb_l, seq), (0, 2, 1, 3))
    new_conv_sample = jnp.transpose(_state_shift(conv_t, c_l, seq), (0, 2, 1, 3))
    new_chunk_v = jnp.stack(v_l)[:, :, None, :]
    return (y_prompt, y_sample, new_pool_prompt, new_conv_prompt,
            new_pool_sample, new_conv_sample, new_chunk_v)
```

```python
import functools

import jax
import jax.numpy as jnp
from jax import lax
from jax.experimental import pallas as pl
from jax.experimental.pallas import tpu as pltpu

F32 = jnp.float32
BF16 = jnp.bfloat16

LN_EPS = 1e-5
A_HEADS = 8
CHUNK = 128
POOL_WINDOWS = (2, 4, 8, 16)
POOL_BUF = 15
CONV_WIDTH = 31
CONV_BUF = CONV_WIDTH - 1
N_BRANCH = 3

SUBLANES = 8
LANES = 128
BF16_ROWS = 16
POOL_HALO = 16
CONV_HALO = 32
ROW_BLOCK = 64
MIX_SUB = 256
PACK_SUB = 512
LN_ROWS = 208
ROW_SPLITS = 5
V7X_VMEM_BYTES = 64 * 1024 * 1024

_ONCE = pl.Buffered(1)


def _cparams(semantics, vmem_mb):
    assert vmem_mb * 1024 * 1024 < V7X_VMEM_BYTES
    return pltpu.CompilerParams(dimension_semantics=semantics,
                                vmem_limit_bytes=vmem_mb * 1024 * 1024)


def _layer_norm(x, g, b):
    mu = jnp.mean(x, axis=-1, keepdims=True)
    xc = x - mu
    var = jnp.mean(xc * xc, axis=-1, keepdims=True)
    return xc * lax.rsqrt(var + LN_EPS) * g + b


def _gelu_exact(x):
    return 0.5 * x * (1.0 + lax.erf(x * (0.5 ** 0.5)))


def _sigmoid(x):
    return 1.0 / (1.0 + jnp.exp(-x))


def _bdot(a, w):
    return jnp.dot(a, w.astype(BF16), preferred_element_type=F32)


def _residual_ln_chunk(x_rows, acc_ref, g_ref, b_ref, bf_ref, alpha, r):
    y = _layer_norm(alpha * x_rows + acc_ref[pl.ds(r, LN_ROWS), :], g_ref[...], b_ref[...])
    acc_ref[pl.ds(r, LN_ROWS), :] = y
    if bf_ref is not None:
        bf_ref[pl.ds(r, LN_ROWS), :] = y.astype(BF16)


def _residual_ln_rows(x_ref, acc_ref, g_ref, b_ref, bf_ref, alpha, n_chunks):
    def body(i, carry):
        r = pl.multiple_of(i * LN_ROWS, LN_ROWS)
        _residual_ln_chunk(x_ref[pl.ds(r, LN_ROWS), :], acc_ref, g_ref, b_ref, bf_ref, alpha, r)
        return carry

    lax.fori_loop(0, n_chunks, body, 0)


def _pack_kernel(xp_ref, xs_ref, o_ref, *, n_sub):
    t = pl.program_id(1)
    sub = xp_ref.shape[0]

    @pl.when(t < n_sub)
    def _():
        o_ref[pl.ds(pl.multiple_of(t * sub, sub), sub), :] = xp_ref[...].astype(BF16)

    @pl.when(t == n_sub)
    def _():
        o_ref[n_sub * sub:, :] = xs_ref[...].astype(BF16)


def _pack_bf16(x_prompt, x_sample3):
    batch, seq, d = x_prompt.shape
    rows_s = x_sample3.shape[1]
    n_sub = seq // PACK_SUB
    return pl.pallas_call(
        functools.partial(_pack_kernel, n_sub=n_sub),
        grid=(batch, n_sub + 1),
        in_specs=[
            pl.BlockSpec((None, PACK_SUB, d), lambda b, t: (b, jnp.minimum(t, n_sub - 1), 0)),
            pl.BlockSpec((None, rows_s, d), lambda b, t: (b, 0, 0)),
        ],
        out_specs=pl.BlockSpec((None, seq + rows_s, d), lambda b, t: (b, 0, 0)),
        out_shape=jax.ShapeDtypeStruct((batch, seq + rows_s, d), BF16),
        compiler_params=_cparams(("parallel", "arbitrary"), 40),
        name="pack",
    )(x_prompt, x_sample3)


def _proj_kernel(x_ref, w_ref, b_ref, o_ref, *, act):
    w = w_ref[...].astype(BF16)
    rows = x_ref.shape[0] // ROW_SPLITS
    for i in range(ROW_SPLITS):
        rs = slice(i * rows, (i + 1) * rows)
        h = jnp.dot(x_ref[rs, :], w, preferred_element_type=F32) + b_ref[...]
        if act == "gelu":
            h = _gelu_exact(h)
        elif act == "sigmoid":
            h = _sigmoid(h)
        o_ref[rs, :] = h.astype(o_ref.dtype)


def _proj_gelu_ln_kernel(x_ref, w_ref, b_ref, g_ref, beta_ref, o_ref, os_ref):
    w = w_ref[...].astype(BF16)
    rows = x_ref.shape[0] // ROW_SPLITS
    n_s = os_ref.shape[0]
    for i in range(ROW_SPLITS):
        rs = slice(i * rows, (i + 1) * rows)
        h = jnp.dot(x_ref[rs, :], w, preferred_element_type=F32) + b_ref[...]
        v = _layer_norm(_gelu_exact(h), g_ref[...], beta_ref[...])
        o_ref[rs, :] = v.astype(BF16)
        if i == ROW_SPLITS - 1:
            os_ref[...] = v[rows - n_s:, :]


def _glu_kernel(x_ref, wv_ref, wg_ref, bv_ref, bg_ref, o_ref):
    wv = wv_ref[...].astype(BF16)
    wg = wg_ref[...].astype(BF16)
    rows = x_ref.shape[0] // ROW_SPLITS
    for i in range(ROW_SPLITS):
        rs = slice(i * rows, (i + 1) * rows)
        x = x_ref[rs, :]
        val = jnp.dot(x, wv, preferred_element_type=F32) + bv_ref[...]
        gate = jnp.dot(x, wg, preferred_element_type=F32) + bg_ref[...]
        o_ref[rs, :] = val * _sigmoid(gate)


def _proj(x, w, b, layer, col0, ncols, act, out_dtype, tm, tn, name):
    m_rows, k = x.shape
    off = col0 // tn
    return pl.pallas_call(
        functools.partial(_proj_kernel, act=act),
        grid=(m_rows // tm, ncols // tn),
        in_specs=[
            pl.BlockSpec((tm, k), lambda m, n: (m, 0)),
            pl.BlockSpec((None, k, tn), lambda m, n: (layer, 0, off + n)),
            pl.BlockSpec((None, 1, tn), lambda m, n: (layer, 0, off + n)),
        ],
        out_specs=pl.BlockSpec((tm, tn), lambda m, n: (m, n)),
        out_shape=jax.ShapeDtypeStruct((m_rows, ncols), out_dtype),
        compiler_params=_cparams(("parallel", "arbitrary"), 56),
        name=name,
    )(x, w, b)


def _proj_gelu_ln(x, w, b, g, beta, layer, col0, ncols, tm, rows_s, name):
    m_rows, k = x.shape
    off = col0 // ncols
    vec = lambda: pl.BlockSpec((None, 1, ncols), lambda m: (layer, 0, 0))
    return pl.pallas_call(
        _proj_gelu_ln_kernel,
        grid=(m_rows // tm,),
        in_specs=[
            pl.BlockSpec((tm, k), lambda m: (m, 0)),
            pl.BlockSpec((None, k, ncols), lambda m: (layer, 0, off)),
            pl.BlockSpec((None, 1, ncols), lambda m: (layer, 0, off)),
            vec(), vec(),
        ],
        out_specs=[pl.BlockSpec((tm, ncols), lambda m: (m, 0)),
                   pl.BlockSpec((rows_s, ncols), lambda m: (m, 0))],
        out_shape=[jax.ShapeDtypeStruct((m_rows, ncols), BF16),
                   jax.ShapeDtypeStruct((m_rows // tm * rows_s, ncols), F32)],
        compiler_params=_cparams(("parallel",), 52),
        name=name,
    )(x, w, b, g, beta)


def _glu_proj(x, w, b, layer, col0, ncols, tm, tn, name):
    m_rows, k = x.shape
    off_v = col0 // tn
    off_g = (col0 + ncols) // tn
    return pl.pallas_call(
        _glu_kernel,
        grid=(m_rows // tm, ncols // tn),
        in_specs=[
            pl.BlockSpec((tm, k), lambda m, n: (m, 0)),
            pl.BlockSpec((None, k, tn), lambda m, n: (layer, 0, off_v + n)),
            pl.BlockSpec((None, k, tn), lambda m, n: (layer, 0, off_g + n)),
            pl.BlockSpec((None, 1, tn), lambda m, n: (layer, 0, off_v + n)),
            pl.BlockSpec((None, 1, tn), lambda m, n: (layer, 0, off_g + n)),
        ],
        out_specs=pl.BlockSpec((tm, tn), lambda m, n: (m, n)),
        out_shape=jax.ShapeDtypeStruct((m_rows, ncols), F32),
        compiler_params=_cparams(("parallel", "arbitrary"), 56),
        name=name,
    )(x, w, w, b, b)


def _trailing_sum(x, w):
    result, offset, part, span = None, 0, x, 1
    while True:
        if w & span:
            term = part if offset == 0 else pltpu.roll(part, offset, axis=0)
            result = term if result is None else result + term
            offset += span
        if span * 2 > w:
            return result
        part = part + pltpu.roll(part, span, axis=0)
        span *= 2


def _conv_rows(x, wdw_ref, bias, ls):
    lead = CONV_HALO - CONV_BUF
    acc = jnp.broadcast_to(bias, (ROW_BLOCK, LANES))
    for r in range(SUBLANES):
        taps = [k for k in range(CONV_WIDTH) if (lead + k) % SUBLANES == r]
        if not taps:
            continue
        xr = x if r == 0 else pltpu.roll(x, x.shape[0] - r, axis=0)
        for k in taps:
            q = (lead + k) // SUBLANES
            acc = acc + wdw_ref[k:k + 1, ls] * xr[SUBLANES * q:SUBLANES * q + ROW_BLOCK]
    return acc


def _mix_prompt_step(t, u_ref, v_ref, hb_ref, hbh_ref, c_ref, ch_ref,
                     ws_ref, bsb_ref, wgrp_ref, bsc_ref,
                     wdw_ref, bdw_ref, clg_ref, clb_ref,
                     pa_ref, pb_ref, pc_ref, hb_ext, c_ext, pool_scr, conv_scr):
    sub, d_mix = v_ref.shape
    n_chunks = sub // CHUNK
    head_dim = d_mix // A_HEADS
    group_dim = d_mix // len(POOL_WINDOWS)
    base = pl.multiple_of(t * sub, sub)

    vb = v_ref[...]
    row = lax.broadcasted_iota(jnp.int32, (CHUNK, CHUNK), 0)
    col = lax.broadcasted_iota(jnp.int32, (CHUNK, CHUNK), 1)
    for h in range(A_HEADS):
        hs = slice(h * head_dim, (h + 1) * head_dim)
        wm = jnp.where(row >= col, ws_ref[h], 0.0).astype(BF16)
        rhs = jnp.concatenate(
            [vb[ci * CHUNK:(ci + 1) * CHUNK, hs] for ci in range(n_chunks)], axis=1)
        s = jnp.dot(wm, rhs, preferred_element_type=F32)
        bias = bsb_ref[h]
        for ci in range(n_chunks):
            rs = slice(ci * CHUNK, (ci + 1) * CHUNK)
            s_c = s[:, ci * head_dim:(ci + 1) * head_dim] + bias
            pa_ref[pl.ds(base + ci * CHUNK, CHUNK), hs] = (u_ref[rs, hs] * s_c).astype(BF16)

    first = t == 0
    hb_ext[0:POOL_HALO, :] = jnp.where(first, 0.0, hbh_ref[...])
    hb_ext[POOL_HALO:POOL_HALO + sub, :] = hb_ref[...]

    def pool_body(i, carry):
        r0 = pl.multiple_of(i * ROW_BLOCK, ROW_BLOCK)
        pos = base + r0 + lax.broadcasted_iota(jnp.int32, (ROW_BLOCK, group_dim), 0)
        for g, w in enumerate(POOL_WINDOWS):
            gs = slice(g * group_dim, (g + 1) * group_dim)
            x = hb_ext[pl.ds(r0, POOL_HALO + ROW_BLOCK), gs]
            tok = x[POOL_HALO:POOL_HALO + ROW_BLOCK]
            win = _trailing_sum(x, w)[POOL_HALO:POOL_HALO + ROW_BLOCK]
            cnt = jnp.minimum(pos + 1, w).astype(F32)
            pool_scr[pl.ds(r0, ROW_BLOCK), gs] = win / cnt - tok
        return carry

    lax.fori_loop(0, sub // ROW_BLOCK, pool_body, 0)
    for g in range(len(POOL_WINDOWS)):
        gs = slice(g * group_dim, (g + 1) * group_dim)
        mixed = _bdot(pool_scr[:, gs].astype(BF16), wgrp_ref[g])
        pb_ref[pl.ds(base, sub), gs] = (mixed * bsc_ref[:, gs]).astype(BF16)

    c_ext[0:CONV_HALO, :] = jnp.where(first, 0.0, ch_ref[...])
    c_ext[CONV_HALO:CONV_HALO + sub, :] = c_ref[...]

    def conv_body(i, carry):
        r0 = pl.multiple_of(i * ROW_BLOCK, ROW_BLOCK)
        for lt in range(d_mix // LANES):
            ls = slice(lt * LANES, (lt + 1) * LANES)
            window = c_ext[pl.ds(r0, CONV_HALO + ROW_BLOCK), ls]
            conv_scr[pl.ds(r0, ROW_BLOCK), ls] = _conv_rows(window, wdw_ref, bdw_ref[:, ls], ls)
        return carry

    lax.fori_loop(0, sub // ROW_BLOCK, conv_body, 0)
    y = _layer_norm(conv_scr[...], clg_ref[...], clb_ref[...])
    pc_ref[pl.ds(base, sub), :] = (y * _sigmoid(y)).astype(BF16)


def _mix_sample_step(row0, u_ref, v_ref, hb_ref, c_ref, sp_ref, sc_ref,
                     ws0_ref, bs0_ref, wgrp_ref, bsc_ref,
                     wdw_ref, bdw_ref, clg_ref, clb_ref,
                     pa_ref, pb_ref, pc_ref):
    rows, d_mix = v_ref.shape
    group_dim = d_mix // len(POOL_WINDOWS)
    out_rows = slice(row0, row0 + rows)

    s = v_ref[...].astype(F32) * ws0_ref[...].astype(BF16).astype(F32) + bs0_ref[...]
    pa_ref[out_rows, :] = (u_ref[...] * s).astype(BF16)

    for g, w in enumerate(POOL_WINDOWS):
        gs = slice(g * group_dim, (g + 1) * group_dim)
        tok = hb_ref[:, gs]
        win = tok
        for k in range(POOL_BUF - (w - 1), POOL_BUF):
            win = win + sp_ref[k, :, gs]
        pooled = win / float(w) - tok
        mixed = _bdot(pooled.astype(BF16), wgrp_ref[g])
        pb_ref[out_rows, gs] = (mixed * bsc_ref[:, gs]).astype(BF16)

    conv = c_ref[...] * wdw_ref[CONV_BUF:CONV_WIDTH, :] + bdw_ref[...]
    for k in range(CONV_BUF):
        conv = conv + sc_ref[k] * wdw_ref[k:k + 1, :]
    y = _layer_norm(conv, clg_ref[...], clb_ref[...])
    pc_ref[out_rows, :] = (y * _sigmoid(y)).astype(BF16)


def _mix_kernel(u_ref, v_ref, hb_ref, hbh_ref, c_ref, ch_ref,
                us_ref, vs_ref, hbs_ref, cs_ref, sp_ref, sc_ref,
                ws_ref, bsb_ref, ws0_ref, bs0_ref, wgrp_ref, bsc_ref,
                wdw_ref, bdw_ref, clg_ref, clb_ref,
                pa_ref, pb_ref, pc_ref,
                hb_ext, c_ext, pool_scr, conv_scr, *, n_sub):
    t = pl.program_id(1)

    @pl.when(t < n_sub)
    def _():
        _mix_prompt_step(t, u_ref, v_ref, hb_ref, hbh_ref, c_ref, ch_ref,
                         ws_ref, bsb_ref, wgrp_ref, bsc_ref,
                         wdw_ref, bdw_ref, clg_ref, clb_ref,
                         pa_ref, pb_ref, pc_ref, hb_ext, c_ext, pool_scr, conv_scr)

    @pl.when(t == n_sub)
    def _():
        _mix_sample_step(n_sub * v_ref.shape[0], us_ref, vs_ref, hbs_ref, cs_ref, sp_ref, sc_ref,
                         ws0_ref, bs0_ref, wgrp_ref, bsc_ref,
                         wdw_ref, bdw_ref, clg_ref, clb_ref,
                         pa_ref, pb_ref, pc_ref)


def _mix(u3, v3, hb3, c3, state_pool, state_conv, lw, layer, seq):
    batch, tile_rows, d_mix = hb3.shape
    rows_s = tile_rows - seq
    n_sub = seq // MIX_SUB
    sub_idx = lambda t: jnp.minimum(t, n_sub - 1)
    vec = lambda: pl.BlockSpec((None, 1, d_mix), lambda b, t: (layer, 0, 0))
    sub_spec = lambda: pl.BlockSpec((None, MIX_SUB, d_mix), lambda b, t: (b, sub_idx(t), 0))
    smp_spec = lambda: pl.BlockSpec((None, rows_s, d_mix), lambda b, t: (b, seq // rows_s, 0))

    def halo_spec(rows):
        per = MIX_SUB // rows
        return pl.BlockSpec((None, rows, d_mix),
                            lambda b, t: (b, jnp.maximum(sub_idx(t) * per - 1, 0), 0))

    def state_spec(buf):
        return pl.BlockSpec((None, buf, rows_s, d_mix), lambda b, t: (layer, 0, b, 0),
                            pipeline_mode=_ONCE)

    head_spec = lambda: pl.BlockSpec((None, A_HEADS, CHUNK, CHUNK), lambda b, t: (layer, 0, 0, 0))
    tile_out = pl.BlockSpec((None, tile_rows, d_mix), lambda b, t: (b, 0, 0))
    out3 = jax.ShapeDtypeStruct((batch, tile_rows, d_mix), BF16)
    return pl.pallas_call(
        functools.partial(_mix_kernel, n_sub=n_sub),
        grid=(batch, n_sub + 1),
        in_specs=[
            sub_spec(), sub_spec(),
            sub_spec(), halo_spec(POOL_HALO),
            sub_spec(), halo_spec(CONV_HALO),
            smp_spec(), smp_spec(), smp_spec(), smp_spec(),
            state_spec(POOL_BUF), state_spec(CONV_BUF),
            head_spec(), head_spec(),
            vec(), vec(),
            pl.BlockSpec((None,) + lw["b_w_group"].shape[1:], lambda b, t: (layer, 0, 0, 0)),
            vec(),
            pl.BlockSpec((None, CONV_WIDTH, d_mix), lambda b, t: (layer, 0, 0)),
            vec(), vec(), vec(),
        ],
        out_specs=[tile_out, tile_out, tile_out],
        out_shape=[out3, out3, out3],
        scratch_shapes=[
            pltpu.VMEM((POOL_HALO + MIX_SUB, d_mix), F32),
            pltpu.VMEM((CONV_HALO + MIX_SUB, d_mix), F32),
            pltpu.VMEM((MIX_SUB, d_mix), F32),
            pltpu.VMEM((MIX_SUB, d_mix), F32),
        ],
        compiler_params=_cparams(("parallel", "arbitrary"), 56),
        name="mix",
    )(u3, v3, hb3, hb3, c3, c3, u3, v3, hb3, c3, state_pool, state_conv,
      lw["a_ws"], lw["a_bs_b"], lw["a_ws0"], lw["a_bs0"],
      lw["b_w_group"], lw["b_scale"], lw["c_w_dw"], lw["c_b_dw"], lw["c_ln_g"], lw["c_ln_b"])


def _merge_kernel(pa_ref, pb_ref, pc_ref, wa_ref, wb_ref, wc_ref,
                  g0_ref, g1_ref, g2_ref, o_ref):
    merged = (g0_ref[...] * _bdot(pa_ref[...], wa_ref[...])
              + g1_ref[...] * _bdot(pb_ref[...], wb_ref[...])
              + g2_ref[...] * _bdot(pc_ref[...], wc_ref[...]))
    o_ref[...] = merged.astype(o_ref.dtype)


def _merge(pa, pb, pc, gates, w_a, w_b, w_c, layer, tm, tn):
    m_rows, d_mix = pa.shape
    d_model = w_a.shape[-1]
    nb = d_model // tn
    act = lambda: pl.BlockSpec((tm, d_mix), lambda m, n: (m, 0))
    wsp = lambda: pl.BlockSpec((None, d_mix, tn), lambda m, n: (layer, 0, n))
    gsp = lambda j: pl.BlockSpec((tm, tn), lambda m, n: (m, j * nb + n))
    return pl.pallas_call(
        _merge_kernel,
        grid=(m_rows // tm, nb),
        in_specs=[act(), act(), act(), wsp(), wsp(), wsp(), gsp(0), gsp(1), gsp(2)],
        out_specs=pl.BlockSpec((tm, tn), lambda m, n: (m, n)),
        out_shape=jax.ShapeDtypeStruct((m_rows, d_model), BF16),
        compiler_params=_cparams(("parallel", "arbitrary"), 52),
        name="merge",
    )(pa, pb, pc, w_a, w_b, w_c, gates, gates, gates)


def _projout_accumulate(m_ref, w_ref, o_ref):
    @pl.when(pl.program_id(2) == 0)
    def _():
        o_ref[...] = jnp.zeros_like(o_ref)

    o_ref[...] += _bdot(m_ref[...], w_ref[...])


def _projout_ln_kernel(m_ref, w_ref, x_ref, g_ref, b_ref, o_ref, *, alpha):
    _projout_accumulate(m_ref, w_ref, o_ref)

    @pl.when(pl.program_id(2) == pl.num_programs(2) - 1)
    def _():
        _residual_ln_rows(x_ref, o_ref, g_ref, b_ref, None, alpha, o_ref.shape[0] // LN_ROWS)


def _projout_ln_inputs_kernel(m_ref, w_ref, xp_ref, xs_ref, g_ref, b_ref, o_ref, *, alpha):
    _projout_accumulate(m_ref, w_ref, o_ref)
    n_chunks = o_ref.shape[0] // LN_ROWS
    rows_s = xs_ref.shape[0]
    last = (n_chunks - 1) * LN_ROWS
    is_last_k = pl.program_id(2) == pl.num_programs(2) - 1
    second_half = pl.program_id(1) == 1

    @pl.when(is_last_k)
    def _():
        _residual_ln_rows(xp_ref, o_ref, g_ref, b_ref, None, alpha, n_chunks - 1)

    @pl.when(jnp.logical_and(is_last_k, jnp.logical_not(second_half)))
    def _():
        _residual_ln_chunk(xp_ref[last:last + LN_ROWS, :], o_ref, g_ref, b_ref, None, alpha, last)

    @pl.when(jnp.logical_and(is_last_k, second_half))
    def _():
        x_rows = jnp.concatenate([xp_ref[last:last + LN_ROWS - rows_s, :], xs_ref[...]], axis=0)
        _residual_ln_chunk(x_rows, o_ref, g_ref, b_ref, None, alpha, last)


def _projout_ln(merged, w_out, x_sources, g, b, layer, alpha, tm, tk):
    m_rows = merged.shape[0]
    d = w_out.shape[-1]
    batch = m_rows // (2 * tm)
    vec = lambda: pl.BlockSpec((None, 1, d), lambda bb, h, k: (layer, 0, 0))
    row_idx = lambda bb, h: bb * 2 + h
    if len(x_sources) == 1:
        kern = _projout_ln_kernel
        x_specs = [pl.BlockSpec((tm, d), lambda bb, h, k: (row_idx(bb, h), 0), pipeline_mode=_ONCE)]
    else:
        kern = _projout_ln_inputs_kernel
        rows_s = x_sources[1].shape[1]
        x_specs = [pl.BlockSpec((None, tm, d), lambda bb, h, k: (bb, h, 0), pipeline_mode=_ONCE),
                   pl.BlockSpec((None, rows_s, d), lambda bb, h, k: (bb, 0, 0))]
    return pl.pallas_call(
        functools.partial(kern, alpha=alpha),
        grid=(batch, 2, merged.shape[1] // tk),
        in_specs=[
            pl.BlockSpec((tm, tk), lambda bb, h, k: (row_idx(bb, h), k)),
            pl.BlockSpec((None, tk, d), lambda bb, h, k: (layer, k, 0)),
            *x_specs,
            vec(), vec(),
        ],
        out_specs=pl.BlockSpec((tm, d), lambda bb, h, k: (row_idx(bb, h), 0)),
        out_shape=jax.ShapeDtypeStruct((m_rows, d), F32),
        compiler_params=_cparams(("parallel", "parallel", "arbitrary"), 48),
        name="projout_ln",
    )(merged, w_out, *x_sources, g, b)


def _ffn_ln_kernel(x_ref, wg_ref, wu_ref, wd_ref, g_ref, b_ref, of_ref, *maybe_bf, alpha):
    f = pl.program_id(1)

    @pl.when(f == 0)
    def _():
        of_ref[...] = jnp.zeros_like(of_ref)

    x = x_ref[...].astype(BF16)
    gate = _bdot(x, wg_ref[...])
    up = _bdot(x, wu_ref[...])
    hid = (gate * _sigmoid(gate) * up).astype(BF16)
    of_ref[...] += _bdot(hid, wd_ref[...])

    @pl.when(f == pl.num_programs(1) - 1)
    def _():
        _residual_ln_rows(x_ref, of_ref, g_ref, b_ref, maybe_bf[0] if maybe_bf else None, alpha,
                          of_ref.shape[0] // LN_ROWS)


def _ffn_ln(x, w_up, w_down, g, b, layer, alpha, tm, tf, emit_bf16):
    m_rows, d = x.shape
    d_ff = w_down.shape[1]
    nf = d_ff // tf
    row = lambda: pl.BlockSpec((tm, d), lambda m, f: (m, 0), pipeline_mode=_ONCE)
    vec = lambda: pl.BlockSpec((None, 1, d), lambda m, f: (layer, 0, 0))
    out_specs = [row()]
    out_shape = [jax.ShapeDtypeStruct((m_rows, d), F32)]
    if emit_bf16:
        out_specs.append(row())
        out_shape.append(jax.ShapeDtypeStruct((m_rows, d), BF16))
    return pl.pallas_call(
        functools.partial(_ffn_ln_kernel, alpha=alpha),
        grid=(m_rows // tm, nf),
        in_specs=[
            row(),
            pl.BlockSpec((None, d, tf), lambda m, f: (layer, 0, f)),
            pl.BlockSpec((None, d, tf), lambda m, f: (layer, 0, nf + f)),
            pl.BlockSpec((None, tf, d), lambda m, f: (layer, f, 0)),
            vec(), vec(),
        ],
        out_specs=out_specs,
        out_shape=out_shape,
        compiler_params=_cparams(("parallel", "arbitrary"), 60),
        name="ffn_ln",
    )(x, w_up, w_up, w_down, g, b)


def _state_shift_kernel(s_ref, *refs):
    new_refs, o_ref = refs[:-1], refs[-1]
    layer = pl.program_id(0)
    keep = s_ref.shape[0] - 1
    new = new_refs[0][...]
    for j in range(1, len(new_refs)):
        new = jnp.where(layer == j, new_refs[j][...], new)
    o_ref[0:keep] = s_ref[1:keep + 1]
    o_ref[keep] = new


def _state_shift(state_t, new_rows3, seq):
    depth, buf, _, d_mix = state_t.shape
    batch, tile_rows, _ = new_rows3[0].shape
    rows_s = tile_rows - seq
    blk = pl.BlockSpec((None, buf, rows_s, d_mix), lambda l, b: (l, 0, b, 0))
    new_spec = pl.BlockSpec((None, rows_s, d_mix), lambda l, b: (b, seq // rows_s, 0))
    return pl.pallas_call(
        _state_shift_kernel,
        grid=(depth, batch),
        in_specs=[blk] + [new_spec] * depth,
        out_specs=blk,
        out_shape=jax.ShapeDtypeStruct(state_t.shape, state_t.dtype),
        compiler_params=_cparams(("parallel", "parallel"), 32),
        name="state_shift",
    )(state_t, *new_rows3)


def kernel(x_prompt, x_sample, state_pool, state_conv, w_in, b_in, a_ln_g, a_ln_b, a_ws, a_bs, w_a_out, b_w_group, b_scale, w_b_out, c_w_dw, c_b_dw, c_ln_g, c_ln_b, w_c_out, w_out, ln1_g, ln1_b, w_ffn_up, w_ffn_down, ln2_g, ln2_b):
    batch, seq, d_model = x_prompt.shape
    m_sample = x_sample.shape[0] * x_sample.shape[1]
    depth = w_in.shape[0]
    d_mix = a_ln_g.shape[-1]
    head_dim = d_mix // A_HEADS
    alpha = (2.0 * depth) ** 0.25

    assert x_sample.shape[1] == 1 and m_sample % batch == 0
    rows_s = m_sample // batch
    tile_rows = seq + rows_s
    m_rows = batch * tile_rows
    half_rows = tile_rows // 2
    assert rows_s % BF16_ROWS == 0 and seq % rows_s == 0 and seq % MIX_SUB == 0
    assert seq % PACK_SUB == 0
    assert tile_rows % 2 == 0 and half_rows % LN_ROWS == 0 and LN_ROWS % BF16_ROWS == 0
    assert rows_s <= LN_ROWS and (LN_ROWS - rows_s) % SUBLANES == 0
    assert tile_rows % (ROW_SPLITS * BF16_ROWS) == 0 and rows_s <= tile_rows // ROW_SPLITS

    vec3 = lambda a: a.reshape(depth, 1, a.shape[-1])
    lw = {
        "a_ln_g": vec3(a_ln_g), "a_ln_b": vec3(a_ln_b),
        "a_ws": a_ws,
        "a_bs_b": jnp.broadcast_to(a_bs[..., None], a_bs.shape + (head_dim,)),
        "a_ws0": jnp.repeat(a_ws[:, :, 0, 0], head_dim, axis=-1).reshape(depth, 1, d_mix),
        "a_bs0": jnp.repeat(a_bs[:, :, 0], head_dim, axis=-1).reshape(depth, 1, d_mix),
        "b_w_group": b_w_group, "b_scale": vec3(b_scale),
        "c_w_dw": c_w_dw, "c_b_dw": vec3(c_b_dw),
        "c_ln_g": vec3(c_ln_g), "c_ln_b": vec3(c_ln_b),
    }
    b_in3 = vec3(b_in)
    ln1_g3, ln1_b3, ln2_g3, ln2_b3 = vec3(ln1_g), vec3(ln1_b), vec3(ln2_g), vec3(ln2_b)

    s1 = 2 * d_mix
    s2 = s1 + d_mix
    s3 = s2 + 2 * d_mix

    x_sample3 = x_sample.reshape(batch, rows_s, d_model)
    x_sources = (x_prompt, x_sample3)
    xb = _pack_bf16(x_prompt, x_sample3).reshape(m_rows, d_model)
    tile3 = lambda a: a.reshape(batch, tile_rows, a.shape[-1])
    pool_t = jnp.transpose(state_pool, (0, 2, 1, 3))
    conv_t = jnp.transpose(state_conv, (0, 2, 1, 3))

    hb_l, c_l, v_l = [], [], []
    for l in range(depth):
        u = _proj(xb, w_in, b_in3, l, 0, d_mix, "gelu", BF16, tile_rows, d_mix, "proj_u")
        v, v_new = _proj_gelu_ln(xb, w_in, b_in3, lw["a_ln_g"], lw["a_ln_b"], l, d_mix, d_mix,
                                 tile_rows, rows_s, "proj_v")
        hb = _proj(xb, w_in, b_in3, l, s1, d_mix, "none", F32, tile_rows, d_mix, "proj_b")
        c = _glu_proj(xb, w_in, b_in3, l, s2, d_mix, tile_rows, 512, "proj_c")
        gates = _proj(xb, w_in, b_in3, l, s3, N_BRANCH * d_model, "sigmoid", BF16,
                      tile_rows, 1024, "proj_gates")

        pa, pb, pc = _mix(tile3(u), tile3(v), tile3(hb), tile3(c), pool_t, conv_t, lw, l, seq)
        flat = lambda a: a.reshape(m_rows, d_mix)
        merged = _merge(flat(pa), flat(pb), flat(pc), gates, w_a_out, w_b_out, w_c_out,
                        l, tile_rows, 256)
        x1 = _projout_ln(merged, w_out, x_sources, ln1_g3, ln1_b3, l, alpha, half_rows, 512)
        outs = _ffn_ln(x1, w_ffn_up, w_ffn_down, ln2_g3, ln2_b3, l, alpha, tile_rows, 256,
                       emit_bf16=l + 1 < depth)
        x_sources = (outs[0],)
        xb = outs[1] if l + 1 < depth else None

        hb_l.append(tile3(hb))
        c_l.append(tile3(c))
        v_l.append(v_new)

    x3 = tile3(x_sources[0])
    y_prompt = x3[:, :seq]
    y_sample = x3[:, seq:].reshape(m_sample, 1, d_model)
    new_pool_prompt = jnp.stack([a[:, seq - POOL_BUF:seq] for a in hb_l])
    new_conv_prompt = jnp.stack([a[:, seq - CONV_BUF:seq] for a in c_l])
    new_pool_sample = jnp.transpose(_state_shift(pool_t, hb_l, seq), (0, 2, 1, 3))
    new_conv_sample = jnp.transpose(_state_shift(conv_t, c_l, seq), (0, 2, 1, 3))
    new_chunk_v = jnp.stack(v_l)[:, :, None, :]
    return (y_prompt, y_sample, new_pool_prompt, new_conv_prompt,
            new_pool_sample, new_conv_sample, new_chunk_v)
```

```python
import functools

import jax
import jax.numpy as jnp
from jax import lax
from jax.experimental import pallas as pl
from jax.experimental.pallas import tpu as pltpu

F32 = jnp.float32
BF16 = jnp.bfloat16

LN_EPS = 1e-5
A_HEADS = 8
CHUNK = 128
POOL_WINDOWS = (2, 4, 8, 16)
POOL_BUF = 15
CONV_WIDTH = 31
CONV_BUF = CONV_WIDTH - 1
N_BRANCH = 3

SUBLANES = 8
LANES = 128
BF16_ROWS = 16
POOL_HALO = 16
CONV_HALO = 32
ROW_BLOCK = 64
MIX_SUB = 256
PACK_SUB = 512
LN_ROWS = 208
ROW_SPLITS = 5
V7X_VMEM_BYTES = 64 * 1024 * 1024

_ONCE = pl.Buffered(1)


def _cparams(semantics, vmem_mb):
    assert vmem_mb * 1024 * 1024 < V7X_VMEM_BYTES
    return pltpu.CompilerParams(dimension_semantics=semantics,
                                vmem_limit_bytes=vmem_mb * 1024 * 1024)


def _layer_norm(x, g, b):
    mu = jnp.mean(x, axis=-1, keepdims=True)
    xc = x - mu
    var = jnp.mean(xc * xc, axis=-1, keepdims=True)
    return xc * lax.rsqrt(var + LN_EPS) * g + b


def _gelu_exact(x):
    return 0.5 * x * (1.0 + lax.erf(x * (0.5 ** 0.5)))


def _sigmoid(x):
    return 1.0 / (1.0 + jnp.exp(-x))


def _bdot(a, w):
    return jnp.dot(a, w.astype(BF16), preferred_element_type=F32)


def _residual_ln_chunk(x_rows, acc_ref, g_ref, b_ref, bf_ref, alpha, r):
    y = _layer_norm(alpha * x_rows + acc_ref[pl.ds(r, LN_ROWS), :], g_ref[...], b_ref[...])
    acc_ref[pl.ds(r, LN_ROWS), :] = y
    if bf_ref is not None:
        bf_ref[pl.ds(r, LN_ROWS), :] = y.astype(BF16)


def _residual_ln_rows(x_ref, acc_ref, g_ref, b_ref, bf_ref, alpha, n_chunks):
    def body(i, carry):
        r = pl.multiple_of(i * LN_ROWS, LN_ROWS)
        _residual_ln_chunk(x_ref[pl.ds(r, LN_ROWS), :], acc_ref, g_ref, b_ref, bf_ref, alpha, r)
        return carry

    lax.fori_loop(0, n_chunks, body, 0)


def _pack_kernel(xp_ref, xs_ref, o_ref, *, n_sub):
    t = pl.program_id(1)
    sub = xp_ref.shape[0]

    @pl.when(t < n_sub)
    def _():
        o_ref[pl.ds(pl.multiple_of(t * sub, sub), sub), :] = xp_ref[...].astype(BF16)

    @pl.when(t == n_sub)
    def _():
        o_ref[n_sub * sub:, :] = xs_ref[...].astype(BF16)


def _pack_bf16(x_prompt, x_sample3):
    batch, seq, d = x_prompt.shape
    rows_s = x_sample3.shape[1]
    n_sub = seq // PACK_SUB
    return pl.pallas_call(
        functools.partial(_pack_kernel, n_sub=n_sub),
        grid=(batch, n_sub + 1),
        in_specs=[
            pl.BlockSpec((None, PACK_SUB, d), lambda b, t: (b, jnp.minimum(t, n_sub - 1), 0)),
            pl.BlockSpec((None, rows_s, d), lambda b, t: (b, 0, 0)),
        ],
        out_specs=pl.BlockSpec((None, seq + rows_s, d), lambda b, t: (b, 0, 0)),
        out_shape=jax.ShapeDtypeStruct((batch, seq + rows_s, d), BF16),
        compiler_params=_cparams(("parallel", "arbitrary"), 40),
        name="pack",
    )(x_prompt, x_sample3)


def _proj_kernel(x_ref, w_ref, b_ref, o_ref, *, act):
    w = w_ref[...].astype(BF16)
    rows = x_ref.shape[0] // ROW_SPLITS
    for i in range(ROW_SPLITS):
        rs = slice(i * rows, (i + 1) * rows)
        h = jnp.dot(x_ref[rs, :], w, preferred_element_type=F32) + b_ref[...]
        if act == "gelu":
            h = _gelu_exact(h)
        elif act == "sigmoid":
            h = _sigmoid(h)
        o_ref[rs, :] = h.astype(o_ref.dtype)


def _proj_sigmoid_blocked_kernel(x_ref, w_ref, b_ref, o_ref):
    w = w_ref[...].astype(BF16)
    n_blk, tile_rows, sub = o_ref.shape
    rows = tile_rows // ROW_SPLITS
    for i in range(ROW_SPLITS):
        rs = slice(i * rows, (i + 1) * rows)
        h = _sigmoid(jnp.dot(x_ref[rs, :], w, preferred_element_type=F32) + b_ref[...])
        for j in range(n_blk):
            o_ref[j, rs, :] = h[:, j * sub:(j + 1) * sub].astype(o_ref.dtype)


def _proj_gelu_ln_kernel(x_ref, w_ref, b_ref, g_ref, beta_ref, o_ref, os_ref):
    w = w_ref[...].astype(BF16)
    rows = x_ref.shape[0] // ROW_SPLITS
    n_s = os_ref.shape[0]
    for i in range(ROW_SPLITS):
        rs = slice(i * rows, (i + 1) * rows)
        h = jnp.dot(x_ref[rs, :], w, preferred_element_type=F32) + b_ref[...]
        v = _layer_norm(_gelu_exact(h), g_ref[...], beta_ref[...])
        o_ref[rs, :] = v.astype(BF16)
        if i == ROW_SPLITS - 1:
            os_ref[...] = v[rows - n_s:, :]


def _glu_kernel(x_ref, wv_ref, wg_ref, bv_ref, bg_ref, o_ref):
    wv = wv_ref[...].astype(BF16)
    wg = wg_ref[...].astype(BF16)
    rows = x_ref.shape[0] // ROW_SPLITS
    for i in range(ROW_SPLITS):
        rs = slice(i * rows, (i + 1) * rows)
        x = x_ref[rs, :]
        val = jnp.dot(x, wv, preferred_element_type=F32) + bv_ref[...]
        gate = jnp.dot(x, wg, preferred_element_type=F32) + bg_ref[...]
        o_ref[rs, :] = val * _sigmoid(gate)


def _proj(x, w, b, layer, col0, ncols, act, out_dtype, tm, tn, name):
    m_rows, k = x.shape
    off = col0 // tn
    return pl.pallas_call(
        functools.partial(_proj_kernel, act=act),
        grid=(m_rows // tm, ncols // tn),
        in_specs=[
            pl.BlockSpec((tm, k), lambda m, n: (m, 0)),
            pl.BlockSpec((None, k, tn), lambda m, n: (layer, 0, off + n)),
            pl.BlockSpec((None, 1, tn), lambda m, n: (layer, 0, off + n)),
        ],
        out_specs=pl.BlockSpec((tm, tn), lambda m, n: (m, n)),
        out_shape=jax.ShapeDtypeStruct((m_rows, ncols), out_dtype),
        compiler_params=_cparams(("parallel", "arbitrary"), 56),
        name=name,
    )(x, w, b)


def _proj_sigmoid_blocked(x, w, b, layer, col0, ncols, tm, tn, sub, name):
    m_rows, k = x.shape
    off = col0 // tn
    return pl.pallas_call(
        _proj_sigmoid_blocked_kernel,
        grid=(m_rows // tm, ncols // tn),
        in_specs=[
            pl.BlockSpec((tm, k), lambda m, n: (m, 0)),
            pl.BlockSpec((None, k, tn), lambda m, n: (layer, 0, off + n)),
            pl.BlockSpec((None, 1, tn), lambda m, n: (layer, 0, off + n)),
        ],
        out_specs=pl.BlockSpec((tn // sub, tm, sub), lambda m, n: (n, m, 0)),
        out_shape=jax.ShapeDtypeStruct((ncols // sub, m_rows, sub), BF16),
        compiler_params=_cparams(("parallel", "arbitrary"), 56),
        name=name,
    )(x, w, b)


def _proj_gelu_ln(x, w, b, g, beta, layer, col0, ncols, tm, rows_s, name):
    m_rows, k = x.shape
    off = col0 // ncols
    vec = lambda: pl.BlockSpec((None, 1, ncols), lambda m: (layer, 0, 0))
    return pl.pallas_call(
        _proj_gelu_ln_kernel,
        grid=(m_rows // tm,),
        in_specs=[
            pl.BlockSpec((tm, k), lambda m: (m, 0)),
            pl.BlockSpec((None, k, ncols), lambda m: (layer, 0, off)),
            pl.BlockSpec((None, 1, ncols), lambda m: (layer, 0, off)),
            vec(), vec(),
        ],
        out_specs=[pl.BlockSpec((tm, ncols), lambda m: (m, 0)),
                   pl.BlockSpec((rows_s, ncols), lambda m: (m, 0))],
        out_shape=[jax.ShapeDtypeStruct((m_rows, ncols), BF16),
                   jax.ShapeDtypeStruct((m_rows // tm * rows_s, ncols), F32)],
        compiler_params=_cparams(("parallel",), 52),
        name=name,
    )(x, w, b, g, beta)


def _glu_proj(x, w, b, layer, col0, ncols, tm, tn, name):
    m_rows, k = x.shape
    off_v = col0 // tn
    off_g = (col0 + ncols) // tn
    return pl.pallas_call(
        _glu_kernel,
        grid=(m_rows // tm, ncols // tn),
        in_specs=[
            pl.BlockSpec((tm, k), lambda m, n: (m, 0)),
            pl.BlockSpec((None, k, tn), lambda m, n: (layer, 0, off_v + n)),
            pl.BlockSpec((None, k, tn), lambda m, n: (layer, 0, off_g + n)),
            pl.BlockSpec((None, 1, tn), lambda m, n: (layer, 0, off_v + n)),
            pl.BlockSpec((None, 1, tn), lambda m, n: (layer, 0, off_g + n)),
        ],
        out_specs=pl.BlockSpec((tm, tn), lambda m, n: (m, n)),
        out_shape=jax.ShapeDtypeStruct((m_rows, ncols), F32),
        compiler_params=_cparams(("parallel", "arbitrary"), 56),
        name=name,
    )(x, w, w, b, b)


def _trailing_sum(x, w):
    result, offset, part, span = None, 0, x, 1
    while True:
        if w & span:
            term = part if offset == 0 else pltpu.roll(part, offset, axis=0)
            result = term if result is None else result + term
            offset += span
        if span * 2 > w:
            return result
        part = part + pltpu.roll(part, span, axis=0)
        span *= 2


def _conv_rows(x, wdw_ref, bias, ls):
    lead = CONV_HALO - CONV_BUF
    acc = jnp.broadcast_to(bias, (ROW_BLOCK, LANES))
    for r in range(SUBLANES):
        taps = [k for k in range(CONV_WIDTH) if (lead + k) % SUBLANES == r]
        if not taps:
            continue
        xr = x if r == 0 else pltpu.roll(x, x.shape[0] - r, axis=0)
        for k in taps:
            q = (lead + k) // SUBLANES
            acc = acc + wdw_ref[k:k + 1, ls] * xr[SUBLANES * q:SUBLANES * q + ROW_BLOCK]
    return acc


def _mix_prompt_step(t, u_ref, v_ref, hb_ref, hbh_ref, c_ref, ch_ref,
                     ws_ref, bsb_ref, wgrp_ref, bsc_ref,
                     wdw_ref, bdw_ref, clg_ref, clb_ref,
                     pa_ref, pb_ref, pc_ref, hb_ext, c_ext, pool_scr, conv_scr):
    sub, d_mix = v_ref.shape
    n_chunks = sub // CHUNK
    head_dim = d_mix // A_HEADS
    group_dim = d_mix // len(POOL_WINDOWS)
    base = pl.multiple_of(t * sub, sub)

    vb = v_ref[...]
    row = lax.broadcasted_iota(jnp.int32, (CHUNK, CHUNK), 0)
    col = lax.broadcasted_iota(jnp.int32, (CHUNK, CHUNK), 1)
    for h in range(A_HEADS):
        hs = slice(h * head_dim, (h + 1) * head_dim)
        wm = jnp.where(row >= col, ws_ref[h], 0.0).astype(BF16)
        rhs = jnp.concatenate(
            [vb[ci * CHUNK:(ci + 1) * CHUNK, hs] for ci in range(n_chunks)], axis=1)
        s = jnp.dot(wm, rhs, preferred_element_type=F32)
        bias = bsb_ref[h]
        for ci in range(n_chunks):
            rs = slice(ci * CHUNK, (ci + 1) * CHUNK)
            s_c = s[:, ci * head_dim:(ci + 1) * head_dim] + bias
            pa_ref[pl.ds(base + ci * CHUNK, CHUNK), hs] = (u_ref[rs, hs] * s_c).astype(BF16)

    first = t == 0
    hb_ext[0:POOL_HALO, :] = jnp.where(first, 0.0, hbh_ref[...])
    hb_ext[POOL_HALO:POOL_HALO + sub, :] = hb_ref[...]

    def pool_body(i, carry):
        r0 = pl.multiple_of(i * ROW_BLOCK, ROW_BLOCK)
        pos = base + r0 + lax.broadcasted_iota(jnp.int32, (ROW_BLOCK, group_dim), 0)
        for g, w in enumerate(POOL_WINDOWS):
            gs = slice(g * group_dim, (g + 1) * group_dim)
            x = hb_ext[pl.ds(r0, POOL_HALO + ROW_BLOCK), gs]
            tok = x[POOL_HALO:POOL_HALO + ROW_BLOCK]
            win = _trailing_sum(x, w)[POOL_HALO:POOL_HALO + ROW_BLOCK]
            cnt = jnp.minimum(pos + 1, w).astype(F32)
            pool_scr[pl.ds(r0, ROW_BLOCK), gs] = win / cnt - tok
        return carry

    lax.fori_loop(0, sub // ROW_BLOCK, pool_body, 0)
    for g in range(len(POOL_WINDOWS)):
        gs = slice(g * group_dim, (g + 1) * group_dim)
        mixed = _bdot(pool_scr[:, gs].astype(BF16), wgrp_ref[g])
        pb_ref[pl.ds(base, sub), gs] = (mixed * bsc_ref[:, gs]).astype(BF16)

    c_ext[0:CONV_HALO, :] = jnp.where(first, 0.0, ch_ref[...])
    c_ext[CONV_HALO:CONV_HALO + sub, :] = c_ref[...]

    def conv_body(i, carry):
        r0 = pl.multiple_of(i * ROW_BLOCK, ROW_BLOCK)
        for lt in range(d_mix // LANES):
            ls = slice(lt * LANES, (lt + 1) * LANES)
            window = c_ext[pl.ds(r0, CONV_HALO + ROW_BLOCK), ls]
            conv_scr[pl.ds(r0, ROW_BLOCK), ls] = _conv_rows(window, wdw_ref, bdw_ref[:, ls], ls)
        return carry

    lax.fori_loop(0, sub // ROW_BLOCK, conv_body, 0)
    y = _layer_norm(conv_scr[...], clg_ref[...], clb_ref[...])
    pc_ref[pl.ds(base, sub), :] = (y * _sigmoid(y)).astype(BF16)


def _mix_sample_step(row0, u_ref, v_ref, hb_ref, c_ref, sp_ref, sc_ref,
                     ws0_ref, bs0_ref, wgrp_ref, bsc_ref,
                     wdw_ref, bdw_ref, clg_ref, clb_ref,
                     pa_ref, pb_ref, pc_ref):
    rows, d_mix = v_ref.shape
    group_dim = d_mix // len(POOL_WINDOWS)
    out_rows = slice(row0, row0 + rows)

    s = v_ref[...].astype(F32) * ws0_ref[...].astype(BF16).astype(F32) + bs0_ref[...]
    pa_ref[out_rows, :] = (u_ref[...] * s).astype(BF16)

    for g, w in enumerate(POOL_WINDOWS):
        gs = slice(g * group_dim, (g + 1) * group_dim)
        tok = hb_ref[:, gs]
        win = tok
        for k in range(POOL_BUF - (w - 1), POOL_BUF):
            win = win + sp_ref[k, :, gs]
        pooled = win / float(w) - tok
        mixed = _bdot(pooled.astype(BF16), wgrp_ref[g])
        pb_ref[out_rows, gs] = (mixed * bsc_ref[:, gs]).astype(BF16)

    conv = c_ref[...] * wdw_ref[CONV_BUF:CONV_WIDTH, :] + bdw_ref[...]
    for k in range(CONV_BUF):
        conv = conv + sc_ref[k] * wdw_ref[k:k + 1, :]
    y = _layer_norm(conv, clg_ref[...], clb_ref[...])
    pc_ref[out_rows, :] = (y * _sigmoid(y)).astype(BF16)


def _mix_kernel(u_ref, v_ref, hb_ref, hbh_ref, c_ref, ch_ref,
                us_ref, vs_ref, hbs_ref, cs_ref, sp_ref, sc_ref,
                ws_ref, bsb_ref, ws0_ref, bs0_ref, wgrp_ref, bsc_ref,
                wdw_ref, bdw_ref, clg_ref, clb_ref,
                pa_ref, pb_ref, pc_ref,
                hb_ext, c_ext, pool_scr, conv_scr, *, n_sub):
    t = pl.program_id(1)

    @pl.when(t < n_sub)
    def _():
        _mix_prompt_step(t, u_ref, v_ref, hb_ref, hbh_ref, c_ref, ch_ref,
                         ws_ref, bsb_ref, wgrp_ref, bsc_ref,
                         wdw_ref, bdw_ref, clg_ref, clb_ref,
                         pa_ref, pb_ref, pc_ref, hb_ext, c_ext, pool_scr, conv_scr)

    @pl.when(t == n_sub)
    def _():
        _mix_sample_step(n_sub * v_ref.shape[0], us_ref, vs_ref, hbs_ref, cs_ref, sp_ref, sc_ref,
                         ws0_ref, bs0_ref, wgrp_ref, bsc_ref,
                         wdw_ref, bdw_ref, clg_ref, clb_ref,
                         pa_ref, pb_ref, pc_ref)


def _mix(u3, v3, hb3, c3, state_pool, state_conv, lw, layer, seq):
    batch, tile_rows, d_mix = hb3.shape
    rows_s = tile_rows - seq
    n_sub = seq // MIX_SUB
    sub_idx = lambda t: jnp.minimum(t, n_sub - 1)
    vec = lambda: pl.BlockSpec((None, 1, d_mix), lambda b, t: (layer, 0, 0))
    sub_spec = lambda: pl.BlockSpec((None, MIX_SUB, d_mix), lambda b, t: (b, sub_idx(t), 0))
    smp_spec = lambda: pl.BlockSpec((None, rows_s, d_mix), lambda b, t: (b, seq // rows_s, 0))

    def halo_spec(rows):
        per = MIX_SUB // rows
        return pl.BlockSpec((None, rows, d_mix),
                            lambda b, t: (b, jnp.maximum(sub_idx(t) * per - 1, 0), 0))

    def state_spec(buf):
        return pl.BlockSpec((None, buf, rows_s, d_mix), lambda b, t: (layer, 0, b, 0),
                            pipeline_mode=_ONCE)

    head_spec = lambda: pl.BlockSpec((None, A_HEADS, CHUNK, CHUNK), lambda b, t: (layer, 0, 0, 0))
    tile_out = pl.BlockSpec((None, tile_rows, d_mix), lambda b, t: (b, 0, 0))
    out3 = jax.ShapeDtypeStruct((batch, tile_rows, d_mix), BF16)
    return pl.pallas_call(
        functools.partial(_mix_kernel, n_sub=n_sub),
        grid=(batch, n_sub + 1),
        in_specs=[
            sub_spec(), sub_spec(),
            sub_spec(), halo_spec(POOL_HALO),
            sub_spec(), halo_spec(CONV_HALO),
            smp_spec(), smp_spec(), smp_spec(), smp_spec(),
            state_spec(POOL_BUF), state_spec(CONV_BUF),
            head_spec(), head_spec(),
            vec(), vec(),
            pl.BlockSpec((None,) + lw["b_w_group"].shape[1:], lambda b, t: (layer, 0, 0, 0)),
            vec(),
            pl.BlockSpec((None, CONV_WIDTH, d_mix), lambda b, t: (layer, 0, 0)),
            vec(), vec(), vec(),
        ],
        out_specs=[tile_out, tile_out, tile_out],
        out_shape=[out3, out3, out3],
        scratch_shapes=[
            pltpu.VMEM((POOL_HALO + MIX_SUB, d_mix), F32),
            pltpu.VMEM((CONV_HALO + MIX_SUB, d_mix), F32),
            pltpu.VMEM((MIX_SUB, d_mix), F32),
            pltpu.VMEM((MIX_SUB, d_mix), F32),
        ],
        compiler_params=_cparams(("parallel", "arbitrary"), 56),
        name="mix",
    )(u3, v3, hb3, hb3, c3, c3, u3, v3, hb3, c3, state_pool, state_conv,
      lw["a_ws"], lw["a_bs_b"], lw["a_ws0"], lw["a_bs0"],
      lw["b_w_group"], lw["b_scale"], lw["c_w_dw"], lw["c_b_dw"], lw["c_ln_g"], lw["c_ln_b"])


def _merge_kernel(pa_ref, pb_ref, pc_ref, wa_ref, wb_ref, wc_ref,
                  g0_ref, g1_ref, g2_ref, o_ref):
    wa = wa_ref[...].astype(BF16)
    wb = wb_ref[...].astype(BF16)
    wc = wc_ref[...].astype(BF16)
    rows = o_ref.shape[0] // ROW_SPLITS
    for i in range(ROW_SPLITS):
        rs = slice(i * rows, (i + 1) * rows)
        merged = (g0_ref[rs, :] * jnp.dot(pa_ref[rs, :], wa, preferred_element_type=F32)
                  + g1_ref[rs, :] * jnp.dot(pb_ref[rs, :], wb, preferred_element_type=F32)
                  + g2_ref[rs, :] * jnp.dot(pc_ref[rs, :], wc, preferred_element_type=F32))
        o_ref[rs, :] = merged.astype(o_ref.dtype)


def _merge(pa, pb, pc, gates_blocked, w_a, w_b, w_c, layer, tm, tn):
    m_rows, d_mix = pa.shape
    d_model = w_a.shape[-1]
    nb = d_model // tn
    assert gates_blocked.shape == (N_BRANCH * nb, m_rows, tn)
    act = lambda: pl.BlockSpec((tm, d_mix), lambda m, n: (m, 0))
    wsp = lambda: pl.BlockSpec((None, d_mix, tn), lambda m, n: (layer, 0, n))
    gsp = lambda j: pl.BlockSpec((None, tm, tn), lambda m, n: (j * nb + n, m, 0))
    return pl.pallas_call(
        _merge_kernel,
        grid=(m_rows // tm, nb),
        in_specs=[act(), act(), act(), wsp(), wsp(), wsp(), gsp(0), gsp(1), gsp(2)],
        out_specs=pl.BlockSpec((tm, tn), lambda m, n: (m, n)),
        out_shape=jax.ShapeDtypeStruct((m_rows, d_model), BF16),
        compiler_params=_cparams(("parallel", "arbitrary"), 52),
        name="merge",
    )(pa, pb, pc, w_a, w_b, w_c, gates_blocked, gates_blocked, gates_blocked)


def _projout_accumulate(m_ref, w_ref, o_ref):
    @pl.when(pl.program_id(2) == 0)
    def _():
        o_ref[...] = jnp.zeros_like(o_ref)

    o_ref[...] += _bdot(m_ref[...], w_ref[...])


def _projout_ln_kernel(m_ref, w_ref, x_ref, g_ref, b_ref, o_ref, *, alpha):
    _projout_accumulate(m_ref, w_ref, o_ref)

    @pl.when(pl.program_id(2) == pl.num_programs(2) - 1)
    def _():
        _residual_ln_rows(x_ref, o_ref, g_ref, b_ref, None, alpha, o_ref.shape[0] // LN_ROWS)


def _projout_ln_inputs_kernel(m_ref, w_ref, xp_ref, xs_ref, g_ref, b_ref, o_ref, *, alpha):
    _projout_accumulate(m_ref, w_ref, o_ref)
    n_chunks = o_ref.shape[0] // LN_ROWS
    rows_s = xs_ref.shape[0]
    last = (n_chunks - 1) * LN_ROWS
    is_last_k = pl.program_id(2) == pl.num_programs(2) - 1
    second_half = pl.program_id(1) == 1

    @pl.when(is_last_k)
    def _():
        _residual_ln_rows(xp_ref, o_ref, g_ref, b_ref, None, alpha, n_chunks - 1)

    @pl.when(jnp.logical_and(is_last_k, jnp.logical_not(second_half)))
    def _():
        _residual_ln_chunk(xp_ref[last:last + LN_ROWS, :], o_ref, g_ref, b_ref, None, alpha, last)

    @pl.when(jnp.logical_and(is_last_k, second_half))
    def _():
        x_rows = jnp.concatenate([xp_ref[last:last + LN_ROWS - rows_s, :], xs_ref[...]], axis=0)
        _residual_ln_chunk(x_rows, o_ref, g_ref, b_ref, None, alpha, last)


def _projout_ln(merged, w_out, x_sources, g, b, layer, alpha, tm, tk):
    m_rows = merged.shape[0]
    d = w_out.shape[-1]
    batch = m_rows // (2 * tm)
    vec = lambda: pl.BlockSpec((None, 1, d), lambda bb, h, k: (layer, 0, 0))
    row_idx = lambda bb, h: bb * 2 + h
    if len(x_sources) == 1:
        kern = _projout_ln_kernel
        x_specs = [pl.BlockSpec((tm, d), lambda bb, h, k: (row_idx(bb, h), 0), pipeline_mode=_ONCE)]
    else:
        kern = _projout_ln_inputs_kernel
        rows_s = x_sources[1].shape[1]
        x_specs = [pl.BlockSpec((None, tm, d), lambda bb, h, k: (bb, h, 0), pipeline_mode=_ONCE),
                   pl.BlockSpec((None, rows_s, d), lambda bb, h, k: (bb, 0, 0))]
    return pl.pallas_call(
        functools.partial(kern, alpha=alpha),
        grid=(batch, 2, merged.shape[1] // tk),
        in_specs=[
            pl.BlockSpec((tm, tk), lambda bb, h, k: (row_idx(bb, h), k)),
            pl.BlockSpec((None, tk, d), lambda bb, h, k: (layer, k, 0)),
            *x_specs,
            vec(), vec(),
        ],
        out_specs=pl.BlockSpec((tm, d), lambda bb, h, k: (row_idx(bb, h), 0)),
        out_shape=jax.ShapeDtypeStruct((m_rows, d), F32),
        compiler_params=_cparams(("parallel", "parallel", "arbitrary"), 48),
        name="projout_ln",
    )(merged, w_out, *x_sources, g, b)


def _ffn_ln_kernel(x_ref, wg_ref, wu_ref, wd_ref, g_ref, b_ref, of_ref, *maybe_bf, alpha):
    f = pl.program_id(1)

    @pl.when(f == 0)
    def _():
        of_ref[...] = jnp.zeros_like(of_ref)

    x = x_ref[...].astype(BF16)
    gate = _bdot(x, wg_ref[...])
    up = _bdot(x, wu_ref[...])
    hid = (gate * _sigmoid(gate) * up).astype(BF16)
    of_ref[...] += _bdot(hid, wd_ref[...])

    @pl.when(f == pl.num_programs(1) - 1)
    def _():
        _residual_ln_rows(x_ref, of_ref, g_ref, b_ref, maybe_bf[0] if maybe_bf else None, alpha,
                          of_ref.shape[0] // LN_ROWS)


def _final_row_copies(acc, yp_hbm, ys_hbm, tile, sems, seq):
    per_chunk, n_sem = [], 0
    for i in range(acc.shape[0] // LN_ROWS):
        r0, r1 = i * LN_ROWS, (i + 1) * LN_ROWS
        pieces = []
        if r0 < seq:
            n = min(r1, seq) - r0
            pieces.append(pltpu.make_async_copy(acc.at[pl.ds(r0, n)],
                                                yp_hbm.at[tile, pl.ds(r0, n)], sems.at[n_sem]))
            n_sem += 1
        if r1 > seq:
            s0 = max(r0, seq)
            pieces.append(pltpu.make_async_copy(acc.at[pl.ds(s0, r1 - s0)],
                                                ys_hbm.at[tile, pl.ds(s0 - seq, r1 - s0)],
                                                sems.at[n_sem]))
            n_sem += 1
        per_chunk.append(pieces)
    return per_chunk


def _ffn_ln_final_kernel(x_ref, wg_ref, wu_ref, wd_ref, g_ref, b_ref, yp_hbm, ys_hbm, acc, sems,
                         *, alpha, seq):
    f = pl.program_id(1)

    @pl.when(f == 0)
    def _():
        acc[...] = jnp.zeros_like(acc)

    x = x_ref[...].astype(BF16)
    gate = _bdot(x, wg_ref[...])
    up = _bdot(x, wu_ref[...])
    hid = (gate * _sigmoid(gate) * up).astype(BF16)
    acc[...] += _bdot(hid, wd_ref[...])

    @pl.when(f == pl.num_programs(1) - 1)
    def _():
        per_chunk = _final_row_copies(acc, yp_hbm, ys_hbm, pl.program_id(0), sems, seq)
        for i, pieces in enumerate(per_chunk):
            r = i * LN_ROWS
            _residual_ln_chunk(x_ref[r:r + LN_ROWS, :], acc, g_ref, b_ref, None, alpha, r)
            for cp in pieces:
                cp.start()
        for pieces in per_chunk:
            for cp in pieces:
                cp.wait()


def _ffn_ln_final(x, w_up, w_down, g, b, layer, alpha, tm, tf, seq):
    m_rows, d = x.shape
    batch = m_rows // tm
    d_ff = w_down.shape[1]
    nf = d_ff // tf
    n_chunks = tm // LN_ROWS
    vec = lambda: pl.BlockSpec((None, 1, d), lambda m, f: (layer, 0, 0))
    return pl.pallas_call(
        functools.partial(_ffn_ln_final_kernel, alpha=alpha, seq=seq),
        grid=(batch, nf),
        in_specs=[
            pl.BlockSpec((tm, d), lambda m, f: (m, 0), pipeline_mode=_ONCE),
            pl.BlockSpec((None, d, tf), lambda m, f: (layer, 0, f)),
            pl.BlockSpec((None, d, tf), lambda m, f: (layer, 0, nf + f)),
            pl.BlockSpec((None, tf, d), lambda m, f: (layer, f, 0)),
            vec(), vec(),
        ],
        out_specs=[pl.BlockSpec(memory_space=pl.ANY), pl.BlockSpec(memory_space=pl.ANY)],
        out_shape=[jax.ShapeDtypeStruct((batch, seq, d), F32),
                   jax.ShapeDtypeStruct((batch, tm - seq, d), F32)],
        scratch_shapes=[pltpu.VMEM((tm, d), F32), pltpu.SemaphoreType.DMA((n_chunks + 1,))],
        compiler_params=_cparams(("arbitrary", "arbitrary"), 56),
        name="ffn_ln_final",
    )(x, w_up, w_up, w_down, g, b)


def _ffn_ln(x, w_up, w_down, g, b, layer, alpha, tm, tf, emit_bf16):
    m_rows, d = x.shape
    d_ff = w_down.shape[1]
    nf = d_ff // tf
    row = lambda: pl.BlockSpec((tm, d), lambda m, f: (m, 0), pipeline_mode=_ONCE)
    vec = lambda: pl.BlockSpec((None, 1, d), lambda m, f: (layer, 0, 0))
    out_specs = [row()]
    out_shape = [jax.ShapeDtypeStruct((m_rows, d), F32)]
    if emit_bf16:
        out_specs.append(row())
        out_shape.append(jax.ShapeDtypeStruct((m_rows, d), BF16))
    return pl.pallas_call(
        functools.partial(_ffn_ln_kernel, alpha=alpha),
        grid=(m_rows // tm, nf),
        in_specs=[
            row(),
            pl.BlockSpec((None, d, tf), lambda m, f: (layer, 0, f)),
            pl.BlockSpec((None, d, tf), lambda m, f: (layer, 0, nf + f)),
            pl.BlockSpec((None, tf, d), lambda m, f: (layer, f, 0)),
            vec(), vec(),
        ],
        out_specs=out_specs,
        out_shape=out_shape,
        compiler_params=_cparams(("parallel", "arbitrary"), 60),
        name="ffn_ln",
    )(x, w_up, w_up, w_down, g, b)


def _state_shift_kernel(s_ref, *refs):
    new_refs, o_ref = refs[:-1], refs[-1]
    layer = pl.program_id(0)
    keep = s_ref.shape[0] - 1
    new = new_refs[0][...]
    for j in range(1, len(new_refs)):
        new = jnp.where(layer == j, new_refs[j][...], new)
    o_ref[0:keep] = s_ref[1:keep + 1]
    o_ref[keep] = new


def _state_shift(state_t, new_rows3, seq):
    depth, buf, _, d_mix = state_t.shape
    batch, tile_rows, _ = new_rows3[0].shape
    rows_s = tile_rows - seq
    blk = pl.BlockSpec((None, buf, rows_s, d_mix), lambda l, b: (l, 0, b, 0))
    new_spec = pl.BlockSpec((None, rows_s, d_mix), lambda l, b: (b, seq // rows_s, 0))
    return pl.pallas_call(
        _state_shift_kernel,
        grid=(depth, batch),
        in_specs=[blk] + [new_spec] * depth,
        out_specs=blk,
        out_shape=jax.ShapeDtypeStruct(state_t.shape, state_t.dtype),
        compiler_params=_cparams(("parallel", "parallel"), 32),
        name="state_shift",
    )(state_t, *new_rows3)


def kernel(x_prompt, x_sample, state_pool, state_conv, w_in, b_in, a_ln_g, a_ln_b, a_ws, a_bs, w_a_out, b_w_group, b_scale, w_b_out, c_w_dw, c_b_dw, c_ln_g, c_ln_b, w_c_out, w_out, ln1_g, ln1_b, w_ffn_up, w_ffn_down, ln2_g, ln2_b):
    batch, seq, d_model = x_prompt.shape
    m_sample = x_sample.shape[0] * x_sample.shape[1]
    depth = w_in.shape[0]
    d_mix = a_ln_g.shape[-1]
    head_dim = d_mix // A_HEADS
    alpha = (2.0 * depth) ** 0.25

    assert x_sample.shape[1] == 1 and m_sample % batch == 0
    rows_s = m_sample // batch
    tile_rows = seq + rows_s
    m_rows = batch * tile_rows
    half_rows = tile_rows // 2
    assert rows_s % BF16_ROWS == 0 and seq % rows_s == 0 and seq % MIX_SUB == 0
    assert seq % PACK_SUB == 0
    assert tile_rows % 2 == 0 and half_rows % LN_ROWS == 0 and LN_ROWS % BF16_ROWS == 0
    assert rows_s <= LN_ROWS and (LN_ROWS - rows_s) % SUBLANES == 0
    assert tile_rows % (ROW_SPLITS * BF16_ROWS) == 0 and rows_s <= tile_rows // ROW_SPLITS

    vec3 = lambda a: a.reshape(depth, 1, a.shape[-1])
    lw = {
        "a_ln_g": vec3(a_ln_g), "a_ln_b": vec3(a_ln_b),
        "a_ws": a_ws,
        "a_bs_b": jnp.broadcast_to(a_bs[..., None], a_bs.shape + (head_dim,)),
        "a_ws0": jnp.repeat(a_ws[:, :, 0, 0], head_dim, axis=-1).reshape(depth, 1, d_mix),
        "a_bs0": jnp.repeat(a_bs[:, :, 0], head_dim, axis=-1).reshape(depth, 1, d_mix),
        "b_w_group": b_w_group, "b_scale": vec3(b_scale),
        "c_w_dw": c_w_dw, "c_b_dw": vec3(c_b_dw),
        "c_ln_g": vec3(c_ln_g), "c_ln_b": vec3(c_ln_b),
    }
    b_in3 = vec3(b_in)
    ln1_g3, ln1_b3, ln2_g3, ln2_b3 = vec3(ln1_g), vec3(ln1_b), vec3(ln2_g), vec3(ln2_b)

    s1 = 2 * d_mix
    s2 = s1 + d_mix
    s3 = s2 + 2 * d_mix

    x_sample3 = x_sample.reshape(batch, rows_s, d_model)
    x_sources = (x_prompt, x_sample3)
    xb = _pack_bf16(x_prompt, x_sample3).reshape(m_rows, d_model)
    tile3 = lambda a: a.reshape(batch, tile_rows, a.shape[-1])
    pool_t = jnp.transpose(state_pool, (0, 2, 1, 3))
    conv_t = jnp.transpose(state_conv, (0, 2, 1, 3))

    hb_l, c_l, v_l = [], [], []
    for l in range(depth):
        u = _proj(xb, w_in, b_in3, l, 0, d_mix, "gelu", BF16, tile_rows, d_mix, "proj_u")
        v, v_new = _proj_gelu_ln(xb, w_in, b_in3, lw["a_ln_g"], lw["a_ln_b"], l, d_mix, d_mix,
                                 tile_rows, rows_s, "proj_v")
        hb = _proj(xb, w_in, b_in3, l, s1, d_mix, "none", F32, tile_rows, d_mix, "proj_b")
        c = _glu_proj(xb, w_in, b_in3, l, s2, d_mix, tile_rows, 512, "proj_c")
        gates = _proj_sigmoid_blocked(xb, w_in, b_in3, l, s3, N_BRANCH * d_model,
                                      tile_rows, 1024, 256, "proj_gates")

        pa, pb, pc = _mix(tile3(u), tile3(v), tile3(hb), tile3(c), pool_t, conv_t, lw, l, seq)
        flat = lambda a: a.reshape(m_rows, d_mix)
        merged = _merge(flat(pa), flat(pb), flat(pc), gates, w_a_out, w_b_out, w_c_out,
                        l, tile_rows, 256)
        x1 = _projout_ln(merged, w_out, x_sources, ln1_g3, ln1_b3, l, alpha, half_rows, 512)
        hb_l.append(tile3(hb))
        c_l.append(tile3(c))
        v_l.append(v_new)
        if l + 1 < depth:
            x_next, xb = _ffn_ln(x1, w_ffn_up, w_ffn_down, ln2_g3, ln2_b3, l, alpha,
                                 tile_rows, 256, emit_bf16=True)
            x_sources = (x_next,)
        else:
            y_prompt, y_sample3 = _ffn_ln_final(x1, w_ffn_up, w_ffn_down, ln2_g3, ln2_b3, l, alpha,
                                                tile_rows, 256, seq)

    y_sample = y_sample3.reshape(m_sample, 1, d_model)
    new_pool_prompt = jnp.stack([a[:, seq - POOL_BUF:seq] for a in hb_l])
    new_conv_prompt = jnp.stack([a[:, seq - CONV_BUF:seq] for a in c_l])
    new_pool_sample = jnp.transpose(_state_shift(pool_t, hb_l, seq), (0, 2, 1, 3))
    new_conv_sample = jnp.transpose(_state_shift(conv_t, c_l, seq), (0, 2, 1, 3))
    new_chunk_v = jnp.stack(v_l)[:, :, None, :]
    return (y_prompt, y_sample, new_pool_prompt, new_conv_prompt,
            new_pool_sample, new_conv_sample, new_chunk_v)
```

```python
import functools

import jax
import jax.numpy as jnp
from jax import lax
from jax.experimental import pallas as pl
from jax.experimental.pallas import tpu as pltpu

F32 = jnp.float32
BF16 = jnp.bfloat16

LN_EPS = 1e-5
A_HEADS = 8
CHUNK = 128
POOL_WINDOWS = (2, 4, 8, 16)
POOL_BUF = 15
CONV_WIDTH = 31
CONV_BUF = CONV_WIDTH - 1
N_BRANCH = 3

SUBLANES = 8
LANES = 128
BF16_ROWS = 16
POOL_HALO = 16
CONV_HALO = 32
ROW_BLOCK = 64
MIX_SUB = 256
PACK_SUB = 512
LN_ROWS = 208
ROW_SPLITS = 5
STAGE_SLOTS = 2
V7X_VMEM_BYTES = 64 * 1024 * 1024

_ONCE = pl.Buffered(1)


def _cparams(semantics, vmem_mb):
    assert vmem_mb * 1024 * 1024 < V7X_VMEM_BYTES
    return pltpu.CompilerParams(dimension_semantics=semantics,
                                vmem_limit_bytes=vmem_mb * 1024 * 1024)


def _layer_norm(x, g, b):
    mu = jnp.mean(x, axis=-1, keepdims=True)
    xc = x - mu
    var = jnp.mean(xc * xc, axis=-1, keepdims=True)
    return xc * lax.rsqrt(var + LN_EPS) * g + b


def _gelu_exact(x):
    return 0.5 * x * (1.0 + lax.erf(x * (0.5 ** 0.5)))


def _sigmoid(x):
    return 1.0 / (1.0 + jnp.exp(-x))


def _bdot(a, w):
    return jnp.dot(a, w.astype(BF16), preferred_element_type=F32)


def _residual_ln_chunk(x_rows, acc_ref, g_ref, b_ref, alpha, r):
    y = _layer_norm(alpha * x_rows + acc_ref[pl.ds(r, LN_ROWS), :], g_ref[...], b_ref[...])
    acc_ref[pl.ds(r, LN_ROWS), :] = y
    return y


def _residual_ln_rows(x_ref, acc_ref, g_ref, b_ref, alpha):
    def body(i, carry):
        r = pl.multiple_of(i * LN_ROWS, LN_ROWS)
        _residual_ln_chunk(x_ref[pl.ds(r, LN_ROWS), :], acc_ref, g_ref, b_ref, alpha, r)
        return carry

    lax.fori_loop(0, acc_ref.shape[0] // LN_ROWS, body, 0)


def _pack_kernel(xp_ref, xs_ref, o_ref, *, n_sub):
    t = pl.program_id(1)
    sub = xp_ref.shape[0]

    @pl.when(t < n_sub)
    def _():
        o_ref[pl.ds(pl.multiple_of(t * sub, sub), sub), :] = xp_ref[...].astype(BF16)

    @pl.when(t == n_sub)
    def _():
        o_ref[n_sub * sub:, :] = xs_ref[...].astype(BF16)


def _pack_bf16(x_prompt, x_sample3):
    batch, seq, d = x_prompt.shape
    rows_s = x_sample3.shape[1]
    n_sub = seq // PACK_SUB
    return pl.pallas_call(
        functools.partial(_pack_kernel, n_sub=n_sub),
        grid=(batch, n_sub + 1),
        in_specs=[
            pl.BlockSpec((None, PACK_SUB, d), lambda b, t: (b, jnp.minimum(t, n_sub - 1), 0)),
            pl.BlockSpec((None, rows_s, d), lambda b, t: (b, 0, 0)),
        ],
        out_specs=pl.BlockSpec((None, seq + rows_s, d), lambda b, t: (b, 0, 0)),
        out_shape=jax.ShapeDtypeStruct((batch, seq + rows_s, d), BF16),
        compiler_params=_cparams(("parallel", "arbitrary"), 40),
        name="pack",
    )(x_prompt, x_sample3)


def _proj_kernel(x_ref, w_ref, b_ref, o_ref, *, act):
    w = w_ref[...].astype(BF16)
    rows = x_ref.shape[0] // ROW_SPLITS
    for i in range(ROW_SPLITS):
        rs = slice(i * rows, (i + 1) * rows)
        h = jnp.dot(x_ref[rs, :], w, preferred_element_type=F32) + b_ref[...]
        if act == "gelu":
            h = _gelu_exact(h)
        elif act == "sigmoid":
            h = _sigmoid(h)
        o_ref[rs, :] = h.astype(o_ref.dtype)


def _proj_sigmoid_blocked_kernel(x_ref, w_ref, b_ref, o_ref):
    w = w_ref[...].astype(BF16)
    n_blk, tile_rows, sub = o_ref.shape
    rows = tile_rows // ROW_SPLITS
    for i in range(ROW_SPLITS):
        rs = slice(i * rows, (i + 1) * rows)
        h = _sigmoid(jnp.dot(x_ref[rs, :], w, preferred_element_type=F32) + b_ref[...])
        for j in range(n_blk):
            o_ref[j, rs, :] = h[:, j * sub:(j + 1) * sub].astype(o_ref.dtype)


def _proj_gelu_ln_kernel(x_ref, w_ref, b_ref, g_ref, beta_ref, o_ref, os_ref):
    w = w_ref[...].astype(BF16)
    rows = x_ref.shape[0] // ROW_SPLITS
    n_s = os_ref.shape[0]
    for i in range(ROW_SPLITS):
        rs = slice(i * rows, (i + 1) * rows)
        h = jnp.dot(x_ref[rs, :], w, preferred_element_type=F32) + b_ref[...]
        v = _layer_norm(_gelu_exact(h), g_ref[...], beta_ref[...])
        o_ref[rs, :] = v.astype(BF16)
        if i == ROW_SPLITS - 1:
            os_ref[...] = v[rows - n_s:, :]


def _glu_kernel(x_ref, wv_ref, wg_ref, bv_ref, bg_ref, o_ref):
    wv = wv_ref[...].astype(BF16)
    wg = wg_ref[...].astype(BF16)
    rows = x_ref.shape[0] // ROW_SPLITS
    for i in range(ROW_SPLITS):
        rs = slice(i * rows, (i + 1) * rows)
        x = x_ref[rs, :]
        val = jnp.dot(x, wv, preferred_element_type=F32) + bv_ref[...]
        gate = jnp.dot(x, wg, preferred_element_type=F32) + bg_ref[...]
        o_ref[rs, :] = val * _sigmoid(gate)


def _proj(x, w, b, layer, col0, ncols, act, out_dtype, tm, tn, name):
    m_rows, k = x.shape
    off = col0 // tn
    return pl.pallas_call(
        functools.partial(_proj_kernel, act=act),
        grid=(m_rows // tm, ncols // tn),
        in_specs=[
            pl.BlockSpec((tm, k), lambda m, n: (m, 0)),
            pl.BlockSpec((None, k, tn), lambda m, n: (layer, 0, off + n)),
            pl.BlockSpec((None, 1, tn), lambda m, n: (layer, 0, off + n)),
        ],
        out_specs=pl.BlockSpec((tm, tn), lambda m, n: (m, n)),
        out_shape=jax.ShapeDtypeStruct((m_rows, ncols), out_dtype),
        compiler_params=_cparams(("parallel", "arbitrary"), 56),
        name=name,
    )(x, w, b)


def _proj_sigmoid_blocked(x, w, b, layer, col0, ncols, tm, tn, sub, name):
    m_rows, k = x.shape
    off = col0 // tn
    return pl.pallas_call(
        _proj_sigmoid_blocked_kernel,
        grid=(m_rows // tm, ncols // tn),
        in_specs=[
            pl.BlockSpec((tm, k), lambda m, n: (m, 0)),
            pl.BlockSpec((None, k, tn), lambda m, n: (layer, 0, off + n)),
            pl.BlockSpec((None, 1, tn), lambda m, n: (layer, 0, off + n)),
        ],
        out_specs=pl.BlockSpec((tn // sub, tm, sub), lambda m, n: (n, m, 0)),
        out_shape=jax.ShapeDtypeStruct((ncols // sub, m_rows, sub), BF16),
        compiler_params=_cparams(("parallel", "arbitrary"), 56),
        name=name,
    )(x, w, b)


def _proj_gelu_ln(x, w, b, g, beta, layer, col0, ncols, tm, rows_s, name):
    m_rows, k = x.shape
    off = col0 // ncols
    vec = lambda: pl.BlockSpec((None, 1, ncols), lambda m: (layer, 0, 0))
    return pl.pallas_call(
        _proj_gelu_ln_kernel,
        grid=(m_rows // tm,),
        in_specs=[
            pl.BlockSpec((tm, k), lambda m: (m, 0)),
            pl.BlockSpec((None, k, ncols), lambda m: (layer, 0, off)),
            pl.BlockSpec((None, 1, ncols), lambda m: (layer, 0, off)),
            vec(), vec(),
        ],
        out_specs=[pl.BlockSpec((tm, ncols), lambda m: (m, 0)),
                   pl.BlockSpec((rows_s, ncols), lambda m: (m, 0))],
        out_shape=[jax.ShapeDtypeStruct((m_rows, ncols), BF16),
                   jax.ShapeDtypeStruct((m_rows // tm * rows_s, ncols), F32)],
        compiler_params=_cparams(("parallel",), 52),
        name=name,
    )(x, w, b, g, beta)


def _glu_proj(x, w, b, layer, col0, ncols, tm, tn, name):
    m_rows, k = x.shape
    off_v = col0 // tn
    off_g = (col0 + ncols) // tn
    return pl.pallas_call(
        _glu_kernel,
        grid=(m_rows // tm, ncols // tn),
        in_specs=[
            pl.BlockSpec((tm, k), lambda m, n: (m, 0)),
            pl.BlockSpec((None, k, tn), lambda m, n: (layer, 0, off_v + n)),
            pl.BlockSpec((None, k, tn), lambda m, n: (layer, 0, off_g + n)),
            pl.BlockSpec((None, 1, tn), lambda m, n: (layer, 0, off_v + n)),
            pl.BlockSpec((None, 1, tn), lambda m, n: (layer, 0, off_g + n)),
        ],
        out_specs=pl.BlockSpec((tm, tn), lambda m, n: (m, n)),
        out_shape=jax.ShapeDtypeStruct((m_rows, ncols), F32),
        compiler_params=_cparams(("parallel", "arbitrary"), 56),
        name=name,
    )(x, w, w, b, b)


def _trailing_sum(x, w):
    result, offset, part, span = None, 0, x, 1
    while True:
        if w & span:
            term = part if offset == 0 else pltpu.roll(part, offset, axis=0)
            result = term if result is None else result + term
            offset += span
        if span * 2 > w:
            return result
        part = part + pltpu.roll(part, span, axis=0)
        span *= 2


def _conv_rows(x, wdw_ref, bias, ls):
    lead = CONV_HALO - CONV_BUF
    acc = jnp.broadcast_to(bias, (ROW_BLOCK, LANES))
    for r in range(SUBLANES):
        taps = [k for k in range(CONV_WIDTH) if (lead + k) % SUBLANES == r]
        if not taps:
            continue
        xr = x if r == 0 else pltpu.roll(x, x.shape[0] - r, axis=0)
        for k in taps:
            q = (lead + k) // SUBLANES
            acc = acc + wdw_ref[k:k + 1, ls] * xr[SUBLANES * q:SUBLANES * q + ROW_BLOCK]
    return acc


def _mix_prompt_step(t, u_ref, v_ref, hb_ref, hbh_ref, c_ref, ch_ref,
                     ws_ref, bsb_ref, wgrp_ref, bsc_ref,
                     wdw_ref, bdw_ref, clg_ref, clb_ref,
                     pa_ref, pb_ref, pc_ref, hb_ext, c_ext, pool_scr, conv_scr):
    sub, d_mix = v_ref.shape
    n_chunks = sub // CHUNK
    head_dim = d_mix // A_HEADS
    group_dim = d_mix // len(POOL_WINDOWS)
    base = pl.multiple_of(t * sub, sub)

    vb = v_ref[...]
    row = lax.broadcasted_iota(jnp.int32, (CHUNK, CHUNK), 0)
    col = lax.broadcasted_iota(jnp.int32, (CHUNK, CHUNK), 1)
    for h in range(A_HEADS):
        hs = slice(h * head_dim, (h + 1) * head_dim)
        wm = jnp.where(row >= col, ws_ref[h], 0.0).astype(BF16)
        rhs = jnp.concatenate(
            [vb[ci * CHUNK:(ci + 1) * CHUNK, hs] for ci in range(n_chunks)], axis=1)
        s = jnp.dot(wm, rhs, preferred_element_type=F32)
        bias = bsb_ref[h]
        for ci in range(n_chunks):
            rs = slice(ci * CHUNK, (ci + 1) * CHUNK)
            s_c = s[:, ci * head_dim:(ci + 1) * head_dim] + bias
            pa_ref[pl.ds(base + ci * CHUNK, CHUNK), hs] = (u_ref[rs, hs] * s_c).astype(BF16)

    first = t == 0
    hb_ext[0:POOL_HALO, :] = jnp.where(first, 0.0, hbh_ref[...])
    hb_ext[POOL_HALO:POOL_HALO + sub, :] = hb_ref[...]

    def pool_body(i, carry):
        r0 = pl.multiple_of(i * ROW_BLOCK, ROW_BLOCK)
        pos = base + r0 + lax.broadcasted_iota(jnp.int32, (ROW_BLOCK, group_dim), 0)
        for g, w in enumerate(POOL_WINDOWS):
            gs = slice(g * group_dim, (g + 1) * group_dim)
            x = hb_ext[pl.ds(r0, POOL_HALO + ROW_BLOCK), gs]
            tok = x[POOL_HALO:POOL_HALO + ROW_BLOCK]
            win = _trailing_sum(x, w)[POOL_HALO:POOL_HALO + ROW_BLOCK]
            cnt = jnp.minimum(pos + 1, w).astype(F32)
            pool_scr[pl.ds(r0, ROW_BLOCK), gs] = win / cnt - tok
        return carry

    lax.fori_loop(0, sub // ROW_BLOCK, pool_body, 0)
    for g in range(len(POOL_WINDOWS)):
        gs = slice(g * group_dim, (g + 1) * group_dim)
        mixed = _bdot(pool_scr[:, gs].astype(BF16), wgrp_ref[g])
        pb_ref[pl.ds(base, sub), gs] = (mixed * bsc_ref[:, gs]).astype(BF16)

    c_ext[0:CONV_HALO, :] = jnp.where(first, 0.0, ch_ref[...])
    c_ext[CONV_HALO:CONV_HALO + sub, :] = c_ref[...]

    def conv_body(i, carry):
        r0 = pl.multiple_of(i * ROW_BLOCK, ROW_BLOCK)
        for lt in range(d_mix // LANES):
            ls = slice(lt * LANES, (lt + 1) * LANES)
            window = c_ext[pl.ds(r0, CONV_HALO + ROW_BLOCK), ls]
            conv_scr[pl.ds(r0, ROW_BLOCK), ls] = _conv_rows(window, wdw_ref, bdw_ref[:, ls], ls)
        return carry

    lax.fori_loop(0, sub // ROW_BLOCK, conv_body, 0)
    y = _layer_norm(conv_scr[...], clg_ref[...], clb_ref[...])
    pc_ref[pl.ds(base, sub), :] = (y * _sigmoid(y)).astype(BF16)


def _mix_sample_step(row0, u_ref, v_ref, hb_ref, c_ref, sp_ref, sc_ref,
                     ws0_ref, bs0_ref, wgrp_ref, bsc_ref,
                     wdw_ref, bdw_ref, clg_ref, clb_ref,
                     pa_ref, pb_ref, pc_ref):
    rows, d_mix = v_ref.shape
    group_dim = d_mix // len(POOL_WINDOWS)
    out_rows = slice(row0, row0 + rows)

    s = v_ref[...].astype(F32) * ws0_ref[...].astype(BF16).astype(F32) + bs0_ref[...]
    pa_ref[out_rows, :] = (u_ref[...] * s).astype(BF16)

    for g, w in enumerate(POOL_WINDOWS):
        gs = slice(g * group_dim, (g + 1) * group_dim)
        tok = hb_ref[:, gs]
        win = tok
        for k in range(POOL_BUF - (w - 1), POOL_BUF):
            win = win + sp_ref[k, :, gs]
        pooled = win / float(w) - tok
        mixed = _bdot(pooled.astype(BF16), wgrp_ref[g])
        pb_ref[out_rows, gs] = (mixed * bsc_ref[:, gs]).astype(BF16)

    conv = c_ref[...] * wdw_ref[CONV_BUF:CONV_WIDTH, :] + bdw_ref[...]
    for k in range(CONV_BUF):
        conv = conv + sc_ref[k] * wdw_ref[k:k + 1, :]
    y = _layer_norm(conv, clg_ref[...], clb_ref[...])
    pc_ref[out_rows, :] = (y * _sigmoid(y)).astype(BF16)


def _mix_kernel(u_ref, v_ref, hb_ref, hbh_ref, c_ref, ch_ref,
                us_ref, vs_ref, hbs_ref, cs_ref, sp_ref, sc_ref,
                ws_ref, bsb_ref, ws0_ref, bs0_ref, wgrp_ref, bsc_ref,
                wdw_ref, bdw_ref, clg_ref, clb_ref,
                pa_ref, pb_ref, pc_ref,
                hb_ext, c_ext, pool_scr, conv_scr, *, n_sub):
    t = pl.program_id(1)

    @pl.when(t < n_sub)
    def _():
        _mix_prompt_step(t, u_ref, v_ref, hb_ref, hbh_ref, c_ref, ch_ref,
                         ws_ref, bsb_ref, wgrp_ref, bsc_ref,
                         wdw_ref, bdw_ref, clg_ref, clb_ref,
                         pa_ref, pb_ref, pc_ref, hb_ext, c_ext, pool_scr, conv_scr)

    @pl.when(t == n_sub)
    def _():
        _mix_sample_step(n_sub * v_ref.shape[0], us_ref, vs_ref, hbs_ref, cs_ref, sp_ref, sc_ref,
                         ws0_ref, bs0_ref, wgrp_ref, bsc_ref,
                         wdw_ref, bdw_ref, clg_ref, clb_ref,
                         pa_ref, pb_ref, pc_ref)


def _mix(u3, v3, hb3, c3, state_pool, state_conv, lw, layer, seq):
    batch, tile_rows, d_mix = hb3.shape
    rows_s = tile_rows - seq
    n_sub = seq // MIX_SUB
    sub_idx = lambda t: jnp.minimum(t, n_sub - 1)
    vec = lambda: pl.BlockSpec((None, 1, d_mix), lambda b, t: (layer, 0, 0))
    sub_spec = lambda: pl.BlockSpec((None, MIX_SUB, d_mix), lambda b, t: (b, sub_idx(t), 0))
    smp_spec = lambda: pl.BlockSpec((None, rows_s, d_mix), lambda b, t: (b, seq // rows_s, 0))

    def halo_spec(rows):
        per = MIX_SUB // rows
        return pl.BlockSpec((None, rows, d_mix),
                            lambda b, t: (b, jnp.maximum(sub_idx(t) * per - 1, 0), 0))

    def state_spec(buf):
        return pl.BlockSpec((None, buf, rows_s, d_mix), lambda b, t: (layer, 0, b, 0),
                            pipeline_mode=_ONCE)

    head_spec = lambda: pl.BlockSpec((None, A_HEADS, CHUNK, CHUNK), lambda b, t: (layer, 0, 0, 0))
    tile_out = pl.BlockSpec((None, tile_rows, d_mix), lambda b, t: (b, 0, 0))
    out3 = jax.ShapeDtypeStruct((batch, tile_rows, d_mix), BF16)
    return pl.pallas_call(
        functools.partial(_mix_kernel, n_sub=n_sub),
        grid=(batch, n_sub + 1),
        in_specs=[
            sub_spec(), sub_spec(),
            sub_spec(), halo_spec(POOL_HALO),
            sub_spec(), halo_spec(CONV_HALO),
            smp_spec(), smp_spec(), smp_spec(), smp_spec(),
            state_spec(POOL_BUF), state_spec(CONV_BUF),
            head_spec(), head_spec(),
            vec(), vec(),
            pl.BlockSpec((None,) + lw["b_w_group"].shape[1:], lambda b, t: (layer, 0, 0, 0)),
            vec(),
            pl.BlockSpec((None, CONV_WIDTH, d_mix), lambda b, t: (layer, 0, 0)),
            vec(), vec(), vec(),
        ],
        out_specs=[tile_out, tile_out, tile_out],
        out_shape=[out3, out3, out3],
        scratch_shapes=[
            pltpu.VMEM((POOL_HALO + MIX_SUB, d_mix), F32),
            pltpu.VMEM((CONV_HALO + MIX_SUB, d_mix), F32),
            pltpu.VMEM((MIX_SUB, d_mix), F32),
            pltpu.VMEM((MIX_SUB, d_mix), F32),
        ],
        compiler_params=_cparams(("parallel", "arbitrary"), 56),
        name="mix",
    )(u3, v3, hb3, hb3, c3, c3, u3, v3, hb3, c3, state_pool, state_conv,
      lw["a_ws"], lw["a_bs_b"], lw["a_ws0"], lw["a_bs0"],
      lw["b_w_group"], lw["b_scale"], lw["c_w_dw"], lw["c_b_dw"], lw["c_ln_g"], lw["c_ln_b"])


def _merge_kernel(pa_ref, pb_ref, pc_ref, wa_ref, wb_ref, wc_ref,
                  g0_ref, g1_ref, g2_ref, o_ref):
    wa = wa_ref[...].astype(BF16)
    wb = wb_ref[...].astype(BF16)
    wc = wc_ref[...].astype(BF16)
    rows = o_ref.shape[0] // ROW_SPLITS
    for i in range(ROW_SPLITS):
        rs = slice(i * rows, (i + 1) * rows)
        merged = (g0_ref[rs, :] * jnp.dot(pa_ref[rs, :], wa, preferred_element_type=F32)
                  + g1_ref[rs, :] * jnp.dot(pb_ref[rs, :], wb, preferred_element_type=F32)
                  + g2_ref[rs, :] * jnp.dot(pc_ref[rs, :], wc, preferred_element_type=F32))
        o_ref[rs, :] = merged.astype(o_ref.dtype)


def _merge(pa, pb, pc, gates_blocked, w_a, w_b, w_c, layer, tm, tn):
    m_rows, d_mix = pa.shape
    d_model = w_a.shape[-1]
    nb = d_model // tn
    assert gates_blocked.shape == (N_BRANCH * nb, m_rows, tn)
    act = lambda: pl.BlockSpec((tm, d_mix), lambda m, n: (m, 0))
    wsp = lambda: pl.BlockSpec((None, d_mix, tn), lambda m, n: (layer, 0, n))
    gsp = lambda j: pl.BlockSpec((None, tm, tn), lambda m, n: (j * nb + n, m, 0))
    return pl.pallas_call(
        _merge_kernel,
        grid=(m_rows // tm, nb),
        in_specs=[act(), act(), act(), wsp(), wsp(), wsp(), gsp(0), gsp(1), gsp(2)],
        out_specs=pl.BlockSpec((tm, tn), lambda m, n: (m, n)),
        out_shape=jax.ShapeDtypeStruct((m_rows, d_model), BF16),
        compiler_params=_cparams(("parallel", "arbitrary"), 52),
        name="merge",
    )(pa, pb, pc, w_a, w_b, w_c, gates_blocked, gates_blocked, gates_blocked)


def _projout_accumulate(m_ref, w_ref, o_ref):
    @pl.when(pl.program_id(2) == 0)
    def _():
        o_ref[...] = jnp.zeros_like(o_ref)

    o_ref[...] += _bdot(m_ref[...], w_ref[...])


def _residual_copies(x_hbm, xbuf, sems, tile, half):
    tm = xbuf.shape[0]
    if len(x_hbm) == 1:
        row0 = pl.multiple_of((tile * 2 + half) * tm, tm)
        return [pltpu.make_async_copy(x_hbm[0].at[pl.ds(row0, tm)], xbuf, sems.at[0])]
    xp, xs = x_hbm
    if half == 0:
        return [pltpu.make_async_copy(xp.at[tile, pl.ds(0, tm)], xbuf, sems.at[0])]
    n_prompt = xp.shape[1] - tm
    return [pltpu.make_async_copy(xp.at[tile, pl.ds(tm, n_prompt)],
                                  xbuf.at[pl.ds(0, n_prompt)], sems.at[1]),
            pltpu.make_async_copy(xs.at[tile], xbuf.at[pl.ds(n_prompt, xs.shape[1])], sems.at[2])]


def _projout_ln_kernel(m_ref, w_ref, *refs, alpha, n_x):
    x_hbm, (g_ref, b_ref, o_ref, xbuf, sems) = refs[:n_x], refs[n_x:]
    tile, h, k = pl.program_id(0), pl.program_id(1), pl.program_id(2)
    for half in range(2):
        @pl.when(jnp.logical_and(k == 0, h == half))
        def _():
            for cp in _residual_copies(x_hbm, xbuf, sems, tile, half):
                cp.start()

    _projout_accumulate(m_ref, w_ref, o_ref)

    for half in range(2):
        @pl.when(jnp.logical_and(k == pl.num_programs(2) - 1, h == half))
        def _():
            for cp in _residual_copies(x_hbm, xbuf, sems, tile, half):
                cp.wait()
            _residual_ln_rows(xbuf, o_ref, g_ref, b_ref, alpha)


def _projout_ln(merged, w_out, x_sources, g, b, layer, alpha, tm, tk):
    m_rows = merged.shape[0]
    d = w_out.shape[-1]
    batch = m_rows // (2 * tm)
    vec = lambda: pl.BlockSpec((None, 1, d), lambda bb, h, k: (layer, 0, 0))
    row_idx = lambda bb, h: bb * 2 + h
    return pl.pallas_call(
        functools.partial(_projout_ln_kernel, alpha=alpha, n_x=len(x_sources)),
        grid=(batch, 2, merged.shape[1] // tk),
        in_specs=[
            pl.BlockSpec((tm, tk), lambda bb, h, k: (row_idx(bb, h), k)),
            pl.BlockSpec((None, tk, d), lambda bb, h, k: (layer, k, 0)),
            *[pl.BlockSpec(memory_space=pl.ANY) for _ in x_sources],
            vec(), vec(),
        ],
        out_specs=pl.BlockSpec((tm, d), lambda bb, h, k: (row_idx(bb, h), 0)),
        out_shape=jax.ShapeDtypeStruct((m_rows, d), F32),
        scratch_shapes=[pltpu.VMEM((tm, d), F32), pltpu.SemaphoreType.DMA((3,))],
        compiler_params=_cparams(("arbitrary", "arbitrary", "arbitrary"), 48),
        name="projout_ln",
    )(merged, w_out, *x_sources, g, b)


def _ffn_row_copies(acc, stage, out_a, out_b, tile, sems, seq, final):
    tm = acc.shape[0]
    per_chunk, n_sem = [], 0
    for i in range(tm // LN_ROWS):
        r0, r1 = i * LN_ROWS, (i + 1) * LN_ROWS
        pieces = []
        if not final:
            row0 = pl.multiple_of(tile * tm, tm)
            for src, dst in ((acc.at[pl.ds(r0, LN_ROWS)], out_a), (stage.at[i % STAGE_SLOTS], out_b)):
                pieces.append(pltpu.make_async_copy(src, dst.at[pl.ds(row0 + r0, LN_ROWS)],
                                                    sems.at[n_sem]))
                n_sem += 1
        else:
            if r0 < seq:
                n = min(r1, seq) - r0
                pieces.append(pltpu.make_async_copy(acc.at[pl.ds(r0, n)],
                                                    out_a.at[tile, pl.ds(r0, n)], sems.at[n_sem]))
                n_sem += 1
            if r1 > seq:
                s0 = max(r0, seq)
                pieces.append(pltpu.make_async_copy(acc.at[pl.ds(s0, r1 - s0)],
                                                    out_b.at[tile, pl.ds(s0 - seq, r1 - s0)],
                                                    sems.at[n_sem]))
                n_sem += 1
        per_chunk.append(pieces)
    return per_chunk


def _ffn_ln_kernel(x_ref, wg_ref, wu_ref, wd_ref, g_ref, b_ref, out_a, out_b, acc, *scratch,
                   alpha, seq, final):
    stage, sems = (None, scratch[0]) if final else scratch
    f = pl.program_id(1)

    @pl.when(f == 0)
    def _():
        acc[...] = jnp.zeros_like(acc)

    x = x_ref[...].astype(BF16)
    gate = _bdot(x, wg_ref[...])
    up = _bdot(x, wu_ref[...])
    hid = (gate * _sigmoid(gate) * up).astype(BF16)
    acc[...] += _bdot(hid, wd_ref[...])

    @pl.when(f == pl.num_programs(1) - 1)
    def _():
        per_chunk = _ffn_row_copies(acc, stage, out_a, out_b, pl.program_id(0), sems, seq, final)
        waited = set()
        for i, pieces in enumerate(per_chunk):
            r = i * LN_ROWS
            if not final and i >= STAGE_SLOTS:
                per_chunk[i - STAGE_SLOTS][1].wait()
                waited.add((i - STAGE_SLOTS, 1))
            y = _residual_ln_chunk(x_ref[r:r + LN_ROWS, :], acc, g_ref, b_ref, alpha, r)
            if not final:
                stage[i % STAGE_SLOTS] = y.astype(BF16)
            for cp in pieces:
                cp.start()
        for i, pieces in enumerate(per_chunk):
            for j, cp in enumerate(pieces):
                if (i, j) not in waited:
                    cp.wait()


def _ffn_ln(x, w_up, w_down, g, b, layer, alpha, tm, tf, seq, final):
    m_rows, d = x.shape
    batch = m_rows // tm
    d_ff = w_down.shape[1]
    nf = d_ff // tf
    n_chunks = tm // LN_ROWS
    vec = lambda: pl.BlockSpec((None, 1, d), lambda m, f: (layer, 0, 0))
    if final:
        out_shape = [jax.ShapeDtypeStruct((batch, seq, d), F32),
                     jax.ShapeDtypeStruct((batch, tm - seq, d), F32)]
        scratch = [pltpu.VMEM((tm, d), F32), pltpu.SemaphoreType.DMA((2 * n_chunks,))]
    else:
        out_shape = [jax.ShapeDtypeStruct((m_rows, d), F32), jax.ShapeDtypeStruct((m_rows, d), BF16)]
        scratch = [pltpu.VMEM((tm, d), F32), pltpu.VMEM((STAGE_SLOTS, LN_ROWS, d), BF16),
                   pltpu.SemaphoreType.DMA((2 * n_chunks,))]
    return pl.pallas_call(
        functools.partial(_ffn_ln_kernel, alpha=alpha, seq=seq, final=final),
        grid=(batch, nf),
        in_specs=[
            pl.BlockSpec((tm, d), lambda m, f: (m, 0), pipeline_mode=_ONCE),
            pl.BlockSpec((None, d, tf), lambda m, f: (layer, 0, f)),
            pl.BlockSpec((None, d, tf), lambda m, f: (layer, 0, nf + f)),
            pl.BlockSpec((None, tf, d), lambda m, f: (layer, f, 0)),
            vec(), vec(),
        ],
        out_specs=[pl.BlockSpec(memory_space=pl.ANY), pl.BlockSpec(memory_space=pl.ANY)],
        out_shape=out_shape,
        scratch_shapes=scratch,
        compiler_params=_cparams(("arbitrary", "arbitrary"), 60),
        name="ffn_ln",
    )(x, w_up, w_up, w_down, g, b)


def _state_shift_kernel(s_ref, *refs):
    new_refs, o_ref = refs[:-1], refs[-1]
    layer = pl.program_id(0)
    keep = s_ref.shape[0] - 1
    new = new_refs[0][...]
    for j in range(1, len(new_refs)):
        new = jnp.where(layer == j, new_refs[j][...], new)
    o_ref[0:keep] = s_ref[1:keep + 1]
    o_ref[keep] = new


def _state_shift(state_t, new_rows3, seq):
    depth, buf, _, d_mix = state_t.shape
    batch, tile_rows, _ = new_rows3[0].shape
    rows_s = tile_rows - seq
    blk = pl.BlockSpec((None, buf, rows_s, d_mix), lambda l, b: (l, 0, b, 0))
    new_spec = pl.BlockSpec((None, rows_s, d_mix), lambda l, b: (b, seq // rows_s, 0))
    return pl.pallas_call(
        _state_shift_kernel,
        grid=(depth, batch),
        in_specs=[blk] + [new_spec] * depth,
        out_specs=blk,
        out_shape=jax.ShapeDtypeStruct(state_t.shape, state_t.dtype),
        compiler_params=_cparams(("parallel", "parallel"), 32),
        name="state_shift",
    )(state_t, *new_rows3)


def kernel(x_prompt, x_sample, state_pool, state_conv, w_in, b_in, a_ln_g, a_ln_b, a_ws, a_bs, w_a_out, b_w_group, b_scale, w_b_out, c_w_dw, c_b_dw, c_ln_g, c_ln_b, w_c_out, w_out, ln1_g, ln1_b, w_ffn_up, w_ffn_down, ln2_g, ln2_b):
    batch, seq, d_model = x_prompt.shape
    m_sample = x_sample.shape[0] * x_sample.shape[1]
    depth = w_in.shape[0]
    d_mix = a_ln_g.shape[-1]
    head_dim = d_mix // A_HEADS
    alpha = (2.0 * depth) ** 0.25

    assert x_sample.shape[1] == 1 and m_sample % batch == 0
    rows_s = m_sample // batch
    tile_rows = seq + rows_s
    m_rows = batch * tile_rows
    half_rows = tile_rows // 2
    assert rows_s % BF16_ROWS == 0 and seq % rows_s == 0 and seq % MIX_SUB == 0
    assert seq % PACK_SUB == 0
    assert tile_rows % 2 == 0 and half_rows % LN_ROWS == 0 and LN_ROWS % BF16_ROWS == 0
    assert rows_s <= LN_ROWS and (LN_ROWS - rows_s) % SUBLANES == 0
    assert tile_rows % (ROW_SPLITS * BF16_ROWS) == 0 and rows_s <= tile_rows // ROW_SPLITS

    vec3 = lambda a: a.reshape(depth, 1, a.shape[-1])
    lw = {
        "a_ln_g": vec3(a_ln_g), "a_ln_b": vec3(a_ln_b),
        "a_ws": a_ws,
        "a_bs_b": jnp.broadcast_to(a_bs[..., None], a_bs.shape + (head_dim,)),
        "a_ws0": jnp.repeat(a_ws[:, :, 0, 0], head_dim, axis=-1).reshape(depth, 1, d_mix),
        "a_bs0": jnp.repeat(a_bs[:, :, 0], head_dim, axis=-1).reshape(depth, 1, d_mix),
        "b_w_group": b_w_group, "b_scale": vec3(b_scale),
        "c_w_dw": c_w_dw, "c_b_dw": vec3(c_b_dw),
        "c_ln_g": vec3(c_ln_g), "c_ln_b": vec3(c_ln_b),
    }
    b_in3 = vec3(b_in)
    ln1_g3, ln1_b3, ln2_g3, ln2_b3 = vec3(ln1_g), vec3(ln1_b), vec3(ln2_g), vec3(ln2_b)

    s1 = 2 * d_mix
    s2 = s1 + d_mix
    s3 = s2 + 2 * d_mix

    x_sample3 = x_sample.reshape(batch, rows_s, d_model)
    x_sources = (x_prompt, x_sample3)
    xb = _pack_bf16(x_prompt, x_sample3).reshape(m_rows, d_model)
    tile3 = lambda a: a.reshape(batch, tile_rows, a.shape[-1])
    pool_t = jnp.transpose(state_pool, (0, 2, 1, 3))
    conv_t = jnp.transpose(state_conv, (0, 2, 1, 3))

    hb_l, c_l, v_l = [], [], []
    for l in range(depth):
        u = _proj(xb, w_in, b_in3, l, 0, d_mix, "gelu", BF16, tile_rows, d_mix, "proj_u")
        v, v_new = _proj_gelu_ln(xb, w_in, b_in3, lw["a_ln_g"], lw["a_ln_b"], l, d_mix, d_mix,
                                 tile_rows, rows_s, "proj_v")
        hb = _proj(xb, w_in, b_in3, l, s1, d_mix, "none", F32, tile_rows, d_mix, "proj_b")
        c = _glu_proj(xb, w_in, b_in3, l, s2, d_mix, tile_rows, 512, "proj_c")
        gates = _proj_sigmoid_blocked(xb, w_in, b_in3, l, s3, N_BRANCH * d_model,
                                      tile_rows, 1024, 256, "proj_gates")

        pa, pb, pc = _mix(tile3(u), tile3(v), tile3(hb), tile3(c), pool_t, conv_t, lw, l, seq)
        flat = lambda a: a.reshape(m_rows, d_mix)
        merged = _merge(flat(pa), flat(pb), flat(pc), gates, w_a_out, w_b_out, w_c_out,
                        l, tile_rows, 256)
        x1 = _projout_ln(merged, w_out, x_sources, ln1_g3, ln1_b3, l, alpha, half_rows, 512)
        hb_l.append(tile3(hb))
        c_l.append(tile3(c))
        v_l.append(v_new)
        final = l + 1 == depth
        out_a, out_b = _ffn_ln(x1, w_ffn_up, w_ffn_down, ln2_g3, ln2_b3, l, alpha,
                               tile_rows, 256, seq, final)
        if final:
            y_prompt, y_sample3 = out_a, out_b
        else:
            x_sources, xb = (out_a,), out_b

    y_sample = y_sample3.reshape(m_sample, 1, d_model)
    new_pool_prompt = jnp.stack([a[:, seq - POOL_BUF:seq] for a in hb_l])
    new_conv_prompt = jnp.stack([a[:, seq - CONV_BUF:seq] for a in c_l])
    new_pool_sample = jnp.transpose(_state_shift(pool_t, hb_l, seq), (0, 2, 1, 3))
    new_conv_sample = jnp.transpose(_state_shift(conv_t, c_l, seq), (0, 2, 1, 3))
    new_chunk_v = jnp.stack(v_l)[:, :, None, :]
    return (y_prompt, y_sample, new_pool_prompt, new_conv_prompt,
            new_pool_sample, new_conv_sample, new_chunk_v)
```

```python
import functools

import jax
import jax.numpy as jnp
from jax import lax
from jax.experimental import pallas as pl
from jax.experimental.pallas import tpu as pltpu

F32 = jnp.float32
BF16 = jnp.bfloat16

LN_EPS = 1e-5
A_HEADS = 8
CHUNK = 128
POOL_WINDOWS = (2, 4, 8, 16)
POOL_BUF = 15
CONV_WIDTH = 31
CONV_BUF = CONV_WIDTH - 1
N_BRANCH = 3

SUBLANES = 8
LANES = 128
BF16_ROWS = 16
POOL_HALO = 16
CONV_HALO = 32
ROW_BLOCK = 64
MIX_SUB = 256
PACK_SUB = 512
LN_ROWS = 208
ROW_SPLITS = 10
FFN_ROW_SPLITS = 2
STAGE_SLOTS = 2
MERGE_COLS = 256
V7X_VMEM_BYTES = 64 * 1024 * 1024

_ONCE = pl.Buffered(1)


def _cparams(semantics, vmem_mb):
    assert vmem_mb * 1024 * 1024 < V7X_VMEM_BYTES
    return pltpu.CompilerParams(dimension_semantics=semantics,
                                vmem_limit_bytes=vmem_mb * 1024 * 1024)


def _layer_norm(x, g, b):
    mu = jnp.mean(x, axis=-1, keepdims=True)
    xc = x - mu
    var = jnp.mean(xc * xc, axis=-1, keepdims=True)
    return xc * lax.rsqrt(var + LN_EPS) * g + b


def _gelu_exact(x):
    return 0.5 * x * (1.0 + lax.erf(x * (0.5 ** 0.5)))


def _sigmoid(x):
    return 1.0 / (1.0 + jnp.exp(-x))


def _bdot(a, w):
    return jnp.dot(a, w.astype(BF16), preferred_element_type=F32)


def _residual_ln_chunk(x_rows, acc_ref, g_ref, b_ref, alpha, r):
    y = _layer_norm(alpha * x_rows + acc_ref[pl.ds(r, LN_ROWS), :], g_ref[...], b_ref[...])
    acc_ref[pl.ds(r, LN_ROWS), :] = y
    return y


def _residual_ln_rows(x_ref, acc_ref, g_ref, b_ref, alpha):
    def body(i, carry):
        r = pl.multiple_of(i * LN_ROWS, LN_ROWS)
        _residual_ln_chunk(x_ref[pl.ds(r, LN_ROWS), :], acc_ref, g_ref, b_ref, alpha, r)
        return carry

    lax.fori_loop(0, acc_ref.shape[0] // LN_ROWS, body, 0)


def _pack_kernel(xp_ref, xs_ref, o_ref, *, n_sub):
    t = pl.program_id(1)
    sub = xp_ref.shape[0]

    @pl.when(t < n_sub)
    def _():
        o_ref[pl.ds(pl.multiple_of(t * sub, sub), sub), :] = xp_ref[...].astype(BF16)

    @pl.when(t == n_sub)
    def _():
        o_ref[n_sub * sub:, :] = xs_ref[...].astype(BF16)


def _pack_bf16(x_prompt, x_sample3):
    batch, seq, d = x_prompt.shape
    rows_s = x_sample3.shape[1]
    n_sub = seq // PACK_SUB
    return pl.pallas_call(
        functools.partial(_pack_kernel, n_sub=n_sub),
        grid=(batch, n_sub + 1),
        in_specs=[
            pl.BlockSpec((None, PACK_SUB, d), lambda b, t: (b, jnp.minimum(t, n_sub - 1), 0)),
            pl.BlockSpec((None, rows_s, d), lambda b, t: (b, 0, 0)),
        ],
        out_specs=pl.BlockSpec((None, seq + rows_s, d), lambda b, t: (b, 0, 0)),
        out_shape=jax.ShapeDtypeStruct((batch, seq + rows_s, d), BF16),
        compiler_params=_cparams(("parallel", "arbitrary"), 40),
        name="pack",
    )(x_prompt, x_sample3)


def _proj_kernel(x_ref, w_ref, b_ref, o_ref, *, act):
    w = w_ref[...].astype(BF16)
    rows = x_ref.shape[0] // ROW_SPLITS
    for i in range(ROW_SPLITS):
        rs = slice(i * rows, (i + 1) * rows)
        h = jnp.dot(x_ref[rs, :], w, preferred_element_type=F32) + b_ref[...]
        if act == "gelu":
            h = _gelu_exact(h)
        elif act == "sigmoid":
            h = _sigmoid(h)
        o_ref[rs, :] = h.astype(o_ref.dtype)


def _proj_sigmoid_blocked_kernel(x_ref, w_ref, b_ref, o_ref):
    w = w_ref[...].astype(BF16)
    n_blk, tile_rows, sub = o_ref.shape
    rows = tile_rows // ROW_SPLITS
    for i in range(ROW_SPLITS):
        rs = slice(i * rows, (i + 1) * rows)
        h = _sigmoid(jnp.dot(x_ref[rs, :], w, preferred_element_type=F32) + b_ref[...])
        for j in range(n_blk):
            o_ref[j, rs, :] = h[:, j * sub:(j + 1) * sub].astype(o_ref.dtype)


def _proj_gelu_ln_kernel(x_ref, w_ref, b_ref, g_ref, beta_ref, o_ref, os_ref):
    w = w_ref[...].astype(BF16)
    rows = x_ref.shape[0] // ROW_SPLITS
    n_s = os_ref.shape[0]
    for i in range(ROW_SPLITS):
        rs = slice(i * rows, (i + 1) * rows)
        h = jnp.dot(x_ref[rs, :], w, preferred_element_type=F32) + b_ref[...]
        v = _layer_norm(_gelu_exact(h), g_ref[...], beta_ref[...])
        o_ref[rs, :] = v.astype(BF16)
        if i == ROW_SPLITS - 1:
            os_ref[...] = v[rows - n_s:, :]


def _glu_kernel(x_ref, wv_ref, wg_ref, bv_ref, bg_ref, o_ref):
    wv = wv_ref[...].astype(BF16)
    wg = wg_ref[...].astype(BF16)
    rows = x_ref.shape[0] // ROW_SPLITS
    for i in range(ROW_SPLITS):
        rs = slice(i * rows, (i + 1) * rows)
        x = x_ref[rs, :]
        val = jnp.dot(x, wv, preferred_element_type=F32) + bv_ref[...]
        gate = jnp.dot(x, wg, preferred_element_type=F32) + bg_ref[...]
        o_ref[rs, :] = val * _sigmoid(gate)


def _proj(x, w, b, layer, col0, ncols, act, out_dtype, tm, tn, name):
    m_rows, k = x.shape
    off = col0 // tn
    return pl.pallas_call(
        functools.partial(_proj_kernel, act=act),
        grid=(m_rows // tm, ncols // tn),
        in_specs=[
            pl.BlockSpec((tm, k), lambda m, n: (m, 0)),
            pl.BlockSpec((None, k, tn), lambda m, n: (layer, 0, off + n)),
            pl.BlockSpec((None, 1, tn), lambda m, n: (layer, 0, off + n)),
        ],
        out_specs=pl.BlockSpec((tm, tn), lambda m, n: (m, n)),
        out_shape=jax.ShapeDtypeStruct((m_rows, ncols), out_dtype),
        compiler_params=_cparams(("parallel", "arbitrary"), 56),
        name=name,
    )(x, w, b)


def _proj_sigmoid_blocked(x, w, b, layer, col0, ncols, tm, tn, sub, name):
    m_rows, k = x.shape
    off = col0 // tn
    return pl.pallas_call(
        _proj_sigmoid_blocked_kernel,
        grid=(m_rows // tm, ncols // tn),
        in_specs=[
            pl.BlockSpec((tm, k), lambda m, n: (m, 0)),
            pl.BlockSpec((None, k, tn), lambda m, n: (layer, 0, off + n)),
            pl.BlockSpec((None, 1, tn), lambda m, n: (layer, 0, off + n)),
        ],
        out_specs=pl.BlockSpec((tn // sub, tm, sub), lambda m, n: (n, m, 0)),
        out_shape=jax.ShapeDtypeStruct((ncols // sub, m_rows, sub), BF16),
        compiler_params=_cparams(("parallel", "arbitrary"), 56),
        name=name,
    )(x, w, b)


def _proj_gelu_ln(x, w, b, g, beta, layer, col0, ncols, tm, rows_s, name):
    m_rows, k = x.shape
    off = col0 // ncols
    vec = lambda: pl.BlockSpec((None, 1, ncols), lambda m: (layer, 0, 0))
    return pl.pallas_call(
        _proj_gelu_ln_kernel,
        grid=(m_rows // tm,),
        in_specs=[
            pl.BlockSpec((tm, k), lambda m: (m, 0)),
            pl.BlockSpec((None, k, ncols), lambda m: (layer, 0, off)),
            pl.BlockSpec((None, 1, ncols), lambda m: (layer, 0, off)),
            vec(), vec(),
        ],
        out_specs=[pl.BlockSpec((tm, ncols), lambda m: (m, 0)),
                   pl.BlockSpec((rows_s, ncols), lambda m: (m, 0))],
        out_shape=[jax.ShapeDtypeStruct((m_rows, ncols), BF16),
                   jax.ShapeDtypeStruct((m_rows // tm * rows_s, ncols), F32)],
        compiler_params=_cparams(("parallel",), 52),
        name=name,
    )(x, w, b, g, beta)


def _glu_proj(x, w, b, layer, col0, ncols, tm, tn, name):
    m_rows, k = x.shape
    off_v = col0 // tn
    off_g = (col0 + ncols) // tn
    return pl.pallas_call(
        _glu_kernel,
        grid=(m_rows // tm, ncols // tn),
        in_specs=[
            pl.BlockSpec((tm, k), lambda m, n: (m, 0)),
            pl.BlockSpec((None, k, tn), lambda m, n: (layer, 0, off_v + n)),
            pl.BlockSpec((None, k, tn), lambda m, n: (layer, 0, off_g + n)),
            pl.BlockSpec((None, 1, tn), lambda m, n: (layer, 0, off_v + n)),
            pl.BlockSpec((None, 1, tn), lambda m, n: (layer, 0, off_g + n)),
        ],
        out_specs=pl.BlockSpec((tm, tn), lambda m, n: (m, n)),
        out_shape=jax.ShapeDtypeStruct((m_rows, ncols), F32),
        compiler_params=_cparams(("parallel", "arbitrary"), 56),
        name=name,
    )(x, w, w, b, b)


def _trailing_sum(x, w):
    result, offset, part, span = None, 0, x, 1
    while True:
        if w & span:
            term = part if offset == 0 else pltpu.roll(part, offset, axis=0)
            result = term if result is None else result + term
            offset += span
        if span * 2 > w:
            return result
        part = part + pltpu.roll(part, span, axis=0)
        span *= 2


def _conv_rows(x, wdw_ref, bias, ls):
    lead = CONV_HALO - CONV_BUF
    acc = jnp.broadcast_to(bias, (ROW_BLOCK, LANES))
    for r in range(SUBLANES):
        taps = [k for k in range(CONV_WIDTH) if (lead + k) % SUBLANES == r]
        if not taps:
            continue
        xr = x if r == 0 else pltpu.roll(x, x.shape[0] - r, axis=0)
        for k in taps:
            q = (lead + k) // SUBLANES
            acc = acc + wdw_ref[k:k + 1, ls] * xr[SUBLANES * q:SUBLANES * q + ROW_BLOCK]
    return acc


def _mix_prompt_step(t, u_ref, v_ref, hb_ref, hbh_ref, c_ref, ch_ref,
                     ws_ref, bsb_ref, wgrp_ref, bsc_ref,
                     wdw_ref, bdw_ref, clg_ref, clb_ref,
                     pa_ref, pb_ref, pc_ref, hb_ext, c_ext, pool_scr, conv_scr):
    sub, d_mix = v_ref.shape
    n_chunks = sub // CHUNK
    head_dim = d_mix // A_HEADS
    group_dim = d_mix // len(POOL_WINDOWS)
    base = pl.multiple_of(t * sub, sub)

    vb = v_ref[...]
    row = lax.broadcasted_iota(jnp.int32, (CHUNK, CHUNK), 0)
    col = lax.broadcasted_iota(jnp.int32, (CHUNK, CHUNK), 1)
    for h in range(A_HEADS):
        hs = slice(h * head_dim, (h + 1) * head_dim)
        wm = jnp.where(row >= col, ws_ref[h], 0.0).astype(BF16)
        rhs = jnp.concatenate(
            [vb[ci * CHUNK:(ci + 1) * CHUNK, hs] for ci in range(n_chunks)], axis=1)
        s = jnp.dot(wm, rhs, preferred_element_type=F32)
        bias = bsb_ref[h]
        for ci in range(n_chunks):
            rs = slice(ci * CHUNK, (ci + 1) * CHUNK)
            s_c = s[:, ci * head_dim:(ci + 1) * head_dim] + bias
            pa_ref[pl.ds(base + ci * CHUNK, CHUNK), hs] = (u_ref[rs, hs] * s_c).astype(BF16)

    first = t == 0
    hb_ext[0:POOL_HALO, :] = jnp.where(first, 0.0, hbh_ref[...])
    hb_ext[POOL_HALO:POOL_HALO + sub, :] = hb_ref[...]

    def pool_body(i, carry):
        r0 = pl.multiple_of(i * ROW_BLOCK, ROW_BLOCK)
        pos = base + r0 + lax.broadcasted_iota(jnp.int32, (ROW_BLOCK, group_dim), 0)
        for g, w in enumerate(POOL_WINDOWS):
            gs = slice(g * group_dim, (g + 1) * group_dim)
            x = hb_ext[pl.ds(r0, POOL_HALO + ROW_BLOCK), gs]
            tok = x[POOL_HALO:POOL_HALO + ROW_BLOCK]
            win = _trailing_sum(x, w)[POOL_HALO:POOL_HALO + ROW_BLOCK]
            cnt = jnp.minimum(pos + 1, w).astype(F32)
            pool_scr[pl.ds(r0, ROW_BLOCK), gs] = win / cnt - tok
        return carry

    lax.fori_loop(0, sub // ROW_BLOCK, pool_body, 0)
    for g in range(len(POOL_WINDOWS)):
        gs = slice(g * group_dim, (g + 1) * group_dim)
        mixed = _bdot(pool_scr[:, gs].astype(BF16), wgrp_ref[g])
        pb_ref[pl.ds(base, sub), gs] = (mixed * bsc_ref[:, gs]).astype(BF16)

    c_ext[0:CONV_HALO, :] = jnp.where(first, 0.0, ch_ref[...])
    c_ext[CONV_HALO:CONV_HALO + sub, :] = c_ref[...]

    def conv_body(i, carry):
        r0 = pl.multiple_of(i * ROW_BLOCK, ROW_BLOCK)
        for lt in range(d_mix // LANES):
            ls = slice(lt * LANES, (lt + 1) * LANES)
            window = c_ext[pl.ds(r0, CONV_HALO + ROW_BLOCK), ls]
            conv_scr[pl.ds(r0, ROW_BLOCK), ls] = _conv_rows(window, wdw_ref, bdw_ref[:, ls], ls)
        return carry

    lax.fori_loop(0, sub // ROW_BLOCK, conv_body, 0)
    y = _layer_norm(conv_scr[...], clg_ref[...], clb_ref[...])
    pc_ref[pl.ds(base, sub), :] = (y * _sigmoid(y)).astype(BF16)


def _mix_sample_step(row0, u_ref, v_ref, hb_ref, c_ref, sp_ref, sc_ref,
                     ws0_ref, bs0_ref, wgrp_ref, bsc_ref,
                     wdw_ref, bdw_ref, clg_ref, clb_ref,
                     pa_ref, pb_ref, pc_ref):
    rows, d_mix = v_ref.shape
    group_dim = d_mix // len(POOL_WINDOWS)
    out_rows = slice(row0, row0 + rows)

    s = v_ref[...].astype(F32) * ws0_ref[...].astype(BF16).astype(F32) + bs0_ref[...]
    pa_ref[out_rows, :] = (u_ref[...] * s).astype(BF16)

    for g, w in enumerate(POOL_WINDOWS):
        gs = slice(g * group_dim, (g + 1) * group_dim)
        tok = hb_ref[:, gs]
        win = tok
        for k in range(POOL_BUF - (w - 1), POOL_BUF):
            win = win + sp_ref[k, :, gs]
        pooled = win / float(w) - tok
        mixed = _bdot(pooled.astype(BF16), wgrp_ref[g])
        pb_ref[out_rows, gs] = (mixed * bsc_ref[:, gs]).astype(BF16)

    conv = c_ref[...] * wdw_ref[CONV_BUF:CONV_WIDTH, :] + bdw_ref[...]
    for k in range(CONV_BUF):
        conv = conv + sc_ref[k] * wdw_ref[k:k + 1, :]
    y = _layer_norm(conv, clg_ref[...], clb_ref[...])
    pc_ref[out_rows, :] = (y * _sigmoid(y)).astype(BF16)


def _mix_kernel(u_ref, v_ref, hb_ref, hbh_ref, c_ref, ch_ref,
                us_ref, vs_ref, hbs_ref, cs_ref, sp_ref, sc_ref,
                ws_ref, bsb_ref, ws0_ref, bs0_ref, wgrp_ref, bsc_ref,
                wdw_ref, bdw_ref, clg_ref, clb_ref,
                pa_ref, pb_ref, pc_ref,
                hb_ext, c_ext, pool_scr, conv_scr, *, n_sub):
    t = pl.program_id(1)

    @pl.when(t < n_sub)
    def _():
        _mix_prompt_step(t, u_ref, v_ref, hb_ref, hbh_ref, c_ref, ch_ref,
                         ws_ref, bsb_ref, wgrp_ref, bsc_ref,
                         wdw_ref, bdw_ref, clg_ref, clb_ref,
                         pa_ref, pb_ref, pc_ref, hb_ext, c_ext, pool_scr, conv_scr)

    @pl.when(t == n_sub)
    def _():
        _mix_sample_step(n_sub * v_ref.shape[0], us_ref, vs_ref, hbs_ref, cs_ref, sp_ref, sc_ref,
                         ws0_ref, bs0_ref, wgrp_ref, bsc_ref,
                         wdw_ref, bdw_ref, clg_ref, clb_ref,
                         pa_ref, pb_ref, pc_ref)


def _mix(u3, v3, hb3, c3, state_pool, state_conv, lw, layer, seq):
    batch, tile_rows, d_mix = hb3.shape
    rows_s = tile_rows - seq
    n_sub = seq // MIX_SUB
    sub_idx = lambda t: jnp.minimum(t, n_sub - 1)
    vec = lambda: pl.BlockSpec((None, 1, d_mix), lambda b, t: (layer, 0, 0))
    sub_spec = lambda: pl.BlockSpec((None, MIX_SUB, d_mix), lambda b, t: (b, sub_idx(t), 0))
    smp_spec = lambda: pl.BlockSpec((None, rows_s, d_mix), lambda b, t: (b, seq // rows_s, 0))

    def halo_spec(rows):
        per = MIX_SUB // rows
        return pl.BlockSpec((None, rows, d_mix),
                            lambda b, t: (b, jnp.maximum(sub_idx(t) * per - 1, 0), 0))

    def state_spec(buf):
        return pl.BlockSpec((None, buf, rows_s, d_mix), lambda b, t: (layer, 0, b, 0),
                            pipeline_mode=_ONCE)

    head_spec = lambda: pl.BlockSpec((None, A_HEADS, CHUNK, CHUNK), lambda b, t: (layer, 0, 0, 0))
    tile_out = pl.BlockSpec((None, tile_rows, d_mix), lambda b, t: (b, 0, 0))
    out3 = jax.ShapeDtypeStruct((batch, tile_rows, d_mix), BF16)
    return pl.pallas_call(
        functools.partial(_mix_kernel, n_sub=n_sub),
        grid=(batch, n_sub + 1),
        in_specs=[
            sub_spec(), sub_spec(),
            sub_spec(), halo_spec(POOL_HALO),
            sub_spec(), halo_spec(CONV_HALO),
            smp_spec(), smp_spec(), smp_spec(), smp_spec(),
            state_spec(POOL_BUF), state_spec(CONV_BUF),
            head_spec(), head_spec(),
            vec(), vec(),
            pl.BlockSpec((None,) + lw["b_w_group"].shape[1:], lambda b, t: (layer, 0, 0, 0)),
            vec(),
            pl.BlockSpec((None, CONV_WIDTH, d_mix), lambda b, t: (layer, 0, 0)),
            vec(), vec(), vec(),
        ],
        out_specs=[tile_out, tile_out, tile_out],
        out_shape=[out3, out3, out3],
        scratch_shapes=[
            pltpu.VMEM((POOL_HALO + MIX_SUB, d_mix), F32),
            pltpu.VMEM((CONV_HALO + MIX_SUB, d_mix), F32),
            pltpu.VMEM((MIX_SUB, d_mix), F32),
            pltpu.VMEM((MIX_SUB, d_mix), F32),
        ],
        compiler_params=_cparams(("parallel", "arbitrary"), 56),
        name="mix",
    )(u3, v3, hb3, hb3, c3, c3, u3, v3, hb3, c3, state_pool, state_conv,
      lw["a_ws"], lw["a_bs_b"], lw["a_ws0"], lw["a_bs0"],
      lw["b_w_group"], lw["b_scale"], lw["c_w_dw"], lw["c_b_dw"], lw["c_ln_g"], lw["c_ln_b"])


def _merge_kernel(pa_ref, pb_ref, pc_ref, wa_ref, wb_ref, wc_ref,
                  g0_ref, g1_ref, g2_ref, o_ref):
    wa = wa_ref[...].astype(BF16)
    wb = wb_ref[...].astype(BF16)
    wc = wc_ref[...].astype(BF16)
    rows = o_ref.shape[0] // ROW_SPLITS
    for i in range(ROW_SPLITS):
        rs = slice(i * rows, (i + 1) * rows)
        merged = (g0_ref[rs, :] * jnp.dot(pa_ref[rs, :], wa, preferred_element_type=F32)
                  + g1_ref[rs, :] * jnp.dot(pb_ref[rs, :], wb, preferred_element_type=F32)
                  + g2_ref[rs, :] * jnp.dot(pc_ref[rs, :], wc, preferred_element_type=F32))
        o_ref[rs, :] = merged.astype(o_ref.dtype)


def _merge(pa, pb, pc, gates_blocked, w_a, w_b, w_c, layer, tm, tn):
    m_rows, d_mix = pa.shape
    d_model = w_a.shape[-1]
    nb = d_model // tn
    assert gates_blocked.shape == (N_BRANCH * nb, m_rows, tn)
    act = lambda: pl.BlockSpec((tm, d_mix), lambda m, n: (m, 0))
    wsp = lambda: pl.BlockSpec((None, d_mix, tn), lambda m, n: (layer, 0, n))
    gsp = lambda j: pl.BlockSpec((None, tm, tn), lambda m, n: (j * nb + n, m, 0))
    return pl.pallas_call(
        _merge_kernel,
        grid=(m_rows // tm, nb),
        in_specs=[act(), act(), act(), wsp(), wsp(), wsp(), gsp(0), gsp(1), gsp(2)],
        out_specs=pl.BlockSpec((tm, tn), lambda m, n: (m, n)),
        out_shape=jax.ShapeDtypeStruct((m_rows, d_model), BF16),
        compiler_params=_cparams(("parallel", "arbitrary"), 52),
        name="merge",
    )(pa, pb, pc, w_a, w_b, w_c, gates_blocked, gates_blocked, gates_blocked)


def _projout_accumulate(m_ref, w_ref, o_ref):
    @pl.when(pl.program_id(2) == 0)
    def _():
        o_ref[...] = jnp.zeros_like(o_ref)

    o_ref[...] += _bdot(m_ref[...], w_ref[...])


def _residual_copies(x_hbm, xbuf, sems, tile, half):
    tm = xbuf.shape[0]
    if len(x_hbm) == 1:
        row0 = pl.multiple_of((tile * 2 + half) * tm, tm)
        return [pltpu.make_async_copy(x_hbm[0].at[pl.ds(row0, tm)], xbuf, sems.at[0])]
    xp, xs = x_hbm
    if half == 0:
        return [pltpu.make_async_copy(xp.at[tile, pl.ds(0, tm)], xbuf, sems.at[0])]
    n_prompt = xp.shape[1] - tm
    return [pltpu.make_async_copy(xp.at[tile, pl.ds(tm, n_prompt)],
                                  xbuf.at[pl.ds(0, n_prompt)], sems.at[1]),
            pltpu.make_async_copy(xs.at[tile], xbuf.at[pl.ds(n_prompt, xs.shape[1])], sems.at[2])]


def _projout_ln_kernel(m_ref, w_ref, *refs, alpha, n_x):
    x_hbm, (g_ref, b_ref, o_ref, xbuf, sems) = refs[:n_x], refs[n_x:]
    tile, h, k = pl.program_id(0), pl.program_id(1), pl.program_id(2)
    for half in range(2):
        @pl.when(jnp.logical_and(k == 0, h == half))
        def _():
            for cp in _residual_copies(x_hbm, xbuf, sems, tile, half):
                cp.start()

    _projout_accumulate(m_ref, w_ref, o_ref)

    for half in range(2):
        @pl.when(jnp.logical_and(k == pl.num_programs(2) - 1, h == half))
        def _():
            for cp in _residual_copies(x_hbm, xbuf, sems, tile, half):
                cp.wait()
            _residual_ln_rows(xbuf, o_ref, g_ref, b_ref, alpha)


def _projout_ln(merged, w_out, x_sources, g, b, layer, alpha, tm, tk):
    m_rows = merged.shape[0]
    d = w_out.shape[-1]
    batch = m_rows // (2 * tm)
    vec = lambda: pl.BlockSpec((None, 1, d), lambda bb, h, k: (layer, 0, 0))
    row_idx = lambda bb, h: bb * 2 + h
    return pl.pallas_call(
        functools.partial(_projout_ln_kernel, alpha=alpha, n_x=len(x_sources)),
        grid=(batch, 2, merged.shape[1] // tk),
        in_specs=[
            pl.BlockSpec((tm, tk), lambda bb, h, k: (row_idx(bb, h), k)),
            pl.BlockSpec((None, tk, d), lambda bb, h, k: (layer, k, 0)),
            *[pl.BlockSpec(memory_space=pl.ANY) for _ in x_sources],
            vec(), vec(),
        ],
        out_specs=pl.BlockSpec((tm, d), lambda bb, h, k: (row_idx(bb, h), 0)),
        out_shape=jax.ShapeDtypeStruct((m_rows, d), F32),
        scratch_shapes=[pltpu.VMEM((tm, d), F32), pltpu.SemaphoreType.DMA((3,))],
        compiler_params=_cparams(("arbitrary", "arbitrary", "arbitrary"), 48),
        name="projout_ln",
    )(merged, w_out, *x_sources, g, b)


def _ffn_row_copies(acc, stage, out_a, out_b, tile, sems, seq, final):
    tm = acc.shape[0]
    per_chunk, n_sem = [], 0
    for i in range(tm // LN_ROWS):
        r0, r1 = i * LN_ROWS, (i + 1) * LN_ROWS
        pieces = []
        if not final:
            row0 = pl.multiple_of(tile * tm, tm)
            for src, dst in ((acc.at[pl.ds(r0, LN_ROWS)], out_a), (stage.at[i % STAGE_SLOTS], out_b)):
                pieces.append(pltpu.make_async_copy(src, dst.at[pl.ds(row0 + r0, LN_ROWS)],
                                                    sems.at[n_sem]))
                n_sem += 1
        else:
            if r0 < seq:
                n = min(r1, seq) - r0
                pieces.append(pltpu.make_async_copy(acc.at[pl.ds(r0, n)],
                                                    out_a.at[tile, pl.ds(r0, n)], sems.at[n_sem]))
                n_sem += 1
            if r1 > seq:
                s0 = max(r0, seq)
                pieces.append(pltpu.make_async_copy(acc.at[pl.ds(s0, r1 - s0)],
                                                    out_b.at[tile, pl.ds(s0 - seq, r1 - s0)],
                                                    sems.at[n_sem]))
                n_sem += 1
        per_chunk.append(pieces)
    return per_chunk


def _ffn_ln_kernel(x_ref, wg_ref, wu_ref, wd_ref, g_ref, b_ref, out_a, out_b, acc, *scratch,
                   alpha, seq, final):
    stage, sems = (None, scratch[0]) if final else scratch
    f = pl.program_id(1)

    @pl.when(f == 0)
    def _():
        acc[...] = jnp.zeros_like(acc)

    wg = wg_ref[...].astype(BF16)
    wu = wu_ref[...].astype(BF16)
    wd = wd_ref[...].astype(BF16)
    rows = acc.shape[0] // FFN_ROW_SPLITS
    for i in range(FFN_ROW_SPLITS):
        rs = slice(i * rows, (i + 1) * rows)
        x = x_ref[rs, :].astype(BF16)
        gate = jnp.dot(x, wg, preferred_element_type=F32)
        up = jnp.dot(x, wu, preferred_element_type=F32)
        hid = (gate * _sigmoid(gate) * up).astype(BF16)
        acc[rs, :] += jnp.dot(hid, wd, preferred_element_type=F32)

    @pl.when(f == pl.num_programs(1) - 1)
    def _():
        per_chunk = _ffn_row_copies(acc, stage, out_a, out_b, pl.program_id(0), sems, seq, final)
        waited = set()
        for i, pieces in enumerate(per_chunk):
            r = i * LN_ROWS
            if not final and i >= STAGE_SLOTS:
                per_chunk[i - STAGE_SLOTS][1].wait()
                waited.add((i - STAGE_SLOTS, 1))
            y = _residual_ln_chunk(x_ref[r:r + LN_ROWS, :], acc, g_ref, b_ref, alpha, r)
            if not final:
                stage[i % STAGE_SLOTS] = y.astype(BF16)
            for cp in pieces:
                cp.start()
        for i, pieces in enumerate(per_chunk):
            for j, cp in enumerate(pieces):
                if (i, j) not in waited:
                    cp.wait()


def _ffn_ln(x, w_up, w_down, g, b, layer, alpha, tm, tf, seq, final):
    m_rows, d = x.shape
    batch = m_rows // tm
    d_ff = w_down.shape[1]
    nf = d_ff // tf
    n_chunks = tm // LN_ROWS
    vec = lambda: pl.BlockSpec((None, 1, d), lambda m, f: (layer, 0, 0))
    if final:
        out_shape = [jax.ShapeDtypeStruct((batch, seq, d), F32),
                     jax.ShapeDtypeStruct((batch, tm - seq, d), F32)]
        scratch = [pltpu.VMEM((tm, d), F32), pltpu.SemaphoreType.DMA((2 * n_chunks,))]
    else:
        out_shape = [jax.ShapeDtypeStruct((m_rows, d), F32), jax.ShapeDtypeStruct((m_rows, d), BF16)]
        scratch = [pltpu.VMEM((tm, d), F32), pltpu.VMEM((STAGE_SLOTS, LN_ROWS, d), BF16),
                   pltpu.SemaphoreType.DMA((2 * n_chunks,))]
    return pl.pallas_call(
        functools.partial(_ffn_ln_kernel, alpha=alpha, seq=seq, final=final),
        grid=(batch, nf),
        in_specs=[
            pl.BlockSpec((tm, d), lambda m, f: (m, 0), pipeline_mode=_ONCE),
            pl.BlockSpec((None, d, tf), lambda m, f: (layer, 0, f)),
            pl.BlockSpec((None, d, tf), lambda m, f: (layer, 0, nf + f)),
            pl.BlockSpec((None, tf, d), lambda m, f: (layer, f, 0)),
            vec(), vec(),
        ],
        out_specs=[pl.BlockSpec(memory_space=pl.ANY), pl.BlockSpec(memory_space=pl.ANY)],
        out_shape=out_shape,
        scratch_shapes=scratch,
        compiler_params=_cparams(("arbitrary", "arbitrary"), 60),
        name="ffn_ln",
    )(x, w_up, w_up, w_down, g, b)


def _state_shift_kernel(s_ref, *refs):
    new_refs, o_ref = refs[:-1], refs[-1]
    layer = pl.program_id(0)
    keep = s_ref.shape[0] - 1
    new = new_refs[0][...]
    for j in range(1, len(new_refs)):
        new = jnp.where(layer == j, new_refs[j][...], new)
    o_ref[0:keep] = s_ref[1:keep + 1]
    o_ref[keep] = new


def _state_shift(state_t, new_rows3, seq):
    depth, buf, _, d_mix = state_t.shape
    batch, tile_rows, _ = new_rows3[0].shape
    rows_s = tile_rows - seq
    blk = pl.BlockSpec((None, buf, rows_s, d_mix), lambda l, b: (l, 0, b, 0))
    new_spec = pl.BlockSpec((None, rows_s, d_mix), lambda l, b: (b, seq // rows_s, 0))
    return pl.pallas_call(
        _state_shift_kernel,
        grid=(depth, batch),
        in_specs=[blk] + [new_spec] * depth,
        out_specs=blk,
        out_shape=jax.ShapeDtypeStruct(state_t.shape, state_t.dtype),
        compiler_params=_cparams(("parallel", "parallel"), 32),
        name="state_shift",
    )(state_t, *new_rows3)


def kernel(x_prompt, x_sample, state_pool, state_conv, w_in, b_in, a_ln_g, a_ln_b, a_ws, a_bs, w_a_out, b_w_group, b_scale, w_b_out, c_w_dw, c_b_dw, c_ln_g, c_ln_b, w_c_out, w_out, ln1_g, ln1_b, w_ffn_up, w_ffn_down, ln2_g, ln2_b):
    batch, seq, d_model = x_prompt.shape
    m_sample = x_sample.shape[0] * x_sample.shape[1]
    depth = w_in.shape[0]
    d_mix = a_ln_g.shape[-1]
    head_dim = d_mix // A_HEADS
    alpha = (2.0 * depth) ** 0.25

    assert x_sample.shape[1] == 1 and m_sample % batch == 0
    rows_s = m_sample // batch
    tile_rows = seq + rows_s
    m_rows = batch * tile_rows
    half_rows = tile_rows // 2
    assert rows_s % BF16_ROWS == 0 and seq % rows_s == 0 and seq % MIX_SUB == 0
    assert seq % PACK_SUB == 0
    assert tile_rows % 2 == 0 and half_rows % LN_ROWS == 0 and LN_ROWS % BF16_ROWS == 0
    assert rows_s <= LN_ROWS and (LN_ROWS - rows_s) % SUBLANES == 0
    assert tile_rows % (ROW_SPLITS * BF16_ROWS) == 0 and rows_s <= tile_rows // ROW_SPLITS

    vec3 = lambda a: a.reshape(depth, 1, a.shape[-1])
    lw = {
        "a_ln_g": vec3(a_ln_g), "a_ln_b": vec3(a_ln_b),
        "a_ws": a_ws,
        "a_bs_b": jnp.broadcast_to(a_bs[..., None], a_bs.shape + (head_dim,)),
        "a_ws0": jnp.repeat(a_ws[:, :, 0, 0], head_dim, axis=-1).reshape(depth, 1, d_mix),
        "a_bs0": jnp.repeat(a_bs[:, :, 0], head_dim, axis=-1).reshape(depth, 1, d_mix),
        "b_w_group": b_w_group, "b_scale": vec3(b_scale),
        "c_w_dw": c_w_dw, "c_b_dw": vec3(c_b_dw),
        "c_ln_g": vec3(c_ln_g), "c_ln_b": vec3(c_ln_b),
    }
    b_in3 = vec3(b_in)
    ln1_g3, ln1_b3, ln2_g3, ln2_b3 = vec3(ln1_g), vec3(ln1_b), vec3(ln2_g), vec3(ln2_b)

    s1 = 2 * d_mix
    s2 = s1 + d_mix
    s3 = s2 + 2 * d_mix

    x_sample3 = x_sample.reshape(batch, rows_s, d_model)
    x_sources = (x_prompt, x_sample3)
    xb = _pack_bf16(x_prompt, x_sample3).reshape(m_rows, d_model)
    tile3 = lambda a: a.reshape(batch, tile_rows, a.shape[-1])
    pool_t = jnp.transpose(state_pool, (0, 2, 1, 3))
    conv_t = jnp.transpose(state_conv, (0, 2, 1, 3))

    hb_l, c_l, v_l = [], [], []
    for l in range(depth):
        u = _proj(xb, w_in, b_in3, l, 0, d_mix, "gelu", BF16, tile_rows, d_mix, "proj_u")
        v, v_new = _proj_gelu_ln(xb, w_in, b_in3, lw["a_ln_g"], lw["a_ln_b"], l, d_mix, d_mix,
                                 tile_rows, rows_s, "proj_v")
        hb = _proj(xb, w_in, b_in3, l, s1, d_mix, "none", F32, tile_rows, d_mix, "proj_b")
        c = _glu_proj(xb, w_in, b_in3, l, s2, d_mix, tile_rows, 512, "proj_c")
        gates = _proj_sigmoid_blocked(xb, w_in, b_in3, l, s3, N_BRANCH * d_model,
                                      tile_rows, 1024, MERGE_COLS, "proj_gates")

        pa, pb, pc = _mix(tile3(u), tile3(v), tile3(hb), tile3(c), pool_t, conv_t, lw, l, seq)
        flat = lambda a: a.reshape(m_rows, d_mix)
        merged = _merge(flat(pa), flat(pb), flat(pc), gates, w_a_out, w_b_out, w_c_out,
                        l, tile_rows, MERGE_COLS)
        x1 = _projout_ln(merged, w_out, x_sources, ln1_g3, ln1_b3, l, alpha, half_rows, 512)
        hb_l.append(tile3(hb))
        c_l.append(tile3(c))
        v_l.append(v_new)
        final = l + 1 == depth
        out_a, out_b = _ffn_ln(x1, w_ffn_up, w_ffn_down, ln2_g3, ln2_b3, l, alpha,
                               tile_rows, 256, seq, final)
        if final:
            y_prompt, y_sample3 = out_a, out_b
        else:
            x_sources, xb = (out_a,), out_b

    y_sample = y_sample3.reshape(m_sample, 1, d_model)
    new_pool_prompt = jnp.stack([a[:, seq - POOL_BUF:seq] for a in hb_l])
    new_conv_prompt = jnp.stack([a[:, seq - CONV_BUF:seq] for a in c_l])
    new_pool_sample = jnp.transpose(_state_shift(pool_t, hb_l, seq), (0, 2, 1, 3))
    new_conv_sample = jnp.transpose(_state_shift(conv_t, c_l, seq), (0, 2, 1, 3))
    new_chunk_v = jnp.stack(v_l)[:, :, None, :]
    return (y_prompt, y_sample, new_pool_prompt, new_conv_prompt,
            new_pool_sample, new_conv_sample, new_chunk_v)
```

```python
import functools

import jax
import jax.numpy as jnp
from jax import lax
from jax.experimental import pallas as pl
from jax.experimental.pallas import tpu as pltpu

F32 = jnp.float32
BF16 = jnp.bfloat16

LN_EPS = 1e-5
A_HEADS = 8
CHUNK = 128
POOL_WINDOWS = (2, 4, 8, 16)
POOL_BUF = 15
CONV_WIDTH = 31
CONV_BUF = CONV_WIDTH - 1
N_BRANCH = 3

SUBLANES = 8
LANES = 128
BF16_ROWS = 16
POOL_HALO = 16
CONV_HALO = 32
ROW_BLOCK = 64
MIX_SUB = 256
PACK_SUB = 512
LN_ROWS = 208
ROW_SPLITS = 5
FFN_ROW_SPLITS = 2
STAGE_SLOTS = 2
MERGE_COLS = 256
V7X_VMEM_BYTES = 64 * 1024 * 1024

_ONCE = pl.Buffered(1)


def _cparams(semantics, vmem_mb):
    assert vmem_mb * 1024 * 1024 < V7X_VMEM_BYTES
    return pltpu.CompilerParams(dimension_semantics=semantics,
                                vmem_limit_bytes=vmem_mb * 1024 * 1024)


def _layer_norm(x, g, b):
    mu = jnp.mean(x, axis=-1, keepdims=True)
    xc = x - mu
    var = jnp.mean(xc * xc, axis=-1, keepdims=True)
    return xc * lax.rsqrt(var + LN_EPS) * g + b


def _gelu_exact(x):
    return 0.5 * x * (1.0 + lax.erf(x * (0.5 ** 0.5)))


def _sigmoid(x):
    return 1.0 / (1.0 + jnp.exp(-x))


def _bdot(a, w):
    return jnp.dot(a, w.astype(BF16), preferred_element_type=F32)


def _residual_ln_chunk(x_rows, acc_ref, g_ref, b_ref, alpha, r):
    y = _layer_norm(alpha * x_rows + acc_ref[pl.ds(r, LN_ROWS), :], g_ref[...], b_ref[...])
    acc_ref[pl.ds(r, LN_ROWS), :] = y
    return y


def _residual_ln_rows(x_ref, acc_ref, g_ref, b_ref, alpha):
    def body(i, carry):
        r = pl.multiple_of(i * LN_ROWS, LN_ROWS)
        _residual_ln_chunk(x_ref[pl.ds(r, LN_ROWS), :], acc_ref, g_ref, b_ref, alpha, r)
        return carry

    lax.fori_loop(0, acc_ref.shape[0] // LN_ROWS, body, 0)


def _pack_kernel(xp_ref, xs_ref, o_ref, *, n_sub):
    t = pl.program_id(1)
    sub = xp_ref.shape[0]

    @pl.when(t < n_sub)
    def _():
        o_ref[pl.ds(pl.multiple_of(t * sub, sub), sub), :] = xp_ref[...].astype(BF16)

    @pl.when(t == n_sub)
    def _():
        o_ref[n_sub * sub:, :] = xs_ref[...].astype(BF16)


def _pack_bf16(x_prompt, x_sample3):
    batch, seq, d = x_prompt.shape
    rows_s = x_sample3.shape[1]
    n_sub = seq // PACK_SUB
    return pl.pallas_call(
        functools.partial(_pack_kernel, n_sub=n_sub),
        grid=(batch, n_sub + 1),
        in_specs=[
            pl.BlockSpec((None, PACK_SUB, d), lambda b, t: (b, jnp.minimum(t, n_sub - 1), 0)),
            pl.BlockSpec((None, rows_s, d), lambda b, t: (b, 0, 0)),
        ],
        out_specs=pl.BlockSpec((None, seq + rows_s, d), lambda b, t: (b, 0, 0)),
        out_shape=jax.ShapeDtypeStruct((batch, seq + rows_s, d), BF16),
        compiler_params=_cparams(("parallel", "arbitrary"), 40),
        name="pack",
    )(x_prompt, x_sample3)


def _proj_kernel(x_ref, w_ref, b_ref, o_ref, *, act):
    w = w_ref[...].astype(BF16)
    rows = x_ref.shape[0] // ROW_SPLITS
    for i in range(ROW_SPLITS):
        rs = slice(i * rows, (i + 1) * rows)
        h = jnp.dot(x_ref[rs, :], w, preferred_element_type=F32) + b_ref[...]
        if act == "gelu":
            h = _gelu_exact(h)
        elif act == "sigmoid":
            h = _sigmoid(h)
        o_ref[rs, :] = h.astype(o_ref.dtype)


def _proj_sigmoid_blocked_kernel(x_ref, w_ref, b_ref, o_ref):
    w = w_ref[...].astype(BF16)
    n_blk, tile_rows, sub = o_ref.shape
    rows = tile_rows // ROW_SPLITS
    for i in range(ROW_SPLITS):
        rs = slice(i * rows, (i + 1) * rows)
        h = _sigmoid(jnp.dot(x_ref[rs, :], w, preferred_element_type=F32) + b_ref[...])
        for j in range(n_blk):
            o_ref[j, rs, :] = h[:, j * sub:(j + 1) * sub].astype(o_ref.dtype)


def _proj_gelu_ln_kernel(x_ref, w_ref, b_ref, g_ref, beta_ref, o_ref, os_ref):
    w = w_ref[...].astype(BF16)
    rows = x_ref.shape[0] // ROW_SPLITS
    n_s = os_ref.shape[0]
    for i in range(ROW_SPLITS):
        rs = slice(i * rows, (i + 1) * rows)
        h = jnp.dot(x_ref[rs, :], w, preferred_element_type=F32) + b_ref[...]
        v = _layer_norm(_gelu_exact(h), g_ref[...], beta_ref[...])
        o_ref[rs, :] = v.astype(BF16)
        if i == ROW_SPLITS - 1:
            os_ref[...] = v[rows - n_s:, :]


def _glu_kernel(x_ref, wv_ref, wg_ref, bv_ref, bg_ref, o_ref):
    wv = wv_ref[...].astype(BF16)
    wg = wg_ref[...].astype(BF16)
    rows = x_ref.shape[0] // ROW_SPLITS
    for i in range(ROW_SPLITS):
        rs = slice(i * rows, (i + 1) * rows)
        x = x_ref[rs, :]
        val = jnp.dot(x, wv, preferred_element_type=F32) + bv_ref[...]
        gate = jnp.dot(x, wg, preferred_element_type=F32) + bg_ref[...]
        o_ref[rs, :] = val * _sigmoid(gate)


def _proj(x, w, b, layer, col0, ncols, act, out_dtype, tm, tn, name):
    m_rows, k = x.shape
    off = col0 // tn
    return pl.pallas_call(
        functools.partial(_proj_kernel, act=act),
        grid=(m_rows // tm, ncols // tn),
        in_specs=[
            pl.BlockSpec((tm, k), lambda m, n: (m, 0)),
            pl.BlockSpec((None, k, tn), lambda m, n: (layer, 0, off + n)),
            pl.BlockSpec((None, 1, tn), lambda m, n: (layer, 0, off + n)),
        ],
        out_specs=pl.BlockSpec((tm, tn), lambda m, n: (m, n)),
        out_shape=jax.ShapeDtypeStruct((m_rows, ncols), out_dtype),
        compiler_params=_cparams(("parallel", "arbitrary"), 56),
        name=name,
    )(x, w, b)


def _proj_sigmoid_blocked(x, w, b, layer, col0, ncols, tm, tn, sub, name):
    m_rows, k = x.shape
    off = col0 // tn
    return pl.pallas_call(
        _proj_sigmoid_blocked_kernel,
        grid=(m_rows // tm, ncols // tn),
        in_specs=[
            pl.BlockSpec((tm, k), lambda m, n: (m, 0)),
            pl.BlockSpec((None, k, tn), lambda m, n: (layer, 0, off + n)),
            pl.BlockSpec((None, 1, tn), lambda m, n: (layer, 0, off + n)),
        ],
        out_specs=pl.BlockSpec((tn // sub, tm, sub), lambda m, n: (n, m, 0)),
        out_shape=jax.ShapeDtypeStruct((ncols // sub, m_rows, sub), BF16),
        compiler_params=_cparams(("parallel", "arbitrary"), 56),
        name=name,
    )(x, w, b)


def _proj_gelu_ln(x, w, b, g, beta, layer, col0, ncols, tm, rows_s, name):
    m_rows, k = x.shape
    off = col0 // ncols
    vec = lambda: pl.BlockSpec((None, 1, ncols), lambda m: (layer, 0, 0))
    return pl.pallas_call(
        _proj_gelu_ln_kernel,
        grid=(m_rows // tm,),
        in_specs=[
            pl.BlockSpec((tm, k), lambda m: (m, 0)),
            pl.BlockSpec((None, k, ncols), lambda m: (layer, 0, off)),
            pl.BlockSpec((None, 1, ncols), lambda m: (layer, 0, off)),
            vec(), vec(),
        ],
        out_specs=[pl.BlockSpec((tm, ncols), lambda m: (m, 0)),
                   pl.BlockSpec((rows_s, ncols), lambda m: (m, 0))],
        out_shape=[jax.ShapeDtypeStruct((m_rows, ncols), BF16),
                   jax.ShapeDtypeStruct((m_rows // tm * rows_s, ncols), F32)],
        compiler_params=_cparams(("parallel",), 52),
        name=name,
    )(x, w, b, g, beta)


def _glu_proj(x, w, b, layer, col0, ncols, tm, tn, name):
    m_rows, k = x.shape
    off_v = col0 // tn
    off_g = (col0 + ncols) // tn
    return pl.pallas_call(
        _glu_kernel,
        grid=(m_rows // tm, ncols // tn),
        in_specs=[
            pl.BlockSpec((tm, k), lambda m, n: (m, 0)),
            pl.BlockSpec((None, k, tn), lambda m, n: (layer, 0, off_v + n)),
            pl.BlockSpec((None, k, tn), lambda m, n: (layer, 0, off_g + n)),
            pl.BlockSpec((None, 1, tn), lambda m, n: (layer, 0, off_v + n)),
            pl.BlockSpec((None, 1, tn), lambda m, n: (layer, 0, off_g + n)),
        ],
        out_specs=pl.BlockSpec((tm, tn), lambda m, n: (m, n)),
        out_shape=jax.ShapeDtypeStruct((m_rows, ncols), F32),
        compiler_params=_cparams(("parallel", "arbitrary"), 56),
        name=name,
    )(x, w, w, b, b)


def _trailing_sum(x, w):
    result, offset, part, span = None, 0, x, 1
    while True:
        if w & span:
            term = part if offset == 0 else pltpu.roll(part, offset, axis=0)
            result = term if result is None else result + term
            offset += span
        if span * 2 > w:
            return result
        part = part + pltpu.roll(part, span, axis=0)
        span *= 2


def _conv_rows(x, wdw_ref, bias, ls):
    lead = CONV_HALO - CONV_BUF
    acc = jnp.broadcast_to(bias, (ROW_BLOCK, LANES))
    for r in range(SUBLANES):
        taps = [k for k in range(CONV_WIDTH) if (lead + k) % SUBLANES == r]
        if not taps:
            continue
        xr = x if r == 0 else pltpu.roll(x, x.shape[0] - r, axis=0)
        for k in taps:
            q = (lead + k) // SUBLANES
            acc = acc + wdw_ref[k:k + 1, ls] * xr[SUBLANES * q:SUBLANES * q + ROW_BLOCK]
    return acc


def _mix_prompt_step(t, u_ref, v_ref, hb_ref, hbh_ref, c_ref, ch_ref,
                     ws_ref, bsb_ref, wgrp_ref, bsc_ref,
                     wdw_ref, bdw_ref, clg_ref, clb_ref,
                     pa_ref, pb_ref, pc_ref, hb_ext, c_ext, pool_scr, conv_scr):
    sub, d_mix = v_ref.shape
    n_chunks = sub // CHUNK
    head_dim = d_mix // A_HEADS
    group_dim = d_mix // len(POOL_WINDOWS)
    base = pl.multiple_of(t * sub, sub)

    vb = v_ref[...]
    row = lax.broadcasted_iota(jnp.int32, (CHUNK, CHUNK), 0)
    col = lax.broadcasted_iota(jnp.int32, (CHUNK, CHUNK), 1)
    for h in range(A_HEADS):
        hs = slice(h * head_dim, (h + 1) * head_dim)
        wm = jnp.where(row >= col, ws_ref[h], 0.0).astype(BF16)
        rhs = jnp.concatenate(
            [vb[ci * CHUNK:(ci + 1) * CHUNK, hs] for ci in range(n_chunks)], axis=1)
        s = jnp.dot(wm, rhs, preferred_element_type=F32)
        bias = bsb_ref[h]
        for ci in range(n_chunks):
            rs = slice(ci * CHUNK, (ci + 1) * CHUNK)
            s_c = s[:, ci * head_dim:(ci + 1) * head_dim] + bias
            pa_ref[pl.ds(base + ci * CHUNK, CHUNK), hs] = (u_ref[rs, hs] * s_c).astype(BF16)

    first = t == 0
    hb_ext[0:POOL_HALO, :] = jnp.where(first, 0.0, hbh_ref[...])
    hb_ext[POOL_HALO:POOL_HALO + sub, :] = hb_ref[...]

    def pool_body(i, carry):
        r0 = pl.multiple_of(i * ROW_BLOCK, ROW_BLOCK)
        pos = base + r0 + lax.broadcasted_iota(jnp.int32, (ROW_BLOCK, group_dim), 0)
        for g, w in enumerate(POOL_WINDOWS):
            gs = slice(g * group_dim, (g + 1) * group_dim)
            x = hb_ext[pl.ds(r0, POOL_HALO + ROW_BLOCK), gs]
            tok = x[POOL_HALO:POOL_HALO + ROW_BLOCK]
            win = _trailing_sum(x, w)[POOL_HALO:POOL_HALO + ROW_BLOCK]
            cnt = jnp.minimum(pos + 1, w).astype(F32)
            pool_scr[pl.ds(r0, ROW_BLOCK), gs] = win / cnt - tok
        return carry

    lax.fori_loop(0, sub // ROW_BLOCK, pool_body, 0)
    for g in range(len(POOL_WINDOWS)):
        gs = slice(g * group_dim, (g + 1) * group_dim)
        mixed = _bdot(pool_scr[:, gs].astype(BF16), wgrp_ref[g])
        pb_ref[pl.ds(base, sub), gs] = (mixed * bsc_ref[:, gs]).astype(BF16)

    c_ext[0:CONV_HALO, :] = jnp.where(first, 0.0, ch_ref[...])
    c_ext[CONV_HALO:CONV_HALO + sub, :] = c_ref[...]

    def conv_body(i, carry):
        r0 = pl.multiple_of(i * ROW_BLOCK, ROW_BLOCK)
        for lt in range(d_mix // LANES):
            ls = slice(lt * LANES, (lt + 1) * LANES)
            window = c_ext[pl.ds(r0, CONV_HALO + ROW_BLOCK), ls]
            conv_scr[pl.ds(r0, ROW_BLOCK), ls] = _conv_rows(window, wdw_ref, bdw_ref[:, ls], ls)
        return carry

    lax.fori_loop(0, sub // ROW_BLOCK, conv_body, 0)
    y = _layer_norm(conv_scr[...], clg_ref[...], clb_ref[...])
    pc_ref[pl.ds(base, sub), :] = (y * _sigmoid(y)).astype(BF16)


def _mix_sample_step(row0, u_ref, v_ref, hb_ref, c_ref, sp_ref, sc_ref,
                     ws0_ref, bs0_ref, wgrp_ref, bsc_ref,
                     wdw_ref, bdw_ref, clg_ref, clb_ref,
                     pa_ref, pb_ref, pc_ref):
    rows, d_mix = v_ref.shape
    group_dim = d_mix // len(POOL_WINDOWS)
    out_rows = slice(row0, row0 + rows)

    s = v_ref[...].astype(F32) * ws0_ref[...].astype(BF16).astype(F32) + bs0_ref[...]
    pa_ref[out_rows, :] = (u_ref[...] * s).astype(BF16)

    for g, w in enumerate(POOL_WINDOWS):
        gs = slice(g * group_dim, (g + 1) * group_dim)
        tok = hb_ref[:, gs]
        win = tok
        for k in range(POOL_BUF - (w - 1), POOL_BUF):
            win = win + sp_ref[k, :, gs]
        pooled = win / float(w) - tok
        mixed = _bdot(pooled.astype(BF16), wgrp_ref[g])
        pb_ref[out_rows, gs] = (mixed * bsc_ref[:, gs]).astype(BF16)

    conv = c_ref[...] * wdw_ref[CONV_BUF:CONV_WIDTH, :] + bdw_ref[...]
    for k in range(CONV_BUF):
        conv = conv + sc_ref[k] * wdw_ref[k:k + 1, :]
    y = _layer_norm(conv, clg_ref[...], clb_ref[...])
    pc_ref[out_rows, :] = (y * _sigmoid(y)).astype(BF16)


def _mix_kernel(u_ref, v_ref, hb_ref, hbh_ref, c_ref, ch_ref,
                us_ref, vs_ref, hbs_ref, cs_ref, sp_ref, sc_ref,
                ws_ref, bsb_ref, ws0_ref, bs0_ref, wgrp_ref, bsc_ref,
                wdw_ref, bdw_ref, clg_ref, clb_ref,
                pa_ref, pb_ref, pc_ref,
                hb_ext, c_ext, pool_scr, conv_scr, *, n_sub):
    t = pl.program_id(1)

    @pl.when(t < n_sub)
    def _():
        _mix_prompt_step(t, u_ref, v_ref, hb_ref, hbh_ref, c_ref, ch_ref,
                         ws_ref, bsb_ref, wgrp_ref, bsc_ref,
                         wdw_ref, bdw_ref, clg_ref, clb_ref,
                         pa_ref, pb_ref, pc_ref, hb_ext, c_ext, pool_scr, conv_scr)

    @pl.when(t == n_sub)
    def _():
        _mix_sample_step(n_sub * v_ref.shape[0], us_ref, vs_ref, hbs_ref, cs_ref, sp_ref, sc_ref,
                         ws0_ref, bs0_ref, wgrp_ref, bsc_ref,
                         wdw_ref, bdw_ref, clg_ref, clb_ref,
                         pa_ref, pb_ref, pc_ref)


def _mix(u3, v3, hb3, c3, state_pool, state_conv, lw, layer, seq):
    batch, tile_rows, d_mix = hb3.shape
    rows_s = tile_rows - seq
    n_sub = seq // MIX_SUB
    sub_idx = lambda t: jnp.minimum(t, n_sub - 1)
    vec = lambda: pl.BlockSpec((None, 1, d_mix), lambda b, t: (layer, 0, 0))
    sub_spec = lambda: pl.BlockSpec((None, MIX_SUB, d_mix), lambda b, t: (b, sub_idx(t), 0))
    smp_spec = lambda: pl.BlockSpec((None, rows_s, d_mix), lambda b, t: (b, seq // rows_s, 0))

    def halo_spec(rows):
        per = MIX_SUB // rows
        return pl.BlockSpec((None, rows, d_mix),
                            lambda b, t: (b, jnp.maximum(sub_idx(t) * per - 1, 0), 0))

    def state_spec(buf):
        return pl.BlockSpec((None, buf, rows_s, d_mix), lambda b, t: (layer, 0, b, 0),
                            pipeline_mode=_ONCE)

    head_spec = lambda: pl.BlockSpec((None, A_HEADS, CHUNK, CHUNK), lambda b, t: (layer, 0, 0, 0))
    tile_out = pl.BlockSpec((None, tile_rows, d_mix), lambda b, t: (b, 0, 0))
    out3 = jax.ShapeDtypeStruct((batch, tile_rows, d_mix), BF16)
    return pl.pallas_call(
        functools.partial(_mix_kernel, n_sub=n_sub),
        grid=(batch, n_sub + 1),
        in_specs=[
            sub_spec(), sub_spec(),
            sub_spec(), halo_spec(POOL_HALO),
            sub_spec(), halo_spec(CONV_HALO),
            smp_spec(), smp_spec(), smp_spec(), smp_spec(),
            state_spec(POOL_BUF), state_spec(CONV_BUF),
            head_spec(), head_spec(),
            vec(), vec(),
            pl.BlockSpec((None,) + lw["b_w_group"].shape[1:], lambda b, t: (layer, 0, 0, 0)),
            vec(),
            pl.BlockSpec((None, CONV_WIDTH, d_mix), lambda b, t: (layer, 0, 0)),
            vec(), vec(), vec(),
        ],
        out_specs=[tile_out, tile_out, tile_out],
        out_shape=[out3, out3, out3],
        scratch_shapes=[
            pltpu.VMEM((POOL_HALO + MIX_SUB, d_mix), F32),
            pltpu.VMEM((CONV_HALO + MIX_SUB, d_mix), F32),
            pltpu.VMEM((MIX_SUB, d_mix), F32),
            pltpu.VMEM((MIX_SUB, d_mix), F32),
        ],
        compiler_params=_cparams(("parallel", "arbitrary"), 56),
        name="mix",
    )(u3, v3, hb3, hb3, c3, c3, u3, v3, hb3, c3, state_pool, state_conv,
      lw["a_ws"], lw["a_bs_b"], lw["a_ws0"], lw["a_bs0"],
      lw["b_w_group"], lw["b_scale"], lw["c_w_dw"], lw["c_b_dw"], lw["c_ln_g"], lw["c_ln_b"])


def _merge_kernel(pa_ref, pb_ref, pc_ref, wa_ref, wb_ref, wc_ref,
                  g0_ref, g1_ref, g2_ref, o_ref):
    wa = wa_ref[...].astype(BF16)
    wb = wb_ref[...].astype(BF16)
    wc = wc_ref[...].astype(BF16)
    rows = o_ref.shape[0] // ROW_SPLITS
    for i in range(ROW_SPLITS):
        rs = slice(i * rows, (i + 1) * rows)
        merged = (g0_ref[rs, :] * jnp.dot(pa_ref[rs, :], wa, preferred_element_type=F32)
                  + g1_ref[rs, :] * jnp.dot(pb_ref[rs, :], wb, preferred_element_type=F32)
                  + g2_ref[rs, :] * jnp.dot(pc_ref[rs, :], wc, preferred_element_type=F32))
        o_ref[rs, :] = merged.astype(o_ref.dtype)


def _merge(pa, pb, pc, gates_blocked, w_a, w_b, w_c, layer, tm, tn):
    m_rows, d_mix = pa.shape
    d_model = w_a.shape[-1]
    nb = d_model // tn
    assert gates_blocked.shape == (N_BRANCH * nb, m_rows, tn)
    act = lambda: pl.BlockSpec((tm, d_mix), lambda m, n: (m, 0))
    wsp = lambda: pl.BlockSpec((None, d_mix, tn), lambda m, n: (layer, 0, n))
    gsp = lambda j: pl.BlockSpec((None, tm, tn), lambda m, n: (j * nb + n, m, 0))
    return pl.pallas_call(
        _merge_kernel,
        grid=(m_rows // tm, nb),
        in_specs=[act(), act(), act(), wsp(), wsp(), wsp(), gsp(0), gsp(1), gsp(2)],
        out_specs=pl.BlockSpec((tm, tn), lambda m, n: (m, n)),
        out_shape=jax.ShapeDtypeStruct((m_rows, d_model), BF16),
        compiler_params=_cparams(("parallel", "arbitrary"), 52),
        name="merge",
    )(pa, pb, pc, w_a, w_b, w_c, gates_blocked, gates_blocked, gates_blocked)


def _projout_accumulate(m_ref, w_ref, o_ref):
    @pl.when(pl.program_id(2) == 0)
    def _():
        o_ref[...] = jnp.zeros_like(o_ref)

    o_ref[...] += _bdot(m_ref[...], w_ref[...])


def _residual_copies(x_hbm, xbuf, sems, tile, half):
    tm = xbuf.shape[0]
    if len(x_hbm) == 1:
        row0 = pl.multiple_of((tile * 2 + half) * tm, tm)
        return [pltpu.make_async_copy(x_hbm[0].at[pl.ds(row0, tm)], xbuf, sems.at[0])]
    xp, xs = x_hbm
    if half == 0:
        return [pltpu.make_async_copy(xp.at[tile, pl.ds(0, tm)], xbuf, sems.at[0])]
    n_prompt = xp.shape[1] - tm
    return [pltpu.make_async_copy(xp.at[tile, pl.ds(tm, n_prompt)],
                                  xbuf.at[pl.ds(0, n_prompt)], sems.at[1]),
            pltpu.make_async_copy(xs.at[tile], xbuf.at[pl.ds(n_prompt, xs.shape[1])], sems.at[2])]


def _projout_ln_kernel(m_ref, w_ref, *refs, alpha, n_x):
    x_hbm, (g_ref, b_ref, o_ref, xbuf, sems) = refs[:n_x], refs[n_x:]
    tile, h, k = pl.program_id(0), pl.program_id(1), pl.program_id(2)
    for half in range(2):
        @pl.when(jnp.logical_and(k == 0, h == half))
        def _():
            for cp in _residual_copies(x_hbm, xbuf, sems, tile, half):
                cp.start()

    _projout_accumulate(m_ref, w_ref, o_ref)

    for half in range(2):
        @pl.when(jnp.logical_and(k == pl.num_programs(2) - 1, h == half))
        def _():
            for cp in _residual_copies(x_hbm, xbuf, sems, tile, half):
                cp.wait()
            _residual_ln_rows(xbuf, o_ref, g_ref, b_ref, alpha)


def _projout_ln(merged, w_out, x_sources, g, b, layer, alpha, tm, tk):
    m_rows = merged.shape[0]
    d = w_out.shape[-1]
    batch = m_rows // (2 * tm)
    vec = lambda: pl.BlockSpec((None, 1, d), lambda bb, h, k: (layer, 0, 0))
    row_idx = lambda bb, h: bb * 2 + h
    return pl.pallas_call(
        functools.partial(_projout_ln_kernel, alpha=alpha, n_x=len(x_sources)),
        grid=(batch, 2, merged.shape[1] // tk),
        in_specs=[
            pl.BlockSpec((tm, tk), lambda bb, h, k: (row_idx(bb, h), k)),
            pl.BlockSpec((None, tk, d), lambda bb, h, k: (layer, k, 0)),
            *[pl.BlockSpec(memory_space=pl.ANY) for _ in x_sources],
            vec(), vec(),
        ],
        out_specs=pl.BlockSpec((tm, d), lambda bb, h, k: (row_idx(bb, h), 0)),
        out_shape=jax.ShapeDtypeStruct((m_rows, d), F32),
        scratch_shapes=[pltpu.VMEM((tm, d), F32), pltpu.SemaphoreType.DMA((3,))],
        compiler_params=_cparams(("arbitrary", "arbitrary", "arbitrary"), 48),
        name="projout_ln",
    )(merged, w_out, *x_sources, g, b)


def _ffn_row_copies(acc, stage, out_a, out_b, tile, sems, seq, final):
    tm = acc.shape[0]
    per_chunk, n_sem = [], 0
    for i in range(tm // LN_ROWS):
        r0, r1 = i * LN_ROWS, (i + 1) * LN_ROWS
        pieces = []
        if not final:
            row0 = pl.multiple_of(tile * tm, tm)
            for src, dst in ((acc.at[pl.ds(r0, LN_ROWS)], out_a), (stage.at[i % STAGE_SLOTS], out_b)):
                pieces.append(pltpu.make_async_copy(src, dst.at[pl.ds(row0 + r0, LN_ROWS)],
                                                    sems.at[n_sem]))
                n_sem += 1
        else:
            if r0 < seq:
                n = min(r1, seq) - r0
                pieces.append(pltpu.make_async_copy(acc.at[pl.ds(r0, n)],
                                                    out_a.at[tile, pl.ds(r0, n)], sems.at[n_sem]))
                n_sem += 1
            if r1 > seq:
                s0 = max(r0, seq)
                pieces.append(pltpu.make_async_copy(acc.at[pl.ds(s0, r1 - s0)],
                                                    out_b.at[tile, pl.ds(s0 - seq, r1 - s0)],
                                                    sems.at[n_sem]))
                n_sem += 1
        per_chunk.append(pieces)
    return per_chunk


def _ffn_ln_kernel(x_ref, wg_ref, wu_ref, wd_ref, g_ref, b_ref, out_a, out_b, acc, *scratch,
                   alpha, seq, final):
    stage, sems = (None, scratch[0]) if final else scratch
    f = pl.program_id(1)

    @pl.when(f == 0)
    def _():
        acc[...] = jnp.zeros_like(acc)

    wg = wg_ref[...].astype(BF16)
    wu = wu_ref[...].astype(BF16)
    wd = wd_ref[...].astype(BF16)
    rows = acc.shape[0] // FFN_ROW_SPLITS
    for i in range(FFN_ROW_SPLITS):
        rs = slice(i * rows, (i + 1) * rows)
        x = x_ref[rs, :].astype(BF16)
        gate = jnp.dot(x, wg, preferred_element_type=F32)
        up = jnp.dot(x, wu, preferred_element_type=F32)
        hid = (gate * _sigmoid(gate) * up).astype(BF16)
        acc[rs, :] += jnp.dot(hid, wd, preferred_element_type=F32)

    @pl.when(f == pl.num_programs(1) - 1)
    def _():
        per_chunk = _ffn_row_copies(acc, stage, out_a, out_b, pl.program_id(0), sems, seq, final)
        waited = set()
        for i, pieces in enumerate(per_chunk):
            r = i * LN_ROWS
            if not final and i >= STAGE_SLOTS:
                per_chunk[i - STAGE_SLOTS][1].wait()
                waited.add((i - STAGE_SLOTS, 1))
            y = _residual_ln_chunk(x_ref[r:r + LN_ROWS, :], acc, g_ref, b_ref, alpha, r)
            if not final:
                stage[i % STAGE_SLOTS] = y.astype(BF16)
            for cp in pieces:
                cp.start()
        for i, pieces in enumerate(per_chunk):
            for j, cp in enumerate(pieces):
                if (i, j) not in waited:
                    cp.wait()


def _ffn_ln(x, w_up, w_down, g, b, layer, alpha, tm, tf, seq, final):
    m_rows, d = x.shape
    batch = m_rows // tm
    d_ff = w_down.shape[1]
    nf = d_ff // tf
    n_chunks = tm // LN_ROWS
    vec = lambda: pl.BlockSpec((None, 1, d), lambda m, f: (layer, 0, 0))
    if final:
        out_shape = [jax.ShapeDtypeStruct((batch, seq, d), F32),
                     jax.ShapeDtypeStruct((batch, tm - seq, d), F32)]
        scratch = [pltpu.VMEM((tm, d), F32), pltpu.SemaphoreType.DMA((2 * n_chunks,))]
    else:
        out_shape = [jax.ShapeDtypeStruct((m_rows, d), F32), jax.ShapeDtypeStruct((m_rows, d), BF16)]
        scratch = [pltpu.VMEM((tm, d), F32), pltpu.VMEM((STAGE_SLOTS, LN_ROWS, d), BF16),
                   pltpu.SemaphoreType.DMA((2 * n_chunks,))]
    return pl.pallas_call(
        functools.partial(_ffn_ln_kernel, alpha=alpha, seq=seq, final=final),
        grid=(batch, nf),
        in_specs=[
            pl.BlockSpec((tm, d), lambda m, f: (m, 0), pipeline_mode=_ONCE),
            pl.BlockSpec((None, d, tf), lambda m, f: (layer, 0, f)),
            pl.BlockSpec((None, d, tf), lambda m, f: (layer, 0, nf + f)),
            pl.BlockSpec((None, tf, d), lambda m, f: (layer, f, 0)),
            vec(), vec(),
        ],
        out_specs=[pl.BlockSpec(memory_space=pl.ANY), pl.BlockSpec(memory_space=pl.ANY)],
        out_shape=out_shape,
        scratch_shapes=scratch,
        compiler_params=_cparams(("arbitrary", "arbitrary"), 60),
        name="ffn_ln",
    )(x, w_up, w_up, w_down, g, b)


def _state_shift_kernel(s_ref, *refs):
    new_refs, o_ref = refs[:-1], refs[-1]
    layer = pl.program_id(0)
    keep = s_ref.shape[0] - 1
    new = new_refs[0][...]
    for j in range(1, len(new_refs)):
        new = jnp.where(layer == j, new_refs[j][...], new)
    o_ref[0:keep] = s_ref[1:keep + 1]
    o_ref[keep] = new


def _state_shift(state_t, new_rows3, seq):
    depth, buf, _, d_mix = state_t.shape
    batch, tile_rows, _ = new_rows3[0].shape
    rows_s = tile_rows - seq
    blk = pl.BlockSpec((None, buf, rows_s, d_mix), lambda l, b: (l, 0, b, 0))
    new_spec = pl.BlockSpec((None, rows_s, d_mix), lambda l, b: (b, seq // rows_s, 0))
    return pl.pallas_call(
        _state_shift_kernel,
        grid=(depth, batch),
        in_specs=[blk] + [new_spec] * depth,
        out_specs=blk,
        out_shape=jax.ShapeDtypeStruct(state_t.shape, state_t.dtype),
        compiler_params=_cparams(("parallel", "parallel"), 32),
        name="state_shift",
    )(state_t, *new_rows3)


def kernel(x_prompt, x_sample, state_pool, state_conv, w_in, b_in, a_ln_g, a_ln_b, a_ws, a_bs, w_a_out, b_w_group, b_scale, w_b_out, c_w_dw, c_b_dw, c_ln_g, c_ln_b, w_c_out, w_out, ln1_g, ln1_b, w_ffn_up, w_ffn_down, ln2_g, ln2_b):
    batch, seq, d_model = x_prompt.shape
    m_sample = x_sample.shape[0] * x_sample.shape[1]
    depth = w_in.shape[0]
    d_mix = a_ln_g.shape[-1]
    head_dim = d_mix // A_HEADS
    alpha = (2.0 * depth) ** 0.25

    assert x_sample.shape[1] == 1 and m_sample % batch == 0
    rows_s = m_sample // batch
    tile_rows = seq + rows_s
    m_rows = batch * tile_rows
    half_rows = tile_rows // 2
    assert rows_s % BF16_ROWS == 0 and seq % rows_s == 0 and seq % MIX_SUB == 0
    assert seq % PACK_SUB == 0
    assert tile_rows % 2 == 0 and half_rows % LN_ROWS == 0 and LN_ROWS % BF16_ROWS == 0
    assert rows_s <= LN_ROWS and (LN_ROWS - rows_s) % SUBLANES == 0
    assert tile_rows % (ROW_SPLITS * BF16_ROWS) == 0 and rows_s <= tile_rows // ROW_SPLITS

    vec3 = lambda a: a.reshape(depth, 1, a.shape[-1])
    lw = {
        "a_ln_g": vec3(a_ln_g), "a_ln_b": vec3(a_ln_b),
        "a_ws": a_ws,
        "a_bs_b": jnp.broadcast_to(a_bs[..., None], a_bs.shape + (head_dim,)),
        "a_ws0": jnp.repeat(a_ws[:, :, 0, 0], head_dim, axis=-1).reshape(depth, 1, d_mix),
        "a_bs0": jnp.repeat(a_bs[:, :, 0], head_dim, axis=-1).reshape(depth, 1, d_mix),
        "b_w_group": b_w_group, "b_scale": vec3(b_scale),
        "c_w_dw": c_w_dw, "c_b_dw": vec3(c_b_dw),
        "c_ln_g": vec3(c_ln_g), "c_ln_b": vec3(c_ln_b),
    }
    b_in3 = vec3(b_in)
    ln1_g3, ln1_b3, ln2_g3, ln2_b3 = vec3(ln1_g), vec3(ln1_b), vec3(ln2_g), vec3(ln2_b)

    s1 = 2 * d_mix
    s2 = s1 + d_mix
    s3 = s2 + 2 * d_mix

    x_sample3 = x_sample.reshape(batch, rows_s, d_model)
    x_sources = (x_prompt, x_sample3)
    xb = _pack_bf16(x_prompt, x_sample3).reshape(m_rows, d_model)
    tile3 = lambda a: a.reshape(batch, tile_rows, a.shape[-1])
    pool_t = jnp.transpose(state_pool, (0, 2, 1, 3))
    conv_t = jnp.transpose(state_conv, (0, 2, 1, 3))

    hb_l, c_l, v_l = [], [], []
    for l in range(depth):
        u = _proj(xb, w_in, b_in3, l, 0, d_mix, "gelu", BF16, tile_rows, d_mix, "proj_u")
        v, v_new = _proj_gelu_ln(xb, w_in, b_in3, lw["a_ln_g"], lw["a_ln_b"], l, d_mix, d_mix,
                                 tile_rows, rows_s, "proj_v")
        hb = _proj(xb, w_in, b_in3, l, s1, d_mix, "none", F32, tile_rows, d_mix, "proj_b")
        c = _glu_proj(xb, w_in, b_in3, l, s2, d_mix, tile_rows, 512, "proj_c")
        gates = _proj_sigmoid_blocked(xb, w_in, b_in3, l, s3, N_BRANCH * d_model,
                                      tile_rows, 1024, MERGE_COLS, "proj_gates")

        pa, pb, pc = _mix(tile3(u), tile3(v), tile3(hb), tile3(c), pool_t, conv_t, lw, l, seq)
        flat = lambda a: a.reshape(m_rows, d_mix)
        merged = _merge(flat(pa), flat(pb), flat(pc), gates, w_a_out, w_b_out, w_c_out,
                        l, tile_rows, MERGE_COLS)
        x1 = _projout_ln(merged, w_out, x_sources, ln1_g3, ln1_b3, l, alpha, half_rows, 512)
        hb_l.append(tile3(hb))
        c_l.append(tile3(c))
        v_l.append(v_new)
        final = l + 1 == depth
        out_a, out_b = _ffn_ln(x1, w_ffn_up, w_ffn_down, ln2_g3, ln2_b3, l, alpha,
                               tile_rows, 256, seq, final)
        if final:
            y_prompt, y_sample3 = out_a, out_b
        else:
            x_sources, xb = (out_a,), out_b

    y_sample = y_sample3.reshape(m_sample, 1, d_model)
    new_pool_prompt = jnp.stack([a[:, seq - POOL_BUF:seq] for a in hb_l])
    new_conv_prompt = jnp.stack([a[:, seq - CONV_BUF:seq] for a in c_l])
    new_pool_sample = jnp.transpose(_state_shift(pool_t, hb_l, seq), (0, 2, 1, 3))
    new_conv_sample = jnp.transpose(_state_shift(conv_t, c_l, seq), (0, 2, 1, 3))
    new_chunk_v = jnp.stack(v_l)[:, :, None, :]
    return (y_prompt, y_sample, new_pool_prompt, new_conv_prompt,
            new_pool_sample, new_conv_sample, new_chunk_v)
```

```python
import functools

import jax
import jax.numpy as jnp
from jax import lax
from jax.experimental import pallas as pl
from jax.experimental.pallas import tpu as pltpu

F32 = jnp.float32
BF16 = jnp.bfloat16

LN_EPS = 1e-5
A_HEADS = 8
CHUNK = 128
POOL_WINDOWS = (2, 4, 8, 16)
POOL_BUF = 15
CONV_WIDTH = 31
CONV_BUF = CONV_WIDTH - 1
N_BRANCH = 3

SUBLANES = 8
LANES = 128
BF16_ROWS = 16
POOL_HALO = 16
CONV_HALO = 32
ROW_BLOCK = 64
MIX_SUB = 256
PACK_SUB = 512
LN_ROWS = 208
ROW_SPLITS = 5
FFN_ROW_SPLITS = 2
STAGE_SLOTS = 2
MERGE_COLS = 256
V7X_VMEM_BYTES = 64 * 1024 * 1024

_ONCE = pl.Buffered(1)


def _cparams(semantics, vmem_mb):
    assert vmem_mb * 1024 * 1024 < V7X_VMEM_BYTES
    return pltpu.CompilerParams(dimension_semantics=semantics,
                                vmem_limit_bytes=vmem_mb * 1024 * 1024)


def _layer_norm(x, g, b):
    mu = jnp.mean(x, axis=-1, keepdims=True)
    xc = x - mu
    var = jnp.mean(xc * xc, axis=-1, keepdims=True)
    return xc * lax.rsqrt(var + LN_EPS) * g + b


def _gelu_exact(x):
    return 0.5 * x * (1.0 + lax.erf(x * (0.5 ** 0.5)))


def _sigmoid(x):
    return 1.0 / (1.0 + jnp.exp(-x))


def _bdot(a, w):
    return jnp.dot(a, w.astype(BF16), preferred_element_type=F32)


def _residual_ln_chunk(x_rows, acc_ref, g_ref, b_ref, alpha, r):
    y = _layer_norm(alpha * x_rows + acc_ref[pl.ds(r, LN_ROWS), :], g_ref[...], b_ref[...])
    acc_ref[pl.ds(r, LN_ROWS), :] = y
    return y


def _residual_ln_rows(x_ref, acc_ref, g_ref, b_ref, alpha):
    def body(i, carry):
        r = pl.multiple_of(i * LN_ROWS, LN_ROWS)
        _residual_ln_chunk(x_ref[pl.ds(r, LN_ROWS), :], acc_ref, g_ref, b_ref, alpha, r)
        return carry

    lax.fori_loop(0, acc_ref.shape[0] // LN_ROWS, body, 0)


def _pack_kernel(xp_ref, xs_ref, o_ref, *, n_sub):
    t = pl.program_id(1)
    sub = xp_ref.shape[0]

    @pl.when(t < n_sub)
    def _():
        o_ref[pl.ds(pl.multiple_of(t * sub, sub), sub), :] = xp_ref[...].astype(BF16)

    @pl.when(t == n_sub)
    def _():
        o_ref[n_sub * sub:, :] = xs_ref[...].astype(BF16)


def _pack_bf16(x_prompt, x_sample3):
    batch, seq, d = x_prompt.shape
    rows_s = x_sample3.shape[1]
    n_sub = seq // PACK_SUB
    return pl.pallas_call(
        functools.partial(_pack_kernel, n_sub=n_sub),
        grid=(batch, n_sub + 1),
        in_specs=[
            pl.BlockSpec((None, PACK_SUB, d), lambda b, t: (b, jnp.minimum(t, n_sub - 1), 0)),
            pl.BlockSpec((None, rows_s, d), lambda b, t: (b, 0, 0)),
        ],
        out_specs=pl.BlockSpec((None, seq + rows_s, d), lambda b, t: (b, 0, 0)),
        out_shape=jax.ShapeDtypeStruct((batch, seq + rows_s, d), BF16),
        compiler_params=_cparams(("parallel", "arbitrary"), 40),
        name="pack",
    )(x_prompt, x_sample3)


def _proj_kernel(x_ref, w_ref, b_ref, o_ref, *, act):
    w = w_ref[...].astype(BF16)
    rows = x_ref.shape[0] // ROW_SPLITS
    for i in range(ROW_SPLITS):
        rs = slice(i * rows, (i + 1) * rows)
        h = jnp.dot(x_ref[rs, :], w, preferred_element_type=F32) + b_ref[...]
        if act == "gelu":
            h = _gelu_exact(h)
        elif act == "sigmoid":
            h = _sigmoid(h)
        o_ref[rs, :] = h.astype(o_ref.dtype)


def _proj_sigmoid_blocked_kernel(x_ref, w_ref, b_ref, o_ref):
    w = w_ref[...].astype(BF16)
    n_blk, tile_rows, sub = o_ref.shape
    rows = tile_rows // ROW_SPLITS
    for i in range(ROW_SPLITS):
        rs = slice(i * rows, (i + 1) * rows)
        h = _sigmoid(jnp.dot(x_ref[rs, :], w, preferred_element_type=F32) + b_ref[...])
        for j in range(n_blk):
            o_ref[j, rs, :] = h[:, j * sub:(j + 1) * sub].astype(o_ref.dtype)


def _proj_gelu_ln_kernel(x_ref, w_ref, b_ref, g_ref, beta_ref, o_ref, os_ref):
    w = w_ref[...].astype(BF16)
    rows = x_ref.shape[0] // ROW_SPLITS
    n_s = os_ref.shape[0]
    for i in range(ROW_SPLITS):
        rs = slice(i * rows, (i + 1) * rows)
        h = jnp.dot(x_ref[rs, :], w, preferred_element_type=F32) + b_ref[...]
        v = _layer_norm(_gelu_exact(h), g_ref[...], beta_ref[...])
        o_ref[rs, :] = v.astype(BF16)
        if i == ROW_SPLITS - 1:
            os_ref[...] = v[rows - n_s:, :]


def _glu_kernel(x_ref, wv_ref, wg_ref, bv_ref, bg_ref, o_ref):
    wv = wv_ref[...].astype(BF16)
    wg = wg_ref[...].astype(BF16)
    rows = x_ref.shape[0] // ROW_SPLITS
    for i in range(ROW_SPLITS):
        rs = slice(i * rows, (i + 1) * rows)
        x = x_ref[rs, :]
        val = jnp.dot(x, wv, preferred_element_type=F32) + bv_ref[...]
        gate = jnp.dot(x, wg, preferred_element_type=F32) + bg_ref[...]
        o_ref[rs, :] = val * _sigmoid(gate)


def _proj(x, w, b, layer, col0, ncols, act, out_dtype, tm, tn, name):
    m_rows, k = x.shape
    off = col0 // tn
    return pl.pallas_call(
        functools.partial(_proj_kernel, act=act),
        grid=(m_rows // tm, ncols // tn),
        in_specs=[
            pl.BlockSpec((tm, k), lambda m, n: (m, 0)),
            pl.BlockSpec((None, k, tn), lambda m, n: (layer, 0, off + n)),
            pl.BlockSpec((None, 1, tn), lambda m, n: (layer, 0, off + n)),
        ],
        out_specs=pl.BlockSpec((tm, tn), lambda m, n: (m, n)),
        out_shape=jax.ShapeDtypeStruct((m_rows, ncols), out_dtype),
        compiler_params=_cparams(("parallel", "arbitrary"), 56),
        name=name,
    )(x, w, b)


def _proj_sigmoid_blocked(x, w, b, layer, col0, ncols, tm, tn, sub, name):
    m_rows, k = x.shape
    off = col0 // tn
    return pl.pallas_call(
        _proj_sigmoid_blocked_kernel,
        grid=(m_rows // tm, ncols // tn),
        in_specs=[
            pl.BlockSpec((tm, k), lambda m, n: (m, 0)),
            pl.BlockSpec((None, k, tn), lambda m, n: (layer, 0, off + n)),
            pl.BlockSpec((None, 1, tn), lambda m, n: (layer, 0, off + n)),
        ],
        out_specs=pl.BlockSpec((tn // sub, tm, sub), lambda m, n: (n, m, 0)),
        out_shape=jax.ShapeDtypeStruct((ncols // sub, m_rows, sub), BF16),
        compiler_params=_cparams(("parallel", "arbitrary"), 56),
        name=name,
    )(x, w, b)


def _proj_gelu_ln(x, w, b, g, beta, layer, col0, ncols, tm, rows_s, name):
    m_rows, k = x.shape
    off = col0 // ncols
    vec = lambda: pl.BlockSpec((None, 1, ncols), lambda m: (layer, 0, 0))
    return pl.pallas_call(
        _proj_gelu_ln_kernel,
        grid=(m_rows // tm,),
        in_specs=[
            pl.BlockSpec((tm, k), lambda m: (m, 0)),
            pl.BlockSpec((None, k, ncols), lambda m: (layer, 0, off)),
            pl.BlockSpec((None, 1, ncols), lambda m: (layer, 0, off)),
            vec(), vec(),
        ],
        out_specs=[pl.BlockSpec((tm, ncols), lambda m: (m, 0)),
                   pl.BlockSpec((rows_s, ncols), lambda m: (m, 0))],
        out_shape=[jax.ShapeDtypeStruct((m_rows, ncols), BF16),
                   jax.ShapeDtypeStruct((m_rows // tm * rows_s, ncols), F32)],
        compiler_params=_cparams(("parallel",), 52),
        name=name,
    )(x, w, b, g, beta)


def _glu_proj(x, w, b, layer, col0, ncols, tm, tn, name):
    m_rows, k = x.shape
    off_v = col0 // tn
    off_g = (col0 + ncols) // tn
    return pl.pallas_call(
        _glu_kernel,
        grid=(m_rows // tm, ncols // tn),
        in_specs=[
            pl.BlockSpec((tm, k), lambda m, n: (m, 0)),
            pl.BlockSpec((None, k, tn), lambda m, n: (layer, 0, off_v + n)),
            pl.BlockSpec((None, k, tn), lambda m, n: (layer, 0, off_g + n)),
            pl.BlockSpec((None, 1, tn), lambda m, n: (layer, 0, off_v + n)),
            pl.BlockSpec((None, 1, tn), lambda m, n: (layer, 0, off_g + n)),
        ],
        out_specs=pl.BlockSpec((tm, tn), lambda m, n: (m, n)),
        out_shape=jax.ShapeDtypeStruct((m_rows, ncols), F32),
        compiler_params=_cparams(("parallel", "arbitrary"), 56),
        name=name,
    )(x, w, w, b, b)


def _trailing_sum(x, w):
    result, offset, part, span = None, 0, x, 1
    while True:
        if w & span:
            term = part if offset == 0 else pltpu.roll(part, offset, axis=0)
            result = term if result is None else result + term
            offset += span
        if span * 2 > w:
            return result
        part = part + pltpu.roll(part, span, axis=0)
        span *= 2


def _conv_rows(x, wdw_ref, bias, ls):
    lead = CONV_HALO - CONV_BUF
    acc = jnp.broadcast_to(bias, (ROW_BLOCK, LANES))
    for r in range(SUBLANES):
        taps = [k for k in range(CONV_WIDTH) if (lead + k) % SUBLANES == r]
        if not taps:
            continue
        xr = x if r == 0 else pltpu.roll(x, x.shape[0] - r, axis=0)
        for k in taps:
            q = (lead + k) // SUBLANES
            acc = acc + wdw_ref[k:k + 1, ls] * xr[SUBLANES * q:SUBLANES * q + ROW_BLOCK]
    return acc


def _mix_prompt_step(t, u_ref, v_ref, hb_ref, hbh_ref, c_ref, ch_ref,
                     ws_ref, bsb_ref, wgrp_ref, bsc_ref,
                     wdw_ref, bdw_ref, clg_ref, clb_ref,
                     pa_ref, pb_ref, pc_ref, hb_ext, c_ext, pool_scr, conv_scr):
    sub, d_mix = v_ref.shape
    n_chunks = sub // CHUNK
    head_dim = d_mix // A_HEADS
    group_dim = d_mix // len(POOL_WINDOWS)
    base = pl.multiple_of(t * sub, sub)

    vb = v_ref[...]
    row = lax.broadcasted_iota(jnp.int32, (CHUNK, CHUNK), 0)
    col = lax.broadcasted_iota(jnp.int32, (CHUNK, CHUNK), 1)
    for h in range(A_HEADS):
        hs = slice(h * head_dim, (h + 1) * head_dim)
        wm = jnp.where(row >= col, ws_ref[h], 0.0).astype(BF16)
        rhs = jnp.concatenate(
            [vb[ci * CHUNK:(ci + 1) * CHUNK, hs] for ci in range(n_chunks)], axis=1)
        s = jnp.dot(wm, rhs, preferred_element_type=F32)
        bias = bsb_ref[h]
        for ci in range(n_chunks):
            rs = slice(ci * CHUNK, (ci + 1) * CHUNK)
            s_c = s[:, ci * head_dim:(ci + 1) * head_dim] + bias
            pa_ref[pl.ds(base + ci * CHUNK, CHUNK), hs] = (u_ref[rs, hs] * s_c).astype(BF16)

    first = t == 0
    hb_ext[0:POOL_HALO, :] = jnp.where(first, 0.0, hbh_ref[...])
    hb_ext[POOL_HALO:POOL_HALO + sub, :] = hb_ref[...]

    def pool_body(i, carry):
        r0 = pl.multiple_of(i * ROW_BLOCK, ROW_BLOCK)
        pos = base + r0 + lax.broadcasted_iota(jnp.int32, (ROW_BLOCK, group_dim), 0)
        for g, w in enumerate(POOL_WINDOWS):
            gs = slice(g * group_dim, (g + 1) * group_dim)
            x = hb_ext[pl.ds(r0, POOL_HALO + ROW_BLOCK), gs]
            tok = x[POOL_HALO:POOL_HALO + ROW_BLOCK]
            win = _trailing_sum(x, w)[POOL_HALO:POOL_HALO + ROW_BLOCK]
            cnt = jnp.minimum(pos + 1, w).astype(F32)
            pool_scr[pl.ds(r0, ROW_BLOCK), gs] = win / cnt - tok
        return carry

    lax.fori_loop(0, sub // ROW_BLOCK, pool_body, 0)
    for g in range(len(POOL_WINDOWS)):
        gs = slice(g * group_dim, (g + 1) * group_dim)
        mixed = _bdot(pool_scr[:, gs].astype(BF16), wgrp_ref[g])
        pb_ref[pl.ds(base, sub), gs] = (mixed * bsc_ref[:, gs]).astype(BF16)

    c_ext[0:CONV_HALO, :] = jnp.where(first, 0.0, ch_ref[...])
    c_ext[CONV_HALO:CONV_HALO + sub, :] = c_ref[...]

    def conv_body(i, carry):
        r0 = pl.multiple_of(i * ROW_BLOCK, ROW_BLOCK)
        for lt in range(d_mix // LANES):
            ls = slice(lt * LANES, (lt + 1) * LANES)
            window = c_ext[pl.ds(r0, CONV_HALO + ROW_BLOCK), ls]
            conv_scr[pl.ds(r0, ROW_BLOCK), ls] = _conv_rows(window, wdw_ref, bdw_ref[:, ls], ls)
        return carry

    lax.fori_loop(0, sub // ROW_BLOCK, conv_body, 0)
    y = _layer_norm(conv_scr[...], clg_ref[...], clb_ref[...])
    pc_ref[pl.ds(base, sub), :] = (y * _sigmoid(y)).astype(BF16)


def _mix_sample_step(row0, u_ref, v_ref, hb_ref, c_ref, sp_ref, sc_ref,
                     ws0_ref, bs0_ref, wgrp_ref, bsc_ref,
                     wdw_ref, bdw_ref, clg_ref, clb_ref,
                     pa_ref, pb_ref, pc_ref):
    rows, d_mix = v_ref.shape
    group_dim = d_mix // len(POOL_WINDOWS)
    out_rows = slice(row0, row0 + rows)

    s = v_ref[...].astype(F32) * ws0_ref[...].astype(BF16).astype(F32) + bs0_ref[...]
    pa_ref[out_rows, :] = (u_ref[...] * s).astype(BF16)

    for g, w in enumerate(POOL_WINDOWS):
        gs = slice(g * group_dim, (g + 1) * group_dim)
        tok = hb_ref[:, gs]
        win = tok
        for k in range(POOL_BUF - (w - 1), POOL_BUF):
            win = win + sp_ref[k, :, gs]
        pooled = win / float(w) - tok
        mixed = _bdot(pooled.astype(BF16), wgrp_ref[g])
        pb_ref[out_rows, gs] = (mixed * bsc_ref[:, gs]).astype(BF16)

    conv = c_ref[...] * wdw_ref[CONV_BUF:CONV_WIDTH, :] + bdw_ref[...]
    for k in range(CONV_BUF):
        conv = conv + sc_ref[k] * wdw_ref[k:k + 1, :]
    y = _layer_norm(conv, clg_ref[...], clb_ref[...])
    pc_ref[out_rows, :] = (y * _sigmoid(y)).astype(BF16)


def _mix_kernel(u_ref, v_ref, hb_ref, hbh_ref, c_ref, ch_ref,
                us_ref, vs_ref, hbs_ref, cs_ref, sp_ref, sc_ref,
                ws_ref, bsb_ref, ws0_ref, bs0_ref, wgrp_ref, bsc_ref,
                wdw_ref, bdw_ref, clg_ref, clb_ref,
                pa_ref, pb_ref, pc_ref,
                hb_ext, c_ext, pool_scr, conv_scr, *, n_sub):
    t = pl.program_id(1)

    @pl.when(t < n_sub)
    def _():
        _mix_prompt_step(t, u_ref, v_ref, hb_ref, hbh_ref, c_ref, ch_ref,
                         ws_ref, bsb_ref, wgrp_ref, bsc_ref,
                         wdw_ref, bdw_ref, clg_ref, clb_ref,
                         pa_ref, pb_ref, pc_ref, hb_ext, c_ext, pool_scr, conv_scr)

    @pl.when(t == n_sub)
    def _():
        _mix_sample_step(n_sub * v_ref.shape[0], us_ref, vs_ref, hbs_ref, cs_ref, sp_ref, sc_ref,
                         ws0_ref, bs0_ref, wgrp_ref, bsc_ref,
                         wdw_ref, bdw_ref, clg_ref, clb_ref,
                         pa_ref, pb_ref, pc_ref)


def _mix(u3, v3, hb3, c3, state_pool, state_conv, lw, layer, seq):
    batch, tile_rows, d_mix = hb3.shape
    rows_s = tile_rows - seq
    n_sub = seq // MIX_SUB
    sub_idx = lambda t: jnp.minimum(t, n_sub - 1)
    vec = lambda: pl.BlockSpec((None, 1, d_mix), lambda b, t: (layer, 0, 0))
    sub_spec = lambda: pl.BlockSpec((None, MIX_SUB, d_mix), lambda b, t: (b, sub_idx(t), 0))
    smp_spec = lambda: pl.BlockSpec((None, rows_s, d_mix), lambda b, t: (b, seq // rows_s, 0))

    def halo_spec(rows):
        per = MIX_SUB // rows
        return pl.BlockSpec((None, rows, d_mix),
                            lambda b, t: (b, jnp.maximum(sub_idx(t) * per - 1, 0), 0))

    def state_spec(buf):
        return pl.BlockSpec((None, buf, rows_s, d_mix), lambda b, t: (layer, 0, b, 0),
                            pipeline_mode=_ONCE)

    head_spec = lambda: pl.BlockSpec((None, A_HEADS, CHUNK, CHUNK), lambda b, t: (layer, 0, 0, 0))
    tile_out = pl.BlockSpec((None, tile_rows, d_mix), lambda b, t: (b, 0, 0))
    out3 = jax.ShapeDtypeStruct((batch, tile_rows, d_mix), BF16)
    return pl.pallas_call(
        functools.partial(_mix_kernel, n_sub=n_sub),
        grid=(batch, n_sub + 1),
        in_specs=[
            sub_spec(), sub_spec(),
            sub_spec(), halo_spec(POOL_HALO),
            sub_spec(), halo_spec(CONV_HALO),
            smp_spec(), smp_spec(), smp_spec(), smp_spec(),
            state_spec(POOL_BUF), state_spec(CONV_BUF),
            head_spec(), head_spec(),
            vec(), vec(),
            pl.BlockSpec((None,) + lw["b_w_group"].shape[1:], lambda b, t: (layer, 0, 0, 0)),
            vec(),
            pl.BlockSpec((None, CONV_WIDTH, d_mix), lambda b, t: (layer, 0, 0)),
            vec(), vec(), vec(),
        ],
        out_specs=[tile_out, tile_out, tile_out],
        out_shape=[out3, out3, out3],
        scratch_shapes=[
            pltpu.VMEM((POOL_HALO + MIX_SUB, d_mix), F32),
            pltpu.VMEM((CONV_HALO + MIX_SUB, d_mix), F32),
            pltpu.VMEM((MIX_SUB, d_mix), F32),
            pltpu.VMEM((MIX_SUB, d_mix), F32),
        ],
        compiler_params=_cparams(("parallel", "arbitrary"), 56),
        name="mix",
    )(u3, v3, hb3, hb3, c3, c3, u3, v3, hb3, c3, state_pool, state_conv,
      lw["a_ws"], lw["a_bs_b"], lw["a_ws0"], lw["a_bs0"],
      lw["b_w_group"], lw["b_scale"], lw["c_w_dw"], lw["c_b_dw"], lw["c_ln_g"], lw["c_ln_b"])


def _merge_kernel(pa_ref, pb_ref, pc_ref, wa_ref, wb_ref, wc_ref,
                  g0_ref, g1_ref, g2_ref, o_ref):
    wa = wa_ref[...].astype(BF16)
    wb = wb_ref[...].astype(BF16)
    wc = wc_ref[...].astype(BF16)
    rows = o_ref.shape[0] // ROW_SPLITS
    for i in range(ROW_SPLITS):
        rs = slice(i * rows, (i + 1) * rows)
        merged = (g0_ref[rs, :] * jnp.dot(pa_ref[rs, :], wa, preferred_element_type=F32)
                  + g1_ref[rs, :] * jnp.dot(pb_ref[rs, :], wb, preferred_element_type=F32)
                  + g2_ref[rs, :] * jnp.dot(pc_ref[rs, :], wc, preferred_element_type=F32))
        o_ref[rs, :] = merged.astype(o_ref.dtype)


def _merge(pa, pb, pc, gates_blocked, w_a, w_b, w_c, layer, tm, tn):
    m_rows, d_mix = pa.shape
    d_model = w_a.shape[-1]
    nb = d_model // tn
    assert gates_blocked.shape == (N_BRANCH * nb, m_rows, tn)
    act = lambda: pl.BlockSpec((tm, d_mix), lambda m, n: (m, 0))
    wsp = lambda: pl.BlockSpec((None, d_mix, tn), lambda m, n: (layer, 0, n))
    gsp = lambda j: pl.BlockSpec((None, tm, tn), lambda m, n: (j * nb + n, m, 0))
    return pl.pallas_call(
        _merge_kernel,
        grid=(m_rows // tm, nb),
        in_specs=[act(), act(), act(), wsp(), wsp(), wsp(), gsp(0), gsp(1), gsp(2)],
        out_specs=pl.BlockSpec((tm, tn), lambda m, n: (m, n)),
        out_shape=jax.ShapeDtypeStruct((m_rows, d_model), BF16),
        compiler_params=_cparams(("parallel", "arbitrary"), 52),
        name="merge",
    )(pa, pb, pc, w_a, w_b, w_c, gates_blocked, gates_blocked, gates_blocked)


def _projout_accumulate(m_ref, w_ref, o_ref):
    @pl.when(pl.program_id(2) == 0)
    def _():
        o_ref[...] = jnp.zeros_like(o_ref)

    o_ref[...] += _bdot(m_ref[...], w_ref[...])


def _residual_copies(x_hbm, xbuf, sems, tile, half):
    tm = xbuf.shape[0]
    if len(x_hbm) == 1:
        row0 = pl.multiple_of((tile * 2 + half) * tm, tm)
        return [pltpu.make_async_copy(x_hbm[0].at[pl.ds(row0, tm)], xbuf, sems.at[0])]
    xp, xs = x_hbm
    if half == 0:
        return [pltpu.make_async_copy(xp.at[tile, pl.ds(0, tm)], xbuf, sems.at[0])]
    n_prompt = xp.shape[1] - tm
    return [pltpu.make_async_copy(xp.at[tile, pl.ds(tm, n_prompt)],
                                  xbuf.at[pl.ds(0, n_prompt)], sems.at[1]),
            pltpu.make_async_copy(xs.at[tile], xbuf.at[pl.ds(n_prompt, xs.shape[1])], sems.at[2])]


def _projout_ln_kernel(m_ref, w_ref, *refs, alpha, n_x):
    x_hbm, (g_ref, b_ref, o_ref, xbuf, sems) = refs[:n_x], refs[n_x:]
    tile, h, k = pl.program_id(0), pl.program_id(1), pl.program_id(2)
    for half in range(2):
        @pl.when(jnp.logical_and(k == 0, h == half))
        def _():
            for cp in _residual_copies(x_hbm, xbuf, sems, tile, half):
                cp.start()

    _projout_accumulate(m_ref, w_ref, o_ref)

    for half in range(2):
        @pl.when(jnp.logical_and(k == pl.num_programs(2) - 1, h == half))
        def _():
            for cp in _residual_copies(x_hbm, xbuf, sems, tile, half):
                cp.wait()
            _residual_ln_rows(xbuf, o_ref, g_ref, b_ref, alpha)


def _projout_ln(merged, w_out, x_sources, g, b, layer, alpha, tm, tk):
    m_rows = merged.shape[0]
    d = w_out.shape[-1]
    batch = m_rows // (2 * tm)
    vec = lambda: pl.BlockSpec((None, 1, d), lambda bb, h, k: (layer, 0, 0))
    row_idx = lambda bb, h: bb * 2 + h
    return pl.pallas_call(
        functools.partial(_projout_ln_kernel, alpha=alpha, n_x=len(x_sources)),
        grid=(batch, 2, merged.shape[1] // tk),
        in_specs=[
            pl.BlockSpec((tm, tk), lambda bb, h, k: (row_idx(bb, h), k)),
            pl.BlockSpec((None, tk, d), lambda bb, h, k: (layer, k, 0)),
            *[pl.BlockSpec(memory_space=pl.ANY) for _ in x_sources],
            vec(), vec(),
        ],
        out_specs=pl.BlockSpec((tm, d), lambda bb, h, k: (row_idx(bb, h), 0)),
        out_shape=jax.ShapeDtypeStruct((m_rows, d), F32),
        scratch_shapes=[pltpu.VMEM((tm, d), F32), pltpu.SemaphoreType.DMA((3,))],
        compiler_params=_cparams(("arbitrary", "arbitrary", "arbitrary"), 48),
        name="projout_ln",
    )(merged, w_out, *x_sources, g, b)


def _ffn_row_copies(acc, stage, out_a, out_b, tile, sems, seq, final):
    tm = acc.shape[0]
    per_chunk, n_sem = [], 0
    for i in range(tm // LN_ROWS):
        r0, r1 = i * LN_ROWS, (i + 1) * LN_ROWS
        pieces = []
        if not final:
            row0 = pl.multiple_of(tile * tm, tm)
            for src, dst in ((acc.at[pl.ds(r0, LN_ROWS)], out_a), (stage.at[i % STAGE_SLOTS], out_b)):
                pieces.append(pltpu.make_async_copy(src, dst.at[pl.ds(row0 + r0, LN_ROWS)],
                                                    sems.at[n_sem]))
                n_sem += 1
        else:
            if r0 < seq:
                n = min(r1, seq) - r0
                pieces.append(pltpu.make_async_copy(acc.at[pl.ds(r0, n)],
                                                    out_a.at[tile, pl.ds(r0, n)], sems.at[n_sem]))
                n_sem += 1
            if r1 > seq:
                s0 = max(r0, seq)
                pieces.append(pltpu.make_async_copy(acc.at[pl.ds(s0, r1 - s0)],
                                                    out_b.at[tile, pl.ds(s0 - seq, r1 - s0)],
                                                    sems.at[n_sem]))
                n_sem += 1
        per_chunk.append(pieces)
    return per_chunk


def _ffn_ln_kernel(x_ref, wg_ref, wu_ref, wd_ref, g_ref, b_ref, out_a, out_b, acc, *scratch,
                   alpha, seq, final):
    stage, sems = (None, scratch[0]) if final else scratch
    f = pl.program_id(1)

    def hidden_tile(first):
        wg = wg_ref[...].astype(BF16)
        wu = wu_ref[...].astype(BF16)
        wd = wd_ref[...].astype(BF16)
        rows = acc.shape[0] // FFN_ROW_SPLITS
        for i in range(FFN_ROW_SPLITS):
            rs = slice(i * rows, (i + 1) * rows)
            x = x_ref[rs, :].astype(BF16)
            gate = jnp.dot(x, wg, preferred_element_type=F32)
            up = jnp.dot(x, wu, preferred_element_type=F32)
            hid = (gate * _sigmoid(gate) * up).astype(BF16)
            part = jnp.dot(hid, wd, preferred_element_type=F32)
            if first:
                acc[rs, :] = part
            else:
                acc[rs, :] += part

    pl.when(f == 0)(functools.partial(hidden_tile, True))
    pl.when(f > 0)(functools.partial(hidden_tile, False))

    @pl.when(f == pl.num_programs(1) - 1)
    def _():
        per_chunk = _ffn_row_copies(acc, stage, out_a, out_b, pl.program_id(0), sems, seq, final)
        waited = set()
        for i, pieces in enumerate(per_chunk):
            r = i * LN_ROWS
            if not final and i >= STAGE_SLOTS:
                per_chunk[i - STAGE_SLOTS][1].wait()
                waited.add((i - STAGE_SLOTS, 1))
            y = _residual_ln_chunk(x_ref[r:r + LN_ROWS, :], acc, g_ref, b_ref, alpha, r)
            if not final:
                stage[i % STAGE_SLOTS] = y.astype(BF16)
            for cp in pieces:
                cp.start()
        for i, pieces in enumerate(per_chunk):
            for j, cp in enumerate(pieces):
                if (i, j) not in waited:
                    cp.wait()


def _ffn_ln(x, w_up, w_down, g, b, layer, alpha, tm, tf, seq, final):
    m_rows, d = x.shape
    batch = m_rows // tm
    d_ff = w_down.shape[1]
    nf = d_ff // tf
    n_chunks = tm // LN_ROWS
    vec = lambda: pl.BlockSpec((None, 1, d), lambda m, f: (layer, 0, 0))
    if final:
        out_shape = [jax.ShapeDtypeStruct((batch, seq, d), F32),
                     jax.ShapeDtypeStruct((batch, tm - seq, d), F32)]
        scratch = [pltpu.VMEM((tm, d), F32), pltpu.SemaphoreType.DMA((2 * n_chunks,))]
    else:
        out_shape = [jax.ShapeDtypeStruct((m_rows, d), F32), jax.ShapeDtypeStruct((m_rows, d), BF16)]
        scratch = [pltpu.VMEM((tm, d), F32), pltpu.VMEM((STAGE_SLOTS, LN_ROWS, d), BF16),
                   pltpu.SemaphoreType.DMA((2 * n_chunks,))]
    return pl.pallas_call(
        functools.partial(_ffn_ln_kernel, alpha=alpha, seq=seq, final=final),
        grid=(batch, nf),
        in_specs=[
            pl.BlockSpec((tm, d), lambda m, f: (m, 0), pipeline_mode=_ONCE),
            pl.BlockSpec((None, d, tf), lambda m, f: (layer, 0, f)),
            pl.BlockSpec((None, d, tf), lambda m, f: (layer, 0, nf + f)),
            pl.BlockSpec((None, tf, d), lambda m, f: (layer, f, 0)),
            vec(), vec(),
        ],
        out_specs=[pl.BlockSpec(memory_space=pl.ANY), pl.BlockSpec(memory_space=pl.ANY)],
        out_shape=out_shape,
        scratch_shapes=scratch,
        compiler_params=_cparams(("arbitrary", "arbitrary"), 60),
        name="ffn_ln",
    )(x, w_up, w_up, w_down, g, b)


def _state_shift_kernel(s_ref, *refs):
    new_refs, o_ref = refs[:-1], refs[-1]
    layer = pl.program_id(0)
    keep = s_ref.shape[0] - 1
    new = new_refs[0][...]
    for j in range(1, len(new_refs)):
        new = jnp.where(layer == j, new_refs[j][...], new)
    o_ref[0:keep] = s_ref[1:keep + 1]
    o_ref[keep] = new


def _state_shift(state_t, new_rows3, seq):
    depth, buf, _, d_mix = state_t.shape
    batch, tile_rows, _ = new_rows3[0].shape
    rows_s = tile_rows - seq
    blk = pl.BlockSpec((None, buf, rows_s, d_mix), lambda l, b: (l, 0, b, 0))
    new_spec = pl.BlockSpec((None, rows_s, d_mix), lambda l, b: (b, seq // rows_s, 0))
    return pl.pallas_call(
        _state_shift_kernel,
        grid=(depth, batch),
        in_specs=[blk] + [new_spec] * depth,
        out_specs=blk,
        out_shape=jax.ShapeDtypeStruct(state_t.shape, state_t.dtype),
        compiler_params=_cparams(("parallel", "parallel"), 32),
        name="state_shift",
    )(state_t, *new_rows3)


def kernel(x_prompt, x_sample, state_pool, state_conv, w_in, b_in, a_ln_g, a_ln_b, a_ws, a_bs, w_a_out, b_w_group, b_scale, w_b_out, c_w_dw, c_b_dw, c_ln_g, c_ln_b, w_c_out, w_out, ln1_g, ln1_b, w_ffn_up, w_ffn_down, ln2_g, ln2_b):
    batch, seq, d_model = x_prompt.shape
    m_sample = x_sample.shape[0] * x_sample.shape[1]
    depth = w_in.shape[0]
    d_mix = a_ln_g.shape[-1]
    head_dim = d_mix // A_HEADS
    alpha = (2.0 * depth) ** 0.25

    assert x_sample.shape[1] == 1 and m_sample % batch == 0
    rows_s = m_sample // batch
    tile_rows = seq + rows_s
    m_rows = batch * tile_rows
    half_rows = tile_rows // 2
    assert rows_s % BF16_ROWS == 0 and seq % rows_s == 0 and seq % MIX_SUB == 0
    assert seq % PACK_SUB == 0
    assert tile_rows % 2 == 0 and half_rows % LN_ROWS == 0 and LN_ROWS % BF16_ROWS == 0
    assert rows_s <= LN_ROWS and (LN_ROWS - rows_s) % SUBLANES == 0
    assert tile_rows % (ROW_SPLITS * BF16_ROWS) == 0 and rows_s <= tile_rows // ROW_SPLITS

    vec3 = lambda a: a.reshape(depth, 1, a.shape[-1])
    lw = {
        "a_ln_g": vec3(a_ln_g), "a_ln_b": vec3(a_ln_b),
        "a_ws": a_ws,
        "a_bs_b": jnp.broadcast_to(a_bs[..., None], a_bs.shape + (head_dim,)),
        "a_ws0": jnp.repeat(a_ws[:, :, 0, 0], head_dim, axis=-1).reshape(depth, 1, d_mix),
        "a_bs0": jnp.repeat(a_bs[:, :, 0], head_dim, axis=-1).reshape(depth, 1, d_mix),
        "b_w_group": b_w_group, "b_scale": vec3(b_scale),
        "c_w_dw": c_w_dw, "c_b_dw": vec3(c_b_dw),
        "c_ln_g": vec3(c_ln_g), "c_ln_b": vec3(c_ln_b),
    }
    b_in3 = vec3(b_in)
    ln1_g3, ln1_b3, ln2_g3, ln2_b3 = vec3(ln1_g), vec3(ln1_b), vec3(ln2_g), vec3(ln2_b)

    s1 = 2 * d_mix
    s2 = s1 + d_mix
    s3 = s2 + 2 * d_mix

    x_sample3 = x_sample.reshape(batch, rows_s, d_model)
    x_sources = (x_prompt, x_sample3)
    xb = _pack_bf16(x_prompt, x_sample3).reshape(m_rows, d_model)
    tile3 = lambda a: a.reshape(batch, tile_rows, a.shape[-1])
    pool_t = jnp.transpose(state_pool, (0, 2, 1, 3))
    conv_t = jnp.transpose(state_conv, (0, 2, 1, 3))

    hb_l, c_l, v_l = [], [], []
    for l in range(depth):
        u = _proj(xb, w_in, b_in3, l, 0, d_mix, "gelu", BF16, tile_rows, d_mix, "proj_u")
        v, v_new = _proj_gelu_ln(xb, w_in, b_in3, lw["a_ln_g"], lw["a_ln_b"], l, d_mix, d_mix,
                                 tile_rows, rows_s, "proj_v")
        hb = _proj(xb, w_in, b_in3, l, s1, d_mix, "none", F32, tile_rows, d_mix, "proj_b")
        c = _glu_proj(xb, w_in, b_in3, l, s2, d_mix, tile_rows, 512, "proj_c")
        gates = _proj_sigmoid_blocked(xb, w_in, b_in3, l, s3, N_BRANCH * d_model,
                                      tile_rows, 1024, MERGE_COLS, "proj_gates")

        pa, pb, pc = _mix(tile3(u), tile3(v), tile3(hb), tile3(c), pool_t, conv_t, lw, l, seq)
        flat = lambda a: a.reshape(m_rows, d_mix)
        merged = _merge(flat(pa), flat(pb), flat(pc), gates, w_a_out, w_b_out, w_c_out,
                        l, tile_rows, MERGE_COLS)
        x1 = _projout_ln(merged, w_out, x_sources, ln1_g3, ln1_b3, l, alpha, half_rows, 512)
        hb_l.append(tile3(hb))
        c_l.append(tile3(c))
        v_l.append(v_new)
        final = l + 1 == depth
        out_a, out_b = _ffn_ln(x1, w_ffn_up, w_ffn_down, ln2_g3, ln2_b3, l, alpha,
                               tile_rows, 256, seq, final)
        if final:
            y_prompt, y_sample3 = out_a, out_b
        else:
            x_sources, xb = (out_a,), out_b

    y_sample = y_sample3.reshape(m_sample, 1, d_model)
    new_pool_prompt = jnp.stack([a[:, seq - POOL_BUF:seq] for a in hb_l])
    new_conv_prompt = jnp.stack([a[:, seq - CONV_BUF:seq] for a in c_l])
    new_pool_sample = jnp.transpose(_state_shift(pool_t, hb_l, seq), (0, 2, 1, 3))
    new_conv_sample = jnp.transpose(_state_shift(conv_t, c_l, seq), (0, 2, 1, 3))
    new_chunk_v = jnp.stack(v_l)[:, :, None, :]
    return (y_prompt, y_sample, new_pool_prompt, new_conv_prompt,
            new_pool_sample, new_conv_sample, new_chunk_v)
```

```python
import functools

import jax
import jax.numpy as jnp
from jax import lax
from jax.experimental import pallas as pl
from jax.experimental.pallas import tpu as pltpu

F32 = jnp.float32
BF16 = jnp.bfloat16

LN_EPS = 1e-5
A_HEADS = 8
CHUNK = 128
POOL_WINDOWS = (2, 4, 8, 16)
POOL_BUF = 15
CONV_WIDTH = 31
CONV_BUF = CONV_WIDTH - 1
N_BRANCH = 3

SUBLANES = 8
LANES = 128
BF16_ROWS = 16
POOL_HALO = 16
CONV_HALO = 32
ROW_BLOCK = 64
MIX_SUB = 256
PACK_SUB = 512
LN_ROWS = 208
ROW_SPLITS = 5
FFN_ROW_SPLITS = 2
STAGE_SLOTS = 2
MERGE_COLS = 256
V7X_VMEM_BYTES = 64 * 1024 * 1024

_ONCE = pl.Buffered(1)


def _cparams(semantics, vmem_mb):
    assert vmem_mb * 1024 * 1024 < V7X_VMEM_BYTES
    return pltpu.CompilerParams(dimension_semantics=semantics,
                                vmem_limit_bytes=vmem_mb * 1024 * 1024)


def _layer_norm(x, g, b):
    mu = jnp.mean(x, axis=-1, keepdims=True)
    xc = x - mu
    var = jnp.mean(xc * xc, axis=-1, keepdims=True)
    return xc * lax.rsqrt(var + LN_EPS) * g + b


def _gelu_exact(x):
    return 0.5 * x * (1.0 + lax.erf(x * (0.5 ** 0.5)))


def _sigmoid(x):
    return 1.0 / (1.0 + jnp.exp(-x))


def _bdot(a, w):
    return jnp.dot(a, w.astype(BF16), preferred_element_type=F32)


def _residual_ln_chunk(x_rows, acc_ref, g_ref, b_ref, alpha, r):
    y = _layer_norm(alpha * x_rows + acc_ref[pl.ds(r, LN_ROWS), :], g_ref[...], b_ref[...])
    acc_ref[pl.ds(r, LN_ROWS), :] = y
    return y


def _residual_ln_rows(x_ref, acc_ref, g_ref, b_ref, alpha):
    def body(i, carry):
        r = pl.multiple_of(i * LN_ROWS, LN_ROWS)
        _residual_ln_chunk(x_ref[pl.ds(r, LN_ROWS), :], acc_ref, g_ref, b_ref, alpha, r)
        return carry

    lax.fori_loop(0, acc_ref.shape[0] // LN_ROWS, body, 0)


def _pack_kernel(xp_ref, xs_ref, o_ref, *, n_sub):
    t = pl.program_id(1)
    sub = xp_ref.shape[0]

    @pl.when(t < n_sub)
    def _():
        o_ref[pl.ds(pl.multiple_of(t * sub, sub), sub), :] = xp_ref[...].astype(BF16)

    @pl.when(t == n_sub)
    def _():
        o_ref[n_sub * sub:, :] = xs_ref[...].astype(BF16)


def _pack_bf16(x_prompt, x_sample3):
    batch, seq, d = x_prompt.shape
    rows_s = x_sample3.shape[1]
    n_sub = seq // PACK_SUB
    return pl.pallas_call(
        functools.partial(_pack_kernel, n_sub=n_sub),
        grid=(batch, n_sub + 1),
        in_specs=[
            pl.BlockSpec((None, PACK_SUB, d), lambda b, t: (b, jnp.minimum(t, n_sub - 1), 0)),
            pl.BlockSpec((None, rows_s, d), lambda b, t: (b, 0, 0)),
        ],
        out_specs=pl.BlockSpec((None, seq + rows_s, d), lambda b, t: (b, 0, 0)),
        out_shape=jax.ShapeDtypeStruct((batch, seq + rows_s, d), BF16),
        compiler_params=_cparams(("parallel", "arbitrary"), 40),
        name="pack",
    )(x_prompt, x_sample3)


def _proj_kernel(x_ref, w_ref, b_ref, o_ref, *, act):
    w = w_ref[...].astype(BF16)
    rows = x_ref.shape[0] // ROW_SPLITS
    for i in range(ROW_SPLITS):
        rs = slice(i * rows, (i + 1) * rows)
        h = jnp.dot(x_ref[rs, :], w, preferred_element_type=F32) + b_ref[...]
        if act == "gelu":
            h = _gelu_exact(h)
        elif act == "sigmoid":
            h = _sigmoid(h)
        o_ref[rs, :] = h.astype(o_ref.dtype)


def _proj_sigmoid_blocked_kernel(x_ref, w_ref, b_ref, o_ref):
    w = w_ref[...].astype(BF16)
    n_blk, tile_rows, sub = o_ref.shape
    rows = tile_rows // ROW_SPLITS
    for i in range(ROW_SPLITS):
        rs = slice(i * rows, (i + 1) * rows)
        h = _sigmoid(jnp.dot(x_ref[rs, :], w, preferred_element_type=F32) + b_ref[...])
        for j in range(n_blk):
            o_ref[j, rs, :] = h[:, j * sub:(j + 1) * sub].astype(o_ref.dtype)


def _proj_gelu_ln_kernel(x_ref, w_ref, b_ref, g_ref, beta_ref, o_ref, os_ref):
    w = w_ref[...].astype(BF16)
    rows = x_ref.shape[0] // ROW_SPLITS
    n_s = os_ref.shape[0]
    for i in range(ROW_SPLITS):
        rs = slice(i * rows, (i + 1) * rows)
        h = jnp.dot(x_ref[rs, :], w, preferred_element_type=F32) + b_ref[...]
        v = _layer_norm(_gelu_exact(h), g_ref[...], beta_ref[...])
        o_ref[rs, :] = v.astype(BF16)
        if i == ROW_SPLITS - 1:
            os_ref[...] = v[rows - n_s:, :]


def _glu_kernel(x_ref, wv_ref, wg_ref, bv_ref, bg_ref, o_ref):
    wv = wv_ref[...].astype(BF16)
    wg = wg_ref[...].astype(BF16)
    rows = x_ref.shape[0] // ROW_SPLITS
    for i in range(ROW_SPLITS):
        rs = slice(i * rows, (i + 1) * rows)
        x = x_ref[rs, :]
        val = jnp.dot(x, wv, preferred_element_type=F32) + bv_ref[...]
        gate = jnp.dot(x, wg, preferred_element_type=F32) + bg_ref[...]
        o_ref[rs, :] = val * _sigmoid(gate)


def _proj(x, w, b, layer, col0, ncols, act, out_dtype, tm, tn, name):
    m_rows, k = x.shape
    off = col0 // tn
    return pl.pallas_call(
        functools.partial(_proj_kernel, act=act),
        grid=(m_rows // tm, ncols // tn),
        in_specs=[
            pl.BlockSpec((tm, k), lambda m, n: (m, 0)),
            pl.BlockSpec((None, k, tn), lambda m, n: (layer, 0, off + n)),
            pl.BlockSpec((None, 1, tn), lambda m, n: (layer, 0, off + n)),
        ],
        out_specs=pl.BlockSpec((tm, tn), lambda m, n: (m, n)),
        out_shape=jax.ShapeDtypeStruct((m_rows, ncols), out_dtype),
        compiler_params=_cparams(("parallel", "arbitrary"), 56),
        name=name,
    )(x, w, b)


def _proj_sigmoid_blocked(x, w, b, layer, col0, ncols, tm, tn, sub, name):
    m_rows, k = x.shape
    off = col0 // tn
    return pl.pallas_call(
        _proj_sigmoid_blocked_kernel,
        grid=(m_rows // tm, ncols // tn),
        in_specs=[
            pl.BlockSpec((tm, k), lambda m, n: (m, 0)),
            pl.BlockSpec((None, k, tn), lambda m, n: (layer, 0, off + n)),
            pl.BlockSpec((None, 1, tn), lambda m, n: (layer, 0, off + n)),
        ],
        out_specs=pl.BlockSpec((tn // sub, tm, sub), lambda m, n: (n, m, 0)),
        out_shape=jax.ShapeDtypeStruct((ncols // sub, m_rows, sub), BF16),
        compiler_params=_cparams(("parallel", "arbitrary"), 56),
        name=name,
    )(x, w, b)


def _proj_gelu_ln(x, w, b, g, beta, layer, col0, ncols, tm, rows_s, name):
    m_rows, k = x.shape
    off = col0 // ncols
    vec = lambda: pl.BlockSpec((None, 1, ncols), lambda m: (layer, 0, 0))
    return pl.pallas_call(
        _proj_gelu_ln_kernel,
        grid=(m_rows // tm,),
        in_specs=[
            pl.BlockSpec((tm, k), lambda m: (m, 0)),
            pl.BlockSpec((None, k, ncols), lambda m: (layer, 0, off)),
            pl.BlockSpec((None, 1, ncols), lambda m: (layer, 0, off)),
            vec(), vec(),
        ],
        out_specs=[pl.BlockSpec((tm, ncols), lambda m: (m, 0)),
                   pl.BlockSpec((rows_s, ncols), lambda m: (m, 0))],
        out_shape=[jax.ShapeDtypeStruct((m_rows, ncols), BF16),
                   jax.ShapeDtypeStruct((m_rows // tm * rows_s, ncols), F32)],
        compiler_params=_cparams(("parallel",), 52),
        name=name,
    )(x, w, b, g, beta)


def _glu_proj(x, w, b, layer, col0, ncols, tm, tn, name):
    m_rows, k = x.shape
    off_v = col0 // tn
    off_g = (col0 + ncols) // tn
    return pl.pallas_call(
        _glu_kernel,
        grid=(m_rows // tm, ncols // tn),
        in_specs=[
            pl.BlockSpec((tm, k), lambda m, n: (m, 0)),
            pl.BlockSpec((None, k, tn), lambda m, n: (layer, 0, off_v + n)),
            pl.BlockSpec((None, k, tn), lambda m, n: (layer, 0, off_g + n)),
            pl.BlockSpec((None, 1, tn), lambda m, n: (layer, 0, off_v + n)),
            pl.BlockSpec((None, 1, tn), lambda m, n: (layer, 0, off_g + n)),
        ],
        out_specs=pl.BlockSpec((tm, tn), lambda m, n: (m, n)),
        out_shape=jax.ShapeDtypeStruct((m_rows, ncols), F32),
        compiler_params=_cparams(("parallel", "arbitrary"), 56),
        name=name,
    )(x, w, w, b, b)


def _trailing_sum(x, w):
    result, offset, part, span = None, 0, x, 1
    while True:
        if w & span:
            term = part if offset == 0 else pltpu.roll(part, offset, axis=0)
            result = term if result is None else result + term
            offset += span
        if span * 2 > w:
            return result
        part = part + pltpu.roll(part, span, axis=0)
        span *= 2


def _conv_rows(x, wdw_ref, bias, ls):
    lead = CONV_HALO - CONV_BUF
    acc = jnp.broadcast_to(bias, (ROW_BLOCK, LANES))
    for r in range(SUBLANES):
        taps = [k for k in range(CONV_WIDTH) if (lead + k) % SUBLANES == r]
        if not taps:
            continue
        xr = x if r == 0 else pltpu.roll(x, x.shape[0] - r, axis=0)
        for k in taps:
            q = (lead + k) // SUBLANES
            acc = acc + wdw_ref[k:k + 1, ls] * xr[SUBLANES * q:SUBLANES * q + ROW_BLOCK]
    return acc


def _mix_prompt_step(t, u_ref, v_ref, hb_ref, hbh_ref, c_ref, ch_ref,
                     ws_ref, bsb_ref, wgrp_ref, bsc_ref,
                     wdw_ref, bdw_ref, clg_ref, clb_ref,
                     pa_ref, pb_ref, pc_ref, hb_ext, c_ext, pool_scr, conv_scr):
    sub, d_mix = v_ref.shape
    n_chunks = sub // CHUNK
    head_dim = d_mix // A_HEADS
    group_dim = d_mix // len(POOL_WINDOWS)
    base = pl.multiple_of(t * sub, sub)

    vb = v_ref[...]
    row = lax.broadcasted_iota(jnp.int32, (CHUNK, CHUNK), 0)
    col = lax.broadcasted_iota(jnp.int32, (CHUNK, CHUNK), 1)
    for h in range(A_HEADS):
        hs = slice(h * head_dim, (h + 1) * head_dim)
        wm = jnp.where(row >= col, ws_ref[h], 0.0).astype(BF16)
        rhs = jnp.concatenate(
            [vb[ci * CHUNK:(ci + 1) * CHUNK, hs] for ci in range(n_chunks)], axis=1)
        s = jnp.dot(wm, rhs, preferred_element_type=F32)
        bias = bsb_ref[h]
        for ci in range(n_chunks):
            rs = slice(ci * CHUNK, (ci + 1) * CHUNK)
            s_c = s[:, ci * head_dim:(ci + 1) * head_dim] + bias
            pa_ref[pl.ds(base + ci * CHUNK, CHUNK), hs] = (u_ref[rs, hs] * s_c).astype(BF16)

    first = t == 0
    hb_ext[0:POOL_HALO, :] = jnp.where(first, 0.0, hbh_ref[...])
    hb_ext[POOL_HALO:POOL_HALO + sub, :] = hb_ref[...]

    def pool_body(i, carry):
        r0 = pl.multiple_of(i * ROW_BLOCK, ROW_BLOCK)
        pos = base + r0 + lax.broadcasted_iota(jnp.int32, (ROW_BLOCK, group_dim), 0)
        for g, w in enumerate(POOL_WINDOWS):
            gs = slice(g * group_dim, (g + 1) * group_dim)
            x = hb_ext[pl.ds(r0, POOL_HALO + ROW_BLOCK), gs]
            tok = x[POOL_HALO:POOL_HALO + ROW_BLOCK]
            win = _trailing_sum(x, w)[POOL_HALO:POOL_HALO + ROW_BLOCK]
            cnt = jnp.minimum(pos + 1, w).astype(F32)
            pool_scr[pl.ds(r0, ROW_BLOCK), gs] = win / cnt - tok
        return carry

    lax.fori_loop(0, sub // ROW_BLOCK, pool_body, 0)
    for g in range(len(POOL_WINDOWS)):
        gs = slice(g * group_dim, (g + 1) * group_dim)
        mixed = _bdot(pool_scr[:, gs].astype(BF16), wgrp_ref[g])
        pb_ref[pl.ds(base, sub), gs] = (mixed * bsc_ref[:, gs]).astype(BF16)

    c_ext[0:CONV_HALO, :] = jnp.where(first, 0.0, ch_ref[...])
    c_ext[CONV_HALO:CONV_HALO + sub, :] = c_ref[...]

    def conv_body(i, carry):
        r0 = pl.multiple_of(i * ROW_BLOCK, ROW_BLOCK)
        for lt in range(d_mix // LANES):
            ls = slice(lt * LANES, (lt + 1) * LANES)
            window = c_ext[pl.ds(r0, CONV_HALO + ROW_BLOCK), ls]
            conv_scr[pl.ds(r0, ROW_BLOCK), ls] = _conv_rows(window, wdw_ref, bdw_ref[:, ls], ls)
        return carry

    lax.fori_loop(0, sub // ROW_BLOCK, conv_body, 0)
    y = _layer_norm(conv_scr[...], clg_ref[...], clb_ref[...])
    pc_ref[pl.ds(base, sub), :] = (y * _sigmoid(y)).astype(BF16)


def _mix_sample_step(row0, u_ref, v_ref, hb_ref, c_ref, sp_ref, sc_ref,
                     ws0_ref, bs0_ref, wgrp_ref, bsc_ref,
                     wdw_ref, bdw_ref, clg_ref, clb_ref,
                     pa_ref, pb_ref, pc_ref):
    rows, d_mix = v_ref.shape
    group_dim = d_mix // len(POOL_WINDOWS)
    out_rows = slice(row0, row0 + rows)

    s = v_ref[...].astype(F32) * ws0_ref[...].astype(BF16).astype(F32) + bs0_ref[...]
    pa_ref[out_rows, :] = (u_ref[...] * s).astype(BF16)

    for g, w in enumerate(POOL_WINDOWS):
        gs = slice(g * group_dim, (g + 1) * group_dim)
        tok = hb_ref[:, gs]
        win = tok
        for k in range(POOL_BUF - (w - 1), POOL_BUF):
            win = win + sp_ref[k, :, gs]
        pooled = win / float(w) - tok
        mixed = _bdot(pooled.astype(BF16), wgrp_ref[g])
        pb_ref[out_rows, gs] = (mixed * bsc_ref[:, gs]).astype(BF16)

    conv = c_ref[...] * wdw_ref[CONV_BUF:CONV_WIDTH, :] + bdw_ref[...]
    for k in range(CONV_BUF):
        conv = conv + sc_ref[k] * wdw_ref[k:k + 1, :]
    y = _layer_norm(conv, clg_ref[...], clb_ref[...])
    pc_ref[out_rows, :] = (y * _sigmoid(y)).astype(BF16)


def _mix_kernel(u_ref, v_ref, hb_ref, hbh_ref, c_ref, ch_ref,
                us_ref, vs_ref, hbs_ref, cs_ref, sp_ref, sc_ref,
                ws_ref, bsb_ref, ws0_ref, bs0_ref, wgrp_ref, bsc_ref,
                wdw_ref, bdw_ref, clg_ref, clb_ref,
                pa_ref, pb_ref, pc_ref,
                hb_ext, c_ext, pool_scr, conv_scr, *, n_sub):
    t = pl.program_id(1)

    @pl.when(t < n_sub)
    def _():
        _mix_prompt_step(t, u_ref, v_ref, hb_ref, hbh_ref, c_ref, ch_ref,
                         ws_ref, bsb_ref, wgrp_ref, bsc_ref,
                         wdw_ref, bdw_ref, clg_ref, clb_ref,
                         pa_ref, pb_ref, pc_ref, hb_ext, c_ext, pool_scr, conv_scr)

    @pl.when(t == n_sub)
    def _():
        _mix_sample_step(n_sub * v_ref.shape[0], us_ref, vs_ref, hbs_ref, cs_ref, sp_ref, sc_ref,
                         ws0_ref, bs0_ref, wgrp_ref, bsc_ref,
                         wdw_ref, bdw_ref, clg_ref, clb_ref,
                         pa_ref, pb_ref, pc_ref)


def _mix(u3, v3, hb3, c3, state_pool, state_conv, lw, layer, seq):
    batch, tile_rows, d_mix = hb3.shape
    rows_s = tile_rows - seq
    n_sub = seq // MIX_SUB
    sub_idx = lambda t: jnp.minimum(t, n_sub - 1)
    vec = lambda: pl.BlockSpec((None, 1, d_mix), lambda b, t: (layer, 0, 0))
    sub_spec = lambda: pl.BlockSpec((None, MIX_SUB, d_mix), lambda b, t: (b, sub_idx(t), 0))
    smp_spec = lambda: pl.BlockSpec((None, rows_s, d_mix), lambda b, t: (b, seq // rows_s, 0))

    def halo_spec(rows):
        per = MIX_SUB // rows
        return pl.BlockSpec((None, rows, d_mix),
                            lambda b, t: (b, jnp.maximum(sub_idx(t) * per - 1, 0), 0))

    def state_spec(buf):
        return pl.BlockSpec((None, buf, rows_s, d_mix), lambda b, t: (layer, 0, b, 0),
                            pipeline_mode=_ONCE)

    head_spec = lambda: pl.BlockSpec((None, A_HEADS, CHUNK, CHUNK), lambda b, t: (layer, 0, 0, 0))
    tile_out = pl.BlockSpec((None, tile_rows, d_mix), lambda b, t: (b, 0, 0))
    out3 = jax.ShapeDtypeStruct((batch, tile_rows, d_mix), BF16)
    return pl.pallas_call(
        functools.partial(_mix_kernel, n_sub=n_sub),
        grid=(batch, n_sub + 1),
        in_specs=[
            sub_spec(), sub_spec(),
            sub_spec(), halo_spec(POOL_HALO),
            sub_spec(), halo_spec(CONV_HALO),
            smp_spec(), smp_spec(), smp_spec(), smp_spec(),
            state_spec(POOL_BUF), state_spec(CONV_BUF),
            head_spec(), head_spec(),
            vec(), vec(),
            pl.BlockSpec((None,) + lw["b_w_group"].shape[1:], lambda b, t: (layer, 0, 0, 0)),
            vec(),
            pl.BlockSpec((None, CONV_WIDTH, d_mix), lambda b, t: (layer, 0, 0)),
            vec(), vec(), vec(),
        ],
        out_specs=[tile_out, tile_out, tile_out],
        out_shape=[out3, out3, out3],
        scratch_shapes=[
            pltpu.VMEM((POOL_HALO + MIX_SUB, d_mix), F32),
            pltpu.VMEM((CONV_HALO + MIX_SUB, d_mix), F32),
            pltpu.VMEM((MIX_SUB, d_mix), F32),
            pltpu.VMEM((MIX_SUB, d_mix), F32),
        ],
        compiler_params=_cparams(("parallel", "arbitrary"), 56),
        name="mix",
    )(u3, v3, hb3, hb3, c3, c3, u3, v3, hb3, c3, state_pool, state_conv,
      lw["a_ws"], lw["a_bs_b"], lw["a_ws0"], lw["a_bs0"],
      lw["b_w_group"], lw["b_scale"], lw["c_w_dw"], lw["c_b_dw"], lw["c_ln_g"], lw["c_ln_b"])


def _merge_kernel(pa_ref, pb_ref, pc_ref, wa_ref, wb_ref, wc_ref,
                  g0_ref, g1_ref, g2_ref, o_ref):
    wa = wa_ref[...].astype(BF16)
    wb = wb_ref[...].astype(BF16)
    wc = wc_ref[...].astype(BF16)
    rows = o_ref.shape[0] // ROW_SPLITS
    for i in range(ROW_SPLITS):
        rs = slice(i * rows, (i + 1) * rows)
        merged = (g0_ref[rs, :] * jnp.dot(pa_ref[rs, :], wa, preferred_element_type=F32)
                  + g1_ref[rs, :] * jnp.dot(pb_ref[rs, :], wb, preferred_element_type=F32)
                  + g2_ref[rs, :] * jnp.dot(pc_ref[rs, :], wc, preferred_element_type=F32))
        o_ref[rs, :] = merged.astype(o_ref.dtype)


def _merge(pa, pb, pc, gates_blocked, w_a, w_b, w_c, layer, tm, tn):
    m_rows, d_mix = pa.shape
    d_model = w_a.shape[-1]
    nb = d_model // tn
    assert gates_blocked.shape == (N_BRANCH * nb, m_rows, tn)
    act = lambda: pl.BlockSpec((tm, d_mix), lambda m, n: (m, 0))
    wsp = lambda: pl.BlockSpec((None, d_mix, tn), lambda m, n: (layer, 0, n))
    gsp = lambda j: pl.BlockSpec((None, tm, tn), lambda m, n: (j * nb + n, m, 0))
    return pl.pallas_call(
        _merge_kernel,
        grid=(m_rows // tm, nb),
        in_specs=[act(), act(), act(), wsp(), wsp(), wsp(), gsp(0), gsp(1), gsp(2)],
        out_specs=pl.BlockSpec((tm, tn), lambda m, n: (m, n)),
        out_shape=jax.ShapeDtypeStruct((m_rows, d_model), BF16),
        compiler_params=_cparams(("parallel", "arbitrary"), 52),
        name="merge",
    )(pa, pb, pc, w_a, w_b, w_c, gates_blocked, gates_blocked, gates_blocked)


def _projout_accumulate(m_ref, w_ref, o_ref):
    @pl.when(pl.program_id(2) == 0)
    def _():
        o_ref[...] = jnp.zeros_like(o_ref)

    o_ref[...] += _bdot(m_ref[...], w_ref[...])


def _residual_copies(x_hbm, xbuf, sems, tile, half):
    tm = xbuf.shape[0]
    if len(x_hbm) == 1:
        row0 = pl.multiple_of((tile * 2 + half) * tm, tm)
        return [pltpu.make_async_copy(x_hbm[0].at[pl.ds(row0, tm)], xbuf, sems.at[0])]
    xp, xs = x_hbm
    if half == 0:
        return [pltpu.make_async_copy(xp.at[tile, pl.ds(0, tm)], xbuf, sems.at[0])]
    n_prompt = xp.shape[1] - tm
    return [pltpu.make_async_copy(xp.at[tile, pl.ds(tm, n_prompt)],
                                  xbuf.at[pl.ds(0, n_prompt)], sems.at[1]),
            pltpu.make_async_copy(xs.at[tile], xbuf.at[pl.ds(n_prompt, xs.shape[1])], sems.at[2])]


def _projout_ln_kernel(m_ref, w_ref, *refs, alpha, n_x):
    x_hbm, (g_ref, b_ref, o_ref, xbuf, sems) = refs[:n_x], refs[n_x:]
    tile, h, k = pl.program_id(0), pl.program_id(1), pl.program_id(2)
    for half in range(2):
        @pl.when(jnp.logical_and(k == 0, h == half))
        def _():
            for cp in _residual_copies(x_hbm, xbuf, sems, tile, half):
                cp.start(priority=1)

    _projout_accumulate(m_ref, w_ref, o_ref)

    for half in range(2):
        @pl.when(jnp.logical_and(k == pl.num_programs(2) - 1, h == half))
        def _():
            for cp in _residual_copies(x_hbm, xbuf, sems, tile, half):
                cp.wait()
            _residual_ln_rows(xbuf, o_ref, g_ref, b_ref, alpha)


def _projout_ln(merged, w_out, x_sources, g, b, layer, alpha, tm, tk):
    m_rows = merged.shape[0]
    d = w_out.shape[-1]
    batch = m_rows // (2 * tm)
    vec = lambda: pl.BlockSpec((None, 1, d), lambda bb, h, k: (layer, 0, 0))
    row_idx = lambda bb, h: bb * 2 + h
    return pl.pallas_call(
        functools.partial(_projout_ln_kernel, alpha=alpha, n_x=len(x_sources)),
        grid=(batch, 2, merged.shape[1] // tk),
        in_specs=[
            pl.BlockSpec((tm, tk), lambda bb, h, k: (row_idx(bb, h), k)),
            pl.BlockSpec((None, tk, d), lambda bb, h, k: (layer, k, 0)),
            *[pl.BlockSpec(memory_space=pl.ANY) for _ in x_sources],
            vec(), vec(),
        ],
        out_specs=pl.BlockSpec((tm, d), lambda bb, h, k: (row_idx(bb, h), 0)),
        out_shape=jax.ShapeDtypeStruct((m_rows, d), F32),
        scratch_shapes=[pltpu.VMEM((tm, d), F32), pltpu.SemaphoreType.DMA((3,))],
        compiler_params=_cparams(("arbitrary", "arbitrary", "arbitrary"), 48),
        name="projout_ln",
    )(merged, w_out, *x_sources, g, b)


def _ffn_row_copies(acc, stage, out_a, out_b, tile, sems, seq, final):
    tm = acc.shape[0]
    per_chunk, n_sem = [], 0
    for i in range(tm // LN_ROWS):
        r0, r1 = i * LN_ROWS, (i + 1) * LN_ROWS
        pieces = []
        if not final:
            row0 = pl.multiple_of(tile * tm, tm)
            for src, dst in ((acc.at[pl.ds(r0, LN_ROWS)], out_a), (stage.at[i % STAGE_SLOTS], out_b)):
                pieces.append(pltpu.make_async_copy(src, dst.at[pl.ds(row0 + r0, LN_ROWS)],
                                                    sems.at[n_sem]))
                n_sem += 1
        else:
            if r0 < seq:
                n = min(r1, seq) - r0
                pieces.append(pltpu.make_async_copy(acc.at[pl.ds(r0, n)],
                                                    out_a.at[tile, pl.ds(r0, n)], sems.at[n_sem]))
                n_sem += 1
            if r1 > seq:
                s0 = max(r0, seq)
                pieces.append(pltpu.make_async_copy(acc.at[pl.ds(s0, r1 - s0)],
                                                    out_b.at[tile, pl.ds(s0 - seq, r1 - s0)],
                                                    sems.at[n_sem]))
                n_sem += 1
        per_chunk.append(pieces)
    return per_chunk


def _ffn_ln_kernel(x_ref, wg_ref, wu_ref, wd_ref, g_ref, b_ref, out_a, out_b, acc, *scratch,
                   alpha, seq, final):
    stage, sems = (None, scratch[0]) if final else scratch
    f = pl.program_id(1)

    def hidden_tile(first):
        wg = wg_ref[...].astype(BF16)
        wu = wu_ref[...].astype(BF16)
        wd = wd_ref[...].astype(BF16)
        rows = acc.shape[0] // FFN_ROW_SPLITS
        for i in range(FFN_ROW_SPLITS):
            rs = slice(i * rows, (i + 1) * rows)
            x = x_ref[rs, :].astype(BF16)
            gate = jnp.dot(x, wg, preferred_element_type=F32)
            up = jnp.dot(x, wu, preferred_element_type=F32)
            hid = (gate * _sigmoid(gate) * up).astype(BF16)
            part = jnp.dot(hid, wd, preferred_element_type=F32)
            if first:
                acc[rs, :] = part
            else:
                acc[rs, :] += part

    pl.when(f == 0)(functools.partial(hidden_tile, True))
    pl.when(f > 0)(functools.partial(hidden_tile, False))

    @pl.when(f == pl.num_programs(1) - 1)
    def _():
        per_chunk = _ffn_row_copies(acc, stage, out_a, out_b, pl.program_id(0), sems, seq, final)
        waited = set()
        for i, pieces in enumerate(per_chunk):
            r = i * LN_ROWS
            if not final and i >= STAGE_SLOTS:
                per_chunk[i - STAGE_SLOTS][1].wait()
                waited.add((i - STAGE_SLOTS, 1))
            y = _residual_ln_chunk(x_ref[r:r + LN_ROWS, :], acc, g_ref, b_ref, alpha, r)
            if not final:
                stage[i % STAGE_SLOTS] = y.astype(BF16)
            for j, cp in enumerate(pieces):
                cp.start(priority=(i + j) % 2)
        for i, pieces in enumerate(per_chunk):
            for j, cp in enumerate(pieces):
                if (i, j) not in waited:
                    cp.wait()


def _ffn_ln(x, w_up, w_down, g, b, layer, alpha, tm, tf, seq, final):
    m_rows, d = x.shape
    batch = m_rows // tm
    d_ff = w_down.shape[1]
    nf = d_ff // tf
    n_chunks = tm // LN_ROWS
    vec = lambda: pl.BlockSpec((None, 1, d), lambda m, f: (layer, 0, 0))
    if final:
        out_shape = [jax.ShapeDtypeStruct((batch, seq, d), F32),
                     jax.ShapeDtypeStruct((batch, tm - seq, d), F32)]
        scratch = [pltpu.VMEM((tm, d), F32), pltpu.SemaphoreType.DMA((2 * n_chunks,))]
    else:
        out_shape = [jax.ShapeDtypeStruct((m_rows, d), F32), jax.ShapeDtypeStruct((m_rows, d), BF16)]
        scratch = [pltpu.VMEM((tm, d), F32), pltpu.VMEM((STAGE_SLOTS, LN_ROWS, d), BF16),
                   pltpu.SemaphoreType.DMA((2 * n_chunks,))]
    return pl.pallas_call(
        functools.partial(_ffn_ln_kernel, alpha=alpha, seq=seq, final=final),
        grid=(batch, nf),
        in_specs=[
            pl.BlockSpec((tm, d), lambda m, f: (m, 0), pipeline_mode=_ONCE),
            pl.BlockSpec((None, d, tf), lambda m, f: (layer, 0, f)),
            pl.BlockSpec((None, d, tf), lambda m, f: (layer, 0, nf + f)),
            pl.BlockSpec((None, tf, d), lambda m, f: (layer, f, 0)),
            vec(), vec(),
        ],
        out_specs=[pl.BlockSpec(memory_space=pl.ANY), pl.BlockSpec(memory_space=pl.ANY)],
        out_shape=out_shape,
        scratch_shapes=scratch,
        compiler_params=_cparams(("arbitrary", "arbitrary"), 60),
        name="ffn_ln",
    )(x, w_up, w_up, w_down, g, b)


def _state_shift_kernel(s_ref, *refs):
    new_refs, o_ref = refs[:-1], refs[-1]
    layer = pl.program_id(0)
    keep = s_ref.shape[0] - 1
    new = new_refs[0][...]
    for j in range(1, len(new_refs)):
        new = jnp.where(layer == j, new_refs[j][...], new)
    o_ref[0:keep] = s_ref[1:keep + 1]
    o_ref[keep] = new


def _state_shift(state_t, new_rows3, seq):
    depth, buf, _, d_mix = state_t.shape
    batch, tile_rows, _ = new_rows3[0].shape
    rows_s = tile_rows - seq
    blk = pl.BlockSpec((None, buf, rows_s, d_mix), lambda l, b: (l, 0, b, 0))
    new_spec = pl.BlockSpec((None, rows_s, d_mix), lambda l, b: (b, seq // rows_s, 0))
    return pl.pallas_call(
        _state_shift_kernel,
        grid=(depth, batch),
        in_specs=[blk] + [new_spec] * depth,
        out_specs=blk,
        out_shape=jax.ShapeDtypeStruct(state_t.shape, state_t.dtype),
        compiler_params=_cparams(("parallel", "parallel"), 32),
        name="state_shift",
    )(state_t, *new_rows3)


def kernel(x_prompt, x_sample, state_pool, state_conv, w_in, b_in, a_ln_g, a_ln_b, a_ws, a_bs, w_a_out, b_w_group, b_scale, w_b_out, c_w_dw, c_b_dw, c_ln_g, c_ln_b, w_c_out, w_out, ln1_g, ln1_b, w_ffn_up, w_ffn_down, ln2_g, ln2_b):
    batch, seq, d_model = x_prompt.shape
    m_sample = x_sample.shape[0] * x_sample.shape[1]
    depth = w_in.shape[0]
    d_mix = a_ln_g.shape[-1]
    head_dim = d_mix // A_HEADS
    alpha = (2.0 * depth) ** 0.25

    assert x_sample.shape[1] == 1 and m_sample % batch == 0
    rows_s = m_sample // batch
    tile_rows = seq + rows_s
    m_rows = batch * tile_rows
    half_rows = tile_rows // 2
    assert rows_s % BF16_ROWS == 0 and seq % rows_s == 0 and seq % MIX_SUB == 0
    assert seq % PACK_SUB == 0
    assert tile_rows % 2 == 0 and half_rows % LN_ROWS == 0 and LN_ROWS % BF16_ROWS == 0
    assert rows_s <= LN_ROWS and (LN_ROWS - rows_s) % SUBLANES == 0
    assert tile_rows % (ROW_SPLITS * BF16_ROWS) == 0 and rows_s <= tile_rows // ROW_SPLITS

    vec3 = lambda a: a.reshape(depth, 1, a.shape[-1])
    lw = {
        "a_ln_g": vec3(a_ln_g), "a_ln_b": vec3(a_ln_b),
        "a_ws": a_ws,
        "a_bs_b": jnp.broadcast_to(a_bs[..., None], a_bs.shape + (head_dim,)),
        "a_ws0": jnp.repeat(a_ws[:, :, 0, 0], head_dim, axis=-1).reshape(depth, 1, d_mix),
        "a_bs0": jnp.repeat(a_bs[:, :, 0], head_dim, axis=-1).reshape(depth, 1, d_mix),
        "b_w_group": b_w_group, "b_scale": vec3(b_scale),
        "c_w_dw": c_w_dw, "c_b_dw": vec3(c_b_dw),
        "c_ln_g": vec3(c_ln_g), "c_ln_b": vec3(c_ln_b),
    }
    b_in3 = vec3(b_in)
    ln1_g3, ln1_b3, ln2_g3, ln2_b3 = vec3(ln1_g), vec3(ln1_b), vec3(ln2_g), vec3(ln2_b)

    s1 = 2 * d_mix
    s2 = s1 + d_mix
    s3 = s2 + 2 * d_mix

    x_sample3 = x_sample.reshape(batch, rows_s, d_model)
    x_sources = (x_prompt, x_sample3)
    xb = _pack_bf16(x_prompt, x_sample3).reshape(m_rows, d_model)
    tile3 = lambda a: a.reshape(batch, tile_rows, a.shape[-1])
    pool_t = jnp.transpose(state_pool, (0, 2, 1, 3))
    conv_t = jnp.transpose(state_conv, (0, 2, 1, 3))

    hb_l, c_l, v_l = [], [], []
    for l in range(depth):
        u = _proj(xb, w_in, b_in3, l, 0, d_mix, "gelu", BF16, tile_rows, d_mix, "proj_u")
        v, v_new = _proj_gelu_ln(xb, w_in, b_in3, lw["a_ln_g"], lw["a_ln_b"], l, d_mix, d_mix,
                                 tile_rows, rows_s, "proj_v")
        hb = _proj(xb, w_in, b_in3, l, s1, d_mix, "none", F32, tile_rows, d_mix, "proj_b")
        c = _glu_proj(xb, w_in, b_in3, l, s2, d_mix, tile_rows, 512, "proj_c")
        gates = _proj_sigmoid_blocked(xb, w_in, b_in3, l, s3, N_BRANCH * d_model,
                                      tile_rows, 1024, MERGE_COLS, "proj_gates")

        pa, pb, pc = _mix(tile3(u), tile3(v), tile3(hb), tile3(c), pool_t, conv_t, lw, l, seq)
        flat = lambda a: a.reshape(m_rows, d_mix)
        merged = _merge(flat(pa), flat(pb), flat(pc), gates, w_a_out, w_b_out, w_c_out,
                        l, tile_rows, MERGE_COLS)
        x1 = _projout_ln(merged, w_out, x_sources, ln1_g3, ln1_b3, l, alpha, half_rows, 512)
        hb_l.append(tile3(hb))
        c_l.append(tile3(c))
        v_l.append(v_new)
        final = l + 1 == depth
        out_a, out_b = _ffn_ln(x1, w_ffn_up, w_ffn_down, ln2_g3, ln2_b3, l, alpha,
                               tile_rows, 256, seq, final)
        if final:
            y_prompt, y_sample3 = out_a, out_b
        else:
            x_sources, xb = (out_a,), out_b

    y_sample = y_sample3.reshape(m_sample, 1, d_model)
    new_pool_prompt = jnp.stack([a[:, seq - POOL_BUF:seq] for a in hb_l])
    new_conv_prompt = jnp.stack([a[:, seq - CONV_BUF:seq] for a in c_l])
    new_pool_sample = jnp.transpose(_state_shift(pool_t, hb_l, seq), (0, 2, 1, 3))
    new_conv_sample = jnp.transpose(_state_shift(conv_t, c_l, seq), (0, 2, 1, 3))
    new_chunk_v = jnp.stack(v_l)[:, :, None, :]
    return (y_prompt, y_sample, new_pool_prompt, new_conv_prompt,
            new_pool_sample, new_conv_sample, new_chunk_v)
```
